```python
import jax, jax.numpy as jnp
from jax import lax
import numpy as np

D_MODEL = 1024
BATCH = 2
SEQ = 16384
DEPTH = 1

HEAD_DIM = 64
ATTN_HEADS = 8
ATTN_WIDTH = ATTN_HEADS * HEAD_DIM
MOBA_BLOCK = 256
MOBA_TOPK = 3
Q_CHUNK = 32
ROPE_THETA = 10000.0

RWKV_HEADS = 8
RWKV_WIDTH = RWKV_HEADS * HEAD_DIM
DECAY_LORA = 64
AAA_LORA = 64
GATE_LORA = 160
GN_EPS = 64e-5

N_EXPERTS = 32
TOP_K = 4
D_EXPERT = 1024
SWIGLU_ALPHA = 1.702
SWIGLU_LIMIT = 7.0
EXPERT_BLOCK = 128

LN_EPS = 1e-5
DEEPNORM_ALPHA = (2 * DEPTH) ** 0.25
DEEPNORM_BETA = (8 * DEPTH) ** -0.25

OFF_R = 0
OFF_K = OFF_R + RWKV_WIDTH
OFF_V = OFF_K + RWKV_WIDTH
OFF_WD = OFF_V + RWKV_WIDTH
OFF_AD = OFF_WD + DECAY_LORA
OFF_GD = OFF_AD + AAA_LORA
SHIFT_COLS = OFF_GD + GATE_LORA
OFF_Q = SHIFT_COLS
OFF_AK = OFF_Q + ATTN_WIDTH
OFF_AV = OFF_AK + ATTN_WIDTH
OFF_GATE_A = OFF_AV + ATTN_WIDTH
OFF_GATE_R = OFF_GATE_A + D_MODEL
IN_COLS = OFF_GATE_R + D_MODEL

kernel_name = 'hybrid_moba_rwkv7_moe_deepnorm'


def layer_norm(x, g, b):
    xf = x.astype(jnp.float32)
    mu = xf.mean(-1, keepdims=True)
    var = jnp.square(xf - mu).mean(-1, keepdims=True)
    return ((xf - mu) * lax.rsqrt(var + LN_EPS) * g + b).astype(x.dtype)


def token_shift(t, mu):
    prev = jnp.pad(t, ((0, 0), (1, 0), (0, 0)))[:, :-1]
    return t + (prev - t) * mu


def rope(t):
    s = t.shape[1]
    half = HEAD_DIM // 2
    inv_freq = ROPE_THETA ** (-jnp.arange(half, dtype=jnp.float32) / half)
    ang = jnp.arange(s, dtype=jnp.float32)[:, None] * inv_freq[None, :]
    cos = jnp.cos(ang)[None, :, None, :]
    sin = jnp.sin(ang)[None, :, None, :]
    t1 = t[..., :half].astype(jnp.float32)
    t2 = t[..., half:].astype(jnp.float32)
    return jnp.concatenate([t1 * cos - t2 * sin, t2 * cos + t1 * sin], -1).astype(t.dtype)


def moba_attention(q, k, v):
    b, s, h, dh = q.shape
    nb = -(-s // MOBA_BLOCK)
    s_pad = nb * MOBA_BLOCK
    q, k = rope(q), rope(k)
    pad = ((0, 0), (0, s_pad - s), (0, 0), (0, 0))
    q, k, v = [jnp.pad(t, pad).transpose(0, 2, 1, 3) for t in (q, k, v)]
    kb = k.reshape(b, h, nb, MOBA_BLOCK, dh)
    vb = v.reshape(b, h, nb, MOBA_BLOCK, dh)
    k_mean = kb.astype(jnp.float32).mean(axis=3)
    topk = min(MOBA_TOPK, nb)
    n_chunks = s_pad // Q_CHUNK
    q_chunks = q.reshape(b, h, n_chunks, Q_CHUNK, dh).transpose(2, 0, 1, 3, 4)
    scale = dh ** -0.5
    gather = jax.vmap(jax.vmap(lambda blocks, idx: blocks[idx]))

    def one_chunk(args):
        c, qc = args
        blk = (c * Q_CHUNK) // MOBA_BLOCK
        q_pos = c * Q_CHUNK + jnp.arange(Q_CHUNK)
        gate = jnp.einsum('bhqd,bhnd->bhqn', qc.astype(jnp.float32), k_mean)
        gate = jnp.where(jnp.arange(nb) < blk, gate, -jnp.inf)
        _, sel = lax.top_k(gate, topk)
        sel_ok = sel < blk
        k_sel = gather(kb, sel)
        v_sel = gather(vb, sel)
        s_sel = jnp.einsum('bhqd,bhqjkd->bhqjk', qc, k_sel).astype(jnp.float32) * scale
        s_sel = jnp.where(sel_ok[..., None], s_sel, -jnp.inf).reshape(b, h, Q_CHUNK, topk * MOBA_BLOCK)
        k_own = lax.dynamic_index_in_dim(kb, blk, axis=2, keepdims=False)
        v_own = lax.dynamic_index_in_dim(vb, blk, axis=2, keepdims=False)
        s_own = jnp.einsum('bhqd,bhkd->bhqk', qc, k_own).astype(jnp.float32) * scale
        key_pos = blk * MOBA_BLOCK + jnp.arange(MOBA_BLOCK)
        s_own = jnp.where(key_pos[None, :] <= q_pos[:, None], s_own, -jnp.inf)
        p = jax.nn.softmax(jnp.concatenate([s_sel, s_own], -1), axis=-1).astype(v.dtype)
        p_sel = p[..., :topk * MOBA_BLOCK].reshape(b, h, Q_CHUNK, topk, MOBA_BLOCK)
        p_own = p[..., topk * MOBA_BLOCK:]
        return (jnp.einsum('bhqjk,bhqjkd->bhqd', p_sel, v_sel)
                + jnp.einsum('bhqk,bhkd->bhqd', p_own, v_own))

    out = lax.map(one_chunk, (jnp.arange(n_chunks), q_chunks))
    return out.transpose(1, 0, 3, 2, 4).reshape(b, s_pad, h * dh)[:, :s]


def rwkv7_time_mix(r, k, v, h_w, h_a, h_g, w0, w_decay_up, a0, w_aaa_up, w_gate_up,
                   k_k, k_a, r_k, lnx_g, lnx_b):
    b, s, _ = r.shape
    nh, n = RWKV_HEADS, HEAD_DIM
    f32 = jnp.float32
    w_log = -jax.nn.softplus(-(w0 + jnp.tanh(h_w) @ w_decay_up).astype(f32)) - 0.5
    decay = jnp.exp(-jnp.exp(w_log))
    a = jax.nn.sigmoid((a0 + h_a @ w_aaa_up).astype(f32))
    g = jax.nn.sigmoid(h_g) @ w_gate_up
    kk = (k * k_k).astype(f32).reshape(b, s, nh, n)
    kk = kk / jnp.maximum(jnp.sqrt(jnp.sum(kk * kk, -1, keepdims=True)), 1e-12)
    k = k * (1.0 + (a - 1.0) * k_a)
    to_heads = lambda t: t.astype(f32).reshape(b, s, nh, n)
    r_h, k_h, v_h, a_h, w_h = to_heads(r), to_heads(k), to_heads(v), to_heads(a), to_heads(decay)
    seq_first = lambda t: jnp.swapaxes(t, 0, 1)

    def step(state, inp):
        r_t, w_t, k_t, v_t, kk_t, a_t = inp
        s_kk = jnp.einsum('bhvk,bhk->bhv', state, kk_t)
        state = (state * w_t[:, :, None, :]
                 - s_kk[..., None] * (kk_t * a_t)[:, :, None, :]
                 + v_t[..., None] * k_t[:, :, None, :])
        return state, jnp.einsum('bhvk,bhk->bhv', state, r_t)

    state0 = jnp.zeros((b, nh, n, n), f32)
    _, y = lax.scan(step, state0, [seq_first(t) for t in (r_h, w_h, k_h, v_h, kk, a_h)])
    y = seq_first(y)
    mu = y.mean(-1, keepdims=True)
    var = jnp.square(y - mu).mean(-1, keepdims=True)
    y = ((y - mu) * lax.rsqrt(var + GN_EPS)).reshape(b, s, nh * n) * lnx_g + lnx_b
    bonus = (jnp.sum(r_h * k_h * r_k, -1, keepdims=True) * v_h).reshape(b, s, nh * n)
    return ((y + bonus) * g).astype(r.dtype)


def clamped_swiglu(h):
    h = h.astype(jnp.float32)
    x_glu = jnp.minimum(h[..., ::2], SWIGLU_LIMIT)
    x_lin = jnp.clip(h[..., 1::2], -SWIGLU_LIMIT, SWIGLU_LIMIT)
    return x_glu * jax.nn.sigmoid(SWIGLU_ALPHA * x_glu) * (x_lin + 1.0)


def moe_ffn(x, w_router, b_router, w1, b1, w2, b2):
    b, s, d = x.shape
    t = b * s
    n_slots = t * TOP_K
    xt = x.reshape(t, d)
    logits = (xt @ w_router + b_router).astype(jnp.float32)
    top_vals, top_idx = lax.top_k(logits, TOP_K)
    gates = jax.nn.softmax(top_vals, axis=-1)
    flat_e = top_idx.reshape(-1)
    flat_tok = jnp.arange(n_slots, dtype=jnp.int32) // TOP_K
    flat_g = gates.reshape(-1)
    order = jnp.argsort(flat_e)
    e_sorted = flat_e[order]
    counts = jnp.bincount(flat_e, length=N_EXPERTS)
    starts = jnp.cumsum(counts) - counts
    padded = (counts + EXPERT_BLOCK - 1) // EXPERT_BLOCK * EXPERT_BLOCK
    pad_ends = jnp.cumsum(padded)
    pad_starts = pad_ends - padded
    dest = pad_starts[e_sorted] + (jnp.arange(n_slots) - starts[e_sorted])
    n_blocks = -(-n_slots // EXPERT_BLOCK) + N_EXPERTS
    cap = n_blocks * EXPERT_BLOCK
    slot_tok = jnp.full((cap,), t, jnp.int32).at[dest].set(flat_tok[order])
    slot_gate = jnp.zeros((cap,), jnp.float32).at[dest].set(flat_g[order])
    block_start = jnp.arange(n_blocks) * EXPERT_BLOCK
    block_expert = jnp.minimum(jnp.searchsorted(pad_ends, block_start, side='right'), N_EXPERTS - 1)
    x_pad = jnp.concatenate([xt, jnp.zeros((1, d), xt.dtype)], axis=0)

    def block_step(acc, inp):
        tok, gate, e = inp
        hid = x_pad[tok] @ w1[e] + b1[e]
        y = clamped_swiglu(hid).astype(w2.dtype) @ w2[e] + b2[e]
        return acc.at[tok].add(y.astype(jnp.float32) * gate[:, None]), None

    acc0 = jnp.zeros((t + 1, d), jnp.float32)
    acc, _ = lax.scan(block_step, acc0, (slot_tok.reshape(n_blocks, EXPERT_BLOCK),
                                          slot_gate.reshape(n_blocks, EXPERT_BLOCK), block_expert))
    return acc[:t].reshape(b, s, d).astype(x.dtype)


def hybrid_layer(x, w_in, mu_shift, w0, w_decay_up, a0, w_aaa_up, w_gate_up, k_k, k_a, r_k,
                 lnx_g, lnx_b, w_attn_br, w_rwkv_br, w_out, ln1_g, ln1_b,
                 w_router, b_router, w1, b1, w2, b2, ln2_g, ln2_b):
    b, s, _ = x.shape
    proj = x @ w_in
    sh = token_shift(proj[..., :SHIFT_COLS], mu_shift)
    y_rwkv = rwkv7_time_mix(sh[..., OFF_R:OFF_K], sh[..., OFF_K:OFF_V], sh[..., OFF_V:OFF_WD],
                            sh[..., OFF_WD:OFF_AD], sh[..., OFF_AD:OFF_GD], sh[..., OFF_GD:SHIFT_COLS],
                            w0, w_decay_up, a0, w_aaa_up, w_gate_up, k_k, k_a, r_k, lnx_g, lnx_b)
    heads = lambda t: t.reshape(b, s, ATTN_HEADS, HEAD_DIM)
    y_attn = moba_attention(heads(proj[..., OFF_Q:OFF_AK]), heads(proj[..., OFF_AK:OFF_AV]),
                            heads(proj[..., OFF_AV:OFF_GATE_A]))
    mixed = (jax.nn.sigmoid(proj[..., OFF_GATE_A:OFF_GATE_R]) * (y_attn @ w_attn_br)
             + jax.nn.sigmoid(proj[..., OFF_GATE_R:]) * (y_rwkv @ w_rwkv_br))
    h = layer_norm(DEEPNORM_ALPHA * x + mixed @ w_out, ln1_g, ln1_b)
    return layer_norm(DEEPNORM_ALPHA * h + moe_ffn(h, w_router, b_router, w1, b1, w2, b2), ln2_g, ln2_b)


def setup_inputs(seed: int = 0) -> dict:
    key = jax.random.key(seed)
    ks = iter(jax.random.split(key, 40))
    nrm = lambda shape, scale: jax.random.normal(next(ks), shape, jnp.float32) * scale
    L, D, E, F = DEPTH, D_MODEL, N_EXPERTS, D_EXPERT
    return {
        'x': nrm((BATCH, SEQ, D), 1.0),
        'w_in': nrm((L, D, IN_COLS), D ** -0.5),
        'mu_shift': jax.random.uniform(next(ks), (L, SHIFT_COLS), jnp.float32),
        'w0': nrm((L, RWKV_WIDTH), 0.5),
        'w_decay_up': nrm((L, DECAY_LORA, RWKV_WIDTH), DECAY_LORA ** -0.5),
        'a0': nrm((L, RWKV_WIDTH), 0.1),
        'w_aaa_up': nrm((L, AAA_LORA, RWKV_WIDTH), AAA_LORA ** -0.5),
        'w_gate_up': nrm((L, GATE_LORA, RWKV_WIDTH), GATE_LORA ** -0.5),
        'k_k': 0.85 + nrm((L, RWKV_WIDTH), 0.05),
        'k_a': 1.0 + nrm((L, RWKV_WIDTH), 0.05),
        'r_k': nrm((L, RWKV_HEADS, HEAD_DIM), 0.1),
        'lnx_g': 1.0 + nrm((L, RWKV_WIDTH), 0.02),
        'lnx_b': nrm((L, RWKV_WIDTH), 0.02),
        'w_attn_br': nrm((L, ATTN_WIDTH, D), ATTN_WIDTH ** -0.5),
        'w_rwkv_br': nrm((L, RWKV_WIDTH, D), RWKV_WIDTH ** -0.5),
        'w_out': nrm((L, D, D), D ** -0.5 * DEEPNORM_BETA),
        'ln1_g': 1.0 + nrm((L, D), 0.02),
        'ln1_b': nrm((L, D), 0.02),
        'w_router': nrm((L, D, E), D ** -0.5),
        'b_router': nrm((L, E), 0.01),
        'w1': nrm((L, E, D, 2 * F), D ** -0.5),
        'b1': nrm((L, E, 2 * F), 0.01),
        'w2': nrm((L, E, F, D), F ** -0.5 * DEEPNORM_BETA),
        'b2': nrm((L, E, D), 0.01),
        'ln2_g': 1.0 + nrm((L, D), 0.02),
        'ln2_b': nrm((L, D), 0.02),
    }


def reference(x, w_in, mu_shift, w0, w_decay_up, a0, w_aaa_up, w_gate_up, k_k, k_a, r_k,
              lnx_g, lnx_b, w_attn_br, w_rwkv_br, w_out, ln1_g, ln1_b,
              w_router, b_router, w1, b1, w2, b2, ln2_g, ln2_b):
    for l in range(DEPTH):
        x = hybrid_layer(x, w_in[l], mu_shift[l], w0[l], w_decay_up[l], a0[l], w_aaa_up[l],
                         w_gate_up[l], k_k[l], k_a[l], r_k[l], lnx_g[l], lnx_b[l],
                         w_attn_br[l], w_rwkv_br[l], w_out[l], ln1_g[l], ln1_b[l],
                         w_router[l], b_router[l], w1[l], b1[l], w2[l], b2[l], ln2_g[l], ln2_b[l])
    return x
```

```python
import functools

import jax
import jax.numpy as jnp
from jax import lax
from jax.experimental import pallas as pl
from jax.experimental.pallas import tpu as pltpu

F32 = jnp.float32
BF16 = jnp.bfloat16
HI = lax.Precision.HIGHEST

HEAD_DIM = 64
N_HEADS = 8
WIDTH = N_HEADS * HEAD_DIM
PAIR = 2 * HEAD_DIM
N_PAIRS = N_HEADS // 2
MOBA_BLOCK = 256
MOBA_TOPK = 3
ROPE_THETA = 10000.0
DECAY_LORA = 64
AAA_LORA = 64
GATE_LORA = 160
GATE_LORA_PAD = 256
RW_COLS = 3 * WIDTH + DECAY_LORA + AAA_LORA + GATE_LORA
RW_COLS_PAD = 3 * WIDTH + DECAY_LORA + AAA_LORA + GATE_LORA_PAD
GN_EPS = 64e-5
LN_EPS = 1e-5
N_EXPERTS = 32
TOP_K = 4
ROUTER_PAD = 128
SWIGLU_ALPHA = 1.702
SWIGLU_LIMIT = 7.0
RWKV_CHUNK = 64
EXPERT_ROWS = 256
COMBINE_ROWS = 128
VMEM_LIMIT = 48 * 1024 * 1024

NEG_INF = float("-inf")


def _nt(a, b, precision=None):
    return lax.dot_general(a, b, (((1,), (1,)), ((), ())), precision=precision,
                           preferred_element_type=F32)


def _dot(a, b, precision=None):
    return jnp.dot(a, b, precision=precision, preferred_element_type=F32)


def _matmul_kernel(x_ref, w_ref, o_ref):
    o_ref[...] = _dot(x_ref[...], w_ref[...])


def _project(xb, w, tm=512):
    t, d = xb.shape
    n = w.shape[1]
    return pl.pallas_call(
        _matmul_kernel,
        grid=(t // tm,),
        in_specs=[pl.BlockSpec((tm, d), lambda i: (i, 0)),
                  pl.BlockSpec((d, n), lambda i: (0, 0))],
        out_specs=pl.BlockSpec((tm, n), lambda i: (i, 0)),
        out_shape=jax.ShapeDtypeStruct((t, n), F32),
        compiler_params=pltpu.CompilerParams(dimension_semantics=("parallel",),
                                             vmem_limit_bytes=VMEM_LIMIT),
    )(xb, w)


def _qkv_kernel(x_ref, w_ref, cos_ref, sin_ref, q_ref, k_ref, v_ref, km_ref, *, tm):
    acc = _dot(x_ref[...], w_ref[...])
    cos = jnp.concatenate([cos_ref[...]] * (WIDTH // PAIR), axis=1)
    sin = jnp.concatenate([sin_ref[...]] * (WIDTH // PAIR), axis=1)
    lane = lax.broadcasted_iota(jnp.int32, (tm, WIDTH), 1)
    first_half = (lane & (HEAD_DIM // 2)) == 0

    def rope(t):
        partner = jnp.where(first_half, pltpu.roll(t, WIDTH - HEAD_DIM // 2, 1),
                            pltpu.roll(t, HEAD_DIM // 2, 1))
        return t * cos + partner * sin

    q = rope(acc[:, :WIDTH]) * (HEAD_DIM ** -0.5)
    k = rope(acc[:, WIDTH:2 * WIDTH])
    q_ref[...] = q.astype(BF16)
    k_ref[...] = k.astype(BF16)
    v_ref[...] = acc[:, 2 * WIDTH:].astype(BF16)
    for j in range(tm // MOBA_BLOCK):
        km_ref[0, j:j + 1, :] = jnp.mean(k[j * MOBA_BLOCK:(j + 1) * MOBA_BLOCK], axis=0, keepdims=True)


def _project_qkv(xb, w, cos_t, sin_t, seq, tm=512):
    t, d = xb.shape
    steps_per_seq = seq // tm
    out_bf = jax.ShapeDtypeStruct((t, WIDTH), BF16)
    row_spec = pl.BlockSpec((tm, WIDTH), lambda i: (i, 0))
    tab_spec = pl.BlockSpec((tm, PAIR), lambda i: (i % steps_per_seq, 0))
    return pl.pallas_call(
        functools.partial(_qkv_kernel, tm=tm),
        grid=(t // tm,),
        in_specs=[pl.BlockSpec((tm, d), lambda i: (i, 0)),
                  pl.BlockSpec((d, 3 * WIDTH), lambda i: (0, 0)),
                  tab_spec, tab_spec],
        out_specs=[row_spec, row_spec, row_spec,
                   pl.BlockSpec((1, tm // MOBA_BLOCK, WIDTH), lambda i: (i, 0, 0))],
        out_shape=[out_bf, out_bf, out_bf,
                   jax.ShapeDtypeStruct((t // tm, tm // MOBA_BLOCK, WIDTH), F32)],
        compiler_params=pltpu.CompilerParams(dimension_semantics=("parallel",),
                                             vmem_limit_bytes=VMEM_LIMIT),
    )(xb, w, cos_t, sin_t)


def _softplus(z):
    return jnp.maximum(z, 0.0) + jnp.log(1.0 + jnp.exp(-jnp.abs(z)))


def _sigmoid(z):
    return 1.0 / (1.0 + jnp.exp(-z))


def _rwkv_kernel(p_ref, mu_ref, w0_ref, a0_ref, wlora_ref, wg_ref, kk_ref, ka_ref, rk_ref,
                 lng_ref, lnb_ref, y_ref, carry_ref, state_ref):
    c = RWKV_CHUNK

    @pl.when(pl.program_id(1) == 0)
    def _():
        carry_ref[...] = jnp.zeros_like(carry_ref)
        state_ref[...] = jnp.zeros_like(state_ref)

    p = p_ref[0]
    row = lax.broadcasted_iota(jnp.int32, p.shape, 0)
    prev = jnp.where(row == 0, carry_ref[0:1, :], pltpu.roll(p, 1, 0))
    carry_ref[...] = jnp.broadcast_to(p[c - 1:c, :], carry_ref.shape)
    sh = p + (prev - p) * mu_ref[...]

    r = sh[:, 0:WIDTH]
    k = sh[:, WIDTH:2 * WIDTH]
    v = sh[:, 2 * WIDTH:3 * WIDTH]
    lora = sh[:, 3 * WIDTH:3 * WIDTH + PAIR]
    hg = sh[:, 3 * WIDTH + PAIR:]
    lane_l = lax.broadcasted_iota(jnp.int32, lora.shape, 1)
    lora_act = jnp.where(lane_l < DECAY_LORA, jnp.tanh(lora), lora)
    wa = _dot(lora_act.astype(BF16), wlora_ref[...])
    w_log = -_softplus(-(w0_ref[...] + wa[:, :WIDTH])) - 0.5
    logw = -jnp.exp(w_log)
    a = _sigmoid(a0_ref[...] + wa[:, WIDTH:])
    g = _dot(_sigmoid(hg).astype(BF16), wg_ref[...])

    ri = lax.broadcasted_iota(jnp.int32, (PAIR, PAIR), 0)
    ci = lax.broadcasted_iota(jnp.int32, (PAIR, PAIR), 1)
    head_sum = jnp.where((ri // HEAD_DIM) == (ci // HEAD_DIM), 1.0, 0.0).astype(F32)
    eye = jnp.where(ri == ci, 1.0, 0.0).astype(F32)
    strict_lower = ri > ci
    lower = ri >= ci
    rc = lax.broadcasted_iota(jnp.int32, (c, c), 0)
    cc = lax.broadcasted_iota(jnp.int32, (c, c), 1)
    cumsum_mat = jnp.where(rc >= cc, 1.0, 0.0).astype(F32)
    lane_p = lax.broadcasted_iota(jnp.int32, (c, PAIR), 1)
    head0 = lane_p < HEAD_DIM

    def stack(t):
        return jnp.concatenate([jnp.where(head0, t, 0.0), jnp.where(head0, 0.0, t)], axis=0)

    kkn = k * kk_ref[...]
    k2 = k * (1.0 + (a - 1.0) * ka_ref[...])
    cum_all = _dot(cumsum_mat, logw, HI)

    for pp in range(N_PAIRS):
        sl = slice(pp * PAIR, (pp + 1) * PAIR)
        r_p, k_p, v_p, a_p = r[:, sl], k2[:, sl], v[:, sl], a[:, sl]
        kk_p = kkn[:, sl]
        ss = _dot(kk_p * kk_p, head_sum, HI)
        kap = kk_p / jnp.maximum(jnp.sqrt(ss), 1e-12)
        lw = logw[:, sl]
        cum = cum_all[:, sl]
        pc = jnp.exp(cum[c - 1:c, :])
        dec = jnp.exp(cum)
        inv = jnp.exp(-cum)
        rt = r_p * dec
        bt = kap * jnp.exp(cum - lw)
        at = -(kap * a_p) * inv
        kt = k_p * inv
        rm, bm, am, km, vm = stack(rt), stack(bt), stack(at), stack(kt), stack(v_p)

        la = jnp.where(strict_lower, _nt(bm, am, HI), 0.0)
        lk = jnp.where(strict_lower, _nt(bm, km, HI), 0.0)
        ma = jnp.where(lower, _nt(rm, am, HI), 0.0)
        mk = jnp.where(lower, _nt(rm, km, HI), 0.0)

        tinv = eye + la
        lpow = la
        n = 2
        while n < c:
            lpow = _dot(lpow, lpow, HI)
            tinv = tinv + _dot(tinv, lpow, HI)
            n *= 2

        h0 = state_ref[pp]
        u = _dot(tinv, _dot(bm, h0, HI) + _dot(lk, vm, HI), HI)
        yst = _dot(rm, h0, HI) + _dot(ma, u, HI) + _dot(mk, vm, HI)
        y = yst[:c] + yst[c:]

        pc_col = jnp.transpose(jnp.broadcast_to(pc, (PAIR, PAIR)))
        upd = _dot(jnp.transpose(am * pc), u, HI) + _dot(jnp.transpose(km * pc), vm, HI)
        state_ref[pp] = h0 * pc_col + upd

        mean = _dot(y, head_sum, HI) * (1.0 / HEAD_DIM)
        yc = y - mean
        var = _dot(yc * yc, head_sum, HI) * (1.0 / HEAD_DIM)
        yn = yc * lax.rsqrt(var + GN_EPS) * lng_ref[:, sl] + lnb_ref[:, sl]
        bonus = _dot(r_p * k_p * rk_ref[:, sl], head_sum, HI) * v_p
        y_ref[0, :, sl] = ((yn + bonus) * g[:, sl]).astype(y_ref.dtype)


def _rwkv(proj_rw, mu, w0, a0, wlora, wg, k_k, k_a, r_k, lnx_g, lnx_b):
    b, s, n = proj_rw.shape
    c = RWKV_CHUNK
    vec = lambda width: pl.BlockSpec((1, width), lambda bi, ci: (0, 0))
    return pl.pallas_call(
        _rwkv_kernel,
        grid=(b, s // c),
        in_specs=[pl.BlockSpec((1, c, n), lambda bi, ci: (bi, ci, 0)),
                  vec(n), vec(WIDTH), vec(WIDTH),
                  pl.BlockSpec(wlora.shape, lambda bi, ci: (0, 0)),
                  pl.BlockSpec(wg.shape, lambda bi, ci: (0, 0)),
                  vec(WIDTH), vec(WIDTH), vec(WIDTH), vec(WIDTH), vec(WIDTH)],
        out_specs=pl.BlockSpec((1, c, WIDTH), lambda bi, ci: (bi, ci, 0)),
        out_shape=jax.ShapeDtypeStruct((b, s, WIDTH), BF16),
        scratch_shapes=[pltpu.VMEM((8, n), F32),
                        pltpu.VMEM((N_PAIRS, PAIR, PAIR), F32)],
        compiler_params=pltpu.CompilerParams(dimension_semantics=("arbitrary", "arbitrary"),
                                             vmem_limit_bytes=VMEM_LIMIT),
    )(proj_rw, mu, w0, a0, wlora, wg, k_k, k_a, r_k, lnx_g, lnx_b)


def _moba_kernel(q_ref, k_ref, v_ref, km_ref, o_ref, *, nb):
    bs = MOBA_BLOCK
    i = pl.program_id(2)
    q2 = q_ref[0]
    km = km_ref[0].astype(BF16)
    lane = lax.broadcasted_iota(jnp.int32, (bs, PAIR), 1)
    blk = lax.broadcasted_iota(jnp.int32, (bs, nb), 1).astype(F32)
    rowi = lax.broadcasted_iota(jnp.int32, (bs, bs), 0)
    coli = lax.broadcasted_iota(jnp.int32, (bs, bs), 1)
    zero = jnp.zeros_like(q2)

    start = pl.multiple_of(i * bs, bs)
    k_own = k_ref[0, pl.ds(start, bs), :]
    v_own = v_ref[0, pl.ds(start, bs), :]

    qs, sels, carry = [], [], []
    for h in range(2):
        in_head = (lane < HEAD_DIM) if h == 0 else (lane >= HEAD_DIM)
        qh = jnp.where(in_head, q2, zero)
        gate = jnp.where(blk < i.astype(F32), _nt(qh, km), NEG_INF)
        sel = jnp.zeros((bs, nb), F32)
        for _ in range(MOBA_TOPK):
            mx = jnp.max(gate, axis=1, keepdims=True)
            hit = (gate == mx) & (mx > NEG_INF)
            idx = jnp.min(jnp.where(hit, blk, float(nb)), axis=1, keepdims=True)
            pick = blk == idx
            sel = jnp.where(pick, 1.0, sel)
            gate = jnp.where(pick, NEG_INF, gate)
        s = jnp.where(coli <= rowi, _nt(qh, k_own), NEG_INF)
        m = jnp.max(s, axis=1, keepdims=True)
        p = jnp.exp(s - m)
        l = jnp.sum(p, axis=1, keepdims=True)
        acc = _dot(p.astype(BF16), v_own)
        qs.append(qh)
        sels.append(sel)
        carry += [m, l, acc]

    def body(j, carry):
        off = pl.multiple_of(j * bs, bs)
        k_j = k_ref[0, pl.ds(off, bs), :]
        v_j = v_ref[0, pl.ds(off, bs), :]
        out = []
        for h in range(2):
            m, l, acc = carry[3 * h:3 * h + 3]
            chosen = jnp.sum(jnp.where(blk == j.astype(F32), sels[h], 0.0), axis=1, keepdims=True) > 0.0
            s = jnp.where(chosen, _nt(qs[h], k_j), NEG_INF)
            m_new = jnp.maximum(m, jnp.max(s, axis=1, keepdims=True))
            alpha = jnp.exp(m - m_new)
            p = jnp.exp(s - m_new)
            l = alpha * l + jnp.sum(p, axis=1, keepdims=True)
            acc = alpha * acc + _dot(p.astype(BF16), v_j)
            out += [m_new, l, acc]
        return tuple(out)

    carry = lax.fori_loop(0, i, body, tuple(carry))
    o0 = carry[2] / carry[1]
    o1 = carry[5] / carry[4]
    o_ref[0] = jnp.where(lane < HEAD_DIM, o0, o1).astype(o_ref.dtype)


def _moba(q, k, v, kmean):
    b, s, _ = q.shape
    nb = s // MOBA_BLOCK
    blk_spec = pl.BlockSpec((1, MOBA_BLOCK, PAIR), lambda bi, pi, qi: (bi, qi, pi))
    seq_spec = pl.BlockSpec((1, s, PAIR), lambda bi, pi, qi: (bi, 0, pi))
    return pl.pallas_call(
        functools.partial(_moba_kernel, nb=nb),
        grid=(b, N_PAIRS, nb),
        in_specs=[blk_spec, seq_spec, seq_spec,
                  pl.BlockSpec((1, nb, PAIR), lambda bi, pi, qi: (bi, 0, pi))],
        out_specs=blk_spec,
        out_shape=jax.ShapeDtypeStruct((b, s, WIDTH), BF16),
        compiler_params=pltpu.CompilerParams(
            dimension_semantics=("parallel", "parallel", "arbitrary"),
            vmem_limit_bytes=VMEM_LIMIT),
    )(q, k, v, kmean)


def _layer_norm(z, g, b):
    mu = jnp.mean(z, axis=1, keepdims=True)
    zc = z - mu
    var = jnp.mean(zc * zc, axis=1, keepdims=True)
    return zc * lax.rsqrt(var + LN_EPS) * g + b


def _merge_kernel(x_ref, ya_ref, yr_ref, wgate_ref, wab_ref, wrb_ref, wout_ref, g1_ref, b1_ref,
                  wr_ref, br_ref, h_ref, idx_ref, gate_ref, *, alpha):
    x = x_ref[...]
    d = x.shape[1]
    gates = _sigmoid(_dot(x.astype(BF16), wgate_ref[...]))
    mixed = (gates[:, :d] * _dot(ya_ref[...], wab_ref[...])
             + gates[:, d:] * _dot(yr_ref[...], wrb_ref[...]))
    h = _layer_norm(alpha * x + _dot(mixed.astype(BF16), wout_ref[...]), g1_ref[...], b1_ref[...])
    h_ref[...] = h

    logits = _dot(h, wr_ref[...], HI) + br_ref[...]
    col = lax.broadcasted_iota(jnp.int32, logits.shape, 1).astype(F32)
    idx_out = jnp.zeros(logits.shape, F32)
    val_out = jnp.zeros(logits.shape, F32)
    top = None
    denom = None
    for t in range(TOP_K):
        mx = jnp.max(logits, axis=1, keepdims=True)
        idx = jnp.min(jnp.where(logits == mx, col, float(ROUTER_PAD)), axis=1, keepdims=True)
        if t == 0:
            top = mx
        e = jnp.exp(mx - top)
        denom = e if t == 0 else denom + e
        idx_out = jnp.where(col == float(t), idx, idx_out)
        val_out = jnp.where(col == float(t), e, val_out)
        logits = jnp.where(col == idx, NEG_INF, logits)
    idx_ref[...] = idx_out.astype(jnp.int32)
    gate_ref[...] = val_out / denom


def _merge(x2, ya, yr, wgate, wab, wrb, wout, g1, b1, wr, br, alpha, tm=256):
    t, d = x2.shape
    row = lambda width: pl.BlockSpec((tm, width), lambda i: (i, 0))
    full = lambda arr: pl.BlockSpec(arr.shape, lambda i: (0, 0))
    return pl.pallas_call(
        functools.partial(_merge_kernel, alpha=alpha),
        grid=(t // tm,),
        in_specs=[row(d), row(WIDTH), row(WIDTH), full(wgate), full(wab), full(wrb), full(wout),
                  full(g1), full(b1), full(wr), full(br)],
        out_specs=[row(d), row(ROUTER_PAD), row(ROUTER_PAD)],
        out_shape=[jax.ShapeDtypeStruct((t, d), F32),
                   jax.ShapeDtypeStruct((t, ROUTER_PAD), jnp.int32),
                   jax.ShapeDtypeStruct((t, ROUTER_PAD), F32)],
        compiler_params=pltpu.CompilerParams(dimension_semantics=("parallel",),
                                             vmem_limit_bytes=VMEM_LIMIT),
    )(x2, ya, yr, wgate, wab, wrb, wout, g1, b1, wr, br)


def _row_copy(src_hbm, dst_buf, sem, src_row, slot, dst_row):
    return pltpu.make_async_copy(src_hbm.at[pl.ds(src_row, 1), :],
                                 dst_buf.at[slot, pl.ds(dst_row, 1), :],
                                 sem.at[slot])


def _gather_rows(src_hbm, dst_buf, sem, idx_ref, slot, n_rows):
    def issue(r, _):
        _row_copy(src_hbm, dst_buf, sem, idx_ref[0, 0, r], slot, r).start()
        return 0
    lax.fori_loop(0, n_rows, issue, 0)


def _wait_rows(src_hbm, dst_buf, sem, slot, n_rows):
    def wait(r, _):
        _row_copy(src_hbm, dst_buf, sem, 0, slot, r).wait()
        return 0
    lax.fori_loop(0, n_rows, wait, 0)


def _expert_kernel(be_ref, nact_ref, tok_ref, tok_next_ref, h_hbm, w1_ref, b1_ref, w2_ref, b2_ref,
                   y_ref, xbuf, sem):
    del be_ref
    rows = EXPERT_ROWS
    b = pl.program_id(0)
    nact = nact_ref[0]
    slot = b % 2

    @pl.when((b == 0) & (nact > 0))
    def _():
        _gather_rows(h_hbm, xbuf, sem, tok_ref, 0, rows)

    @pl.when(b + 1 < nact)
    def _():
        _gather_rows(h_hbm, xbuf, sem, tok_next_ref, 1 - slot, rows)

    @pl.when(b < nact)
    def _():
        _wait_rows(h_hbm, xbuf, sem, slot, rows)
        xs = xbuf[slot].astype(BF16)
        f = w2_ref.shape[1]
        hid = _dot(xs, w1_ref[0]) + b1_ref[0]
        x_glu = jnp.minimum(hid[:, :f], SWIGLU_LIMIT)
        x_lin = jnp.clip(hid[:, f:], -SWIGLU_LIMIT, SWIGLU_LIMIT)
        act = x_glu * _sigmoid(SWIGLU_ALPHA * x_glu) * (x_lin + 1.0)
        y_ref[...] = _dot(act.astype(BF16), w2_ref[0]) + b2_ref[0]

    @pl.when(b >= nact)
    def _():
        y_ref[...] = jnp.zeros_like(y_ref)


def _experts(block_expert, n_active, slot_tok, h, w1p, b1p, w2b, b2):
    t, d = h.shape
    nblk = block_expert.shape[0]
    rows = EXPERT_ROWS
    f = w2b.shape[1]
    tok3 = slot_tok.reshape(nblk, 1, rows)
    grid_spec = pltpu.PrefetchScalarGridSpec(
        num_scalar_prefetch=2,
        grid=(nblk,),
        in_specs=[
            pl.BlockSpec((1, 1, rows), lambda b, be, na: (b, 0, 0), memory_space=pltpu.SMEM),
            pl.BlockSpec((1, 1, rows), lambda b, be, na: (jnp.minimum(b + 1, nblk - 1), 0, 0),
                         memory_space=pltpu.SMEM),
            pl.BlockSpec(memory_space=pl.ANY),
            pl.BlockSpec((1, d, 2 * f), lambda b, be, na: (be[b], 0, 0)),
            pl.BlockSpec((1, 1, 2 * f), lambda b, be, na: (be[b], 0, 0)),
            pl.BlockSpec((1, f, d), lambda b, be, na: (be[b], 0, 0)),
            pl.BlockSpec((1, 1, d), lambda b, be, na: (be[b], 0, 0)),
        ],
        out_specs=pl.BlockSpec((rows, d), lambda b, be, na: (b, 0)),
        scratch_shapes=[pltpu.VMEM((2, rows, d), F32), pltpu.SemaphoreType.DMA((2,))],
    )
    return pl.pallas_call(
        _expert_kernel,
        grid_spec=grid_spec,
        out_shape=jax.ShapeDtypeStruct((nblk * rows, d), F32),
        compiler_params=pltpu.CompilerParams(dimension_semantics=("arbitrary",),
                                             vmem_limit_bytes=VMEM_LIMIT),
    )(block_expert, n_active, tok3, tok3, h, w1p, b1p, w2b, b2)


def _combine_kernel(dest_ref, dest_next_ref, h_ref, gate_ref, g2_ref, b2_ref, y_hbm, o_ref, ybuf, sem,
                    *, alpha, nsteps):
    tm = COMBINE_ROWS
    n = TOP_K * tm
    s = pl.program_id(0)
    slot = s % 2

    @pl.when(s == 0)
    def _():
        _gather_rows(y_hbm, ybuf, sem, dest_ref, 0, n)

    @pl.when(s + 1 < nsteps)
    def _():
        _gather_rows(y_hbm, ybuf, sem, dest_next_ref, 1 - slot, n)

    _wait_rows(y_hbm, ybuf, sem, slot, n)
    gates = gate_ref[...]
    moe = gates[:, 0:1] * ybuf[slot, 0:tm, :]
    for kk in range(1, TOP_K):
        moe = moe + gates[:, kk:kk + 1] * ybuf[slot, kk * tm:(kk + 1) * tm, :]
    o_ref[...] = _layer_norm(alpha * h_ref[...] + moe, g2_ref[...], b2_ref[...])


def _combine(dest_km, h, gate_pad, g2, b2, y_sorted, alpha):
    t, d = h.shape
    tm = COMBINE_ROWS
    nsteps = t // tm
    n = TOP_K * tm
    return pl.pallas_call(
        functools.partial(_combine_kernel, alpha=alpha, nsteps=nsteps),
        grid=(nsteps,),
        in_specs=[
            pl.BlockSpec((1, 1, n), lambda s: (s, 0, 0), memory_space=pltpu.SMEM),
            pl.BlockSpec((1, 1, n), lambda s: (jnp.minimum(s + 1, nsteps - 1), 0, 0),
                         memory_space=pltpu.SMEM),
            pl.BlockSpec((tm, d), lambda s: (s, 0)),
            pl.BlockSpec((tm, ROUTER_PAD), lambda s: (s, 0)),
            pl.BlockSpec((1, d), lambda s: (0, 0)),
            pl.BlockSpec((1, d), lambda s: (0, 0)),
            pl.BlockSpec(memory_space=pl.ANY),
        ],
        out_specs=pl.BlockSpec((tm, d), lambda s: (s, 0)),
        out_shape=jax.ShapeDtypeStruct((t, d), F32),
        scratch_shapes=[pltpu.VMEM((2, n, d), F32), pltpu.SemaphoreType.DMA((2,))],
        compiler_params=pltpu.CompilerParams(dimension_semantics=("arbitrary",),
                                             vmem_limit_bytes=VMEM_LIMIT),
    )(dest_km, dest_km, h, gate_pad, g2, b2, y_sorted)


def _routing_tables(top_idx, t):
    rows = EXPERT_ROWS
    n_slots = t * TOP_K
    nblk = n_slots // rows + N_EXPERTS
    flat_e = top_idx.reshape(-1)
    order = jnp.argsort(flat_e)
    e_sorted = flat_e[order]
    counts = jnp.bincount(flat_e, length=N_EXPERTS)
    starts = jnp.cumsum(counts) - counts
    padded = (counts + rows - 1) // rows * rows
    pad_ends = jnp.cumsum(padded)
    pad_starts = pad_ends - padded
    dest_sorted = (pad_starts[e_sorted] + jnp.arange(n_slots) - starts[e_sorted]).astype(jnp.int32)
    slot_tok = jnp.zeros((nblk * rows,), jnp.int32).at[dest_sorted].set((order // TOP_K).astype(jnp.int32))
    dest = jnp.zeros((n_slots,), jnp.int32).at[order].set(dest_sorted)
    block_start = jnp.arange(nblk) * rows
    block_expert = jnp.minimum(jnp.searchsorted(pad_ends, block_start, side='right'),
                               N_EXPERTS - 1).astype(jnp.int32)
    n_active = (pad_ends[-1] // rows).astype(jnp.int32).reshape(1)
    return slot_tok, dest, block_expert, n_active


def _layer(x, w_in, mu_shift, w0, w_decay_up, a0, w_aaa_up, w_gate_up, k_k, k_a, r_k, lnx_g, lnx_b,
           w_attn_br, w_rwkv_br, w_out, ln1_g, ln1_b, w_router, b_router, w1, b1, w2, b2, ln2_g, ln2_b,
           alpha):
    b, s, d = x.shape
    t = b * s
    x2 = x.reshape(t, d)
    xb = x2.astype(BF16)
    row = lambda vec: vec.reshape(1, -1)

    off_q = RW_COLS
    off_gate = off_q + 3 * WIDTH
    pad_cols = RW_COLS_PAD - RW_COLS
    w_rw = jnp.pad(w_in[:, :RW_COLS], ((0, 0), (0, pad_cols))).astype(BF16)
    mu = jnp.pad(mu_shift, (0, pad_cols)).reshape(1, -1)
    w_qkv = w_in[:, off_q:off_gate].astype(BF16)
    w_gate = w_in[:, off_gate:].astype(BF16)

    half = HEAD_DIM // 2
    inv_freq = ROPE_THETA ** (-jnp.arange(half, dtype=F32) / half)
    ang = jnp.arange(s, dtype=F32)[:, None] * inv_freq[None, :]
    cos, sin = jnp.cos(ang), jnp.sin(ang)
    cos_t = jnp.concatenate([cos, cos, cos, cos], axis=1)
    sin_t = jnp.concatenate([-sin, sin, -sin, sin], axis=1)

    proj_rw = _project(xb, w_rw).reshape(b, s, RW_COLS_PAD)
    q, k, v, kmean = _project_qkv(xb, w_qkv, cos_t, sin_t, s)
    q, k, v = (a.reshape(b, s, WIDTH) for a in (q, k, v))
    kmean = kmean.reshape(b, s // MOBA_BLOCK, WIDTH)

    zeros = jnp.zeros((DECAY_LORA, WIDTH), F32)
    w_lora = jnp.concatenate([jnp.concatenate([w_decay_up, zeros], axis=1),
                              jnp.concatenate([zeros, w_aaa_up], axis=1)], axis=0).astype(BF16)
    w_g = jnp.pad(w_gate_up, ((0, GATE_LORA_PAD - GATE_LORA), (0, 0))).astype(BF16)
    y_rwkv = _rwkv(proj_rw, mu, row(w0), row(a0), w_lora, w_g, row(k_k), row(k_a), row(r_k),
                   row(lnx_g), row(lnx_b))
    y_attn = _moba(q, k, v, kmean)

    w_r = jnp.pad(w_router, ((0, 0), (0, ROUTER_PAD - N_EXPERTS)))
    b_r = jnp.pad(b_router, (0, ROUTER_PAD - N_EXPERTS), constant_values=NEG_INF).reshape(1, -1)
    h, idx_pad, gate_pad = _merge(x2, y_attn.reshape(t, WIDTH), y_rwkv.reshape(t, WIDTH), w_gate,
                                  w_attn_br.astype(BF16), w_rwkv_br.astype(BF16), w_out.astype(BF16),
                                  row(ln1_g), row(ln1_b), w_r, b_r, alpha)

    slot_tok, dest, block_expert, n_active = _routing_tables(idx_pad[:, :TOP_K], t)
    w1p = jnp.concatenate([w1[:, :, 0::2], w1[:, :, 1::2]], axis=2).astype(BF16)
    b1p = jnp.concatenate([b1[:, 0::2], b1[:, 1::2]], axis=1)[:, None, :]
    y_sorted = _experts(block_expert, n_active, slot_tok, h, w1p, b1p, w2.astype(BF16), b2[:, None, :])

    tm = COMBINE_ROWS
    dest_km = dest.reshape(t // tm, tm, TOP_K).transpose(0, 2, 1).reshape(t // tm, 1, TOP_K * tm)
    out = _combine(dest_km, h, gate_pad, row(ln2_g), row(ln2_b), y_sorted, alpha)
    return out.reshape(b, s, d)


def kernel(x, w_in, mu_shift, w0, w_decay_up, a0, w_aaa_up, w_gate_up, k_k, k_a, r_k, lnx_g, lnx_b,
           w_attn_br, w_rwkv_br, w_out, ln1_g, ln1_b, w_router, b_router, w1, b1, w2, b2, ln2_g, ln2_b):
    depth = w_in.shape[0]
    alpha = (2 * depth) ** 0.25
    for l in range(depth):
        x = _layer(x, w_in[l], mu_shift[l], w0[l], w_decay_up[l], a0[l], w_aaa_up[l], w_gate_up[l],
                   k_k[l], k_a[l], r_k[l].reshape(-1), lnx_g[l], lnx_b[l], w_attn_br[l], w_rwkv_br[l],
                   w_out[l], ln1_g[l], ln1_b[l], w_router[l], b_router[l], w1[l], b1[l], w2[l], b2[l],
                   ln2_g[l], ln2_b[l], alpha)
    return x
```

```python
import functools

import jax
import jax.numpy as jnp
from jax import lax
from jax.experimental import pallas as pl
from jax.experimental.pallas import tpu as pltpu

F32 = jnp.float32
BF16 = jnp.bfloat16
HI = lax.Precision.HIGHEST

HEAD_DIM = 64
N_HEADS = 8
WIDTH = N_HEADS * HEAD_DIM
PAIR = 2 * HEAD_DIM
N_PAIRS = N_HEADS // 2
MOBA_BLOCK = 256
MOBA_TOPK = 3
KEY_GROUP = 8
ROPE_THETA = 10000.0
DECAY_LORA = 64
AAA_LORA = 64
GATE_LORA = 160
GATE_LORA_PAD = 256
RW_COLS = 3 * WIDTH + DECAY_LORA + AAA_LORA + GATE_LORA
RW_COLS_PAD = 3 * WIDTH + DECAY_LORA + AAA_LORA + GATE_LORA_PAD
GN_EPS = 64e-5
LN_EPS = 1e-5
N_EXPERTS = 32
TOP_K = 4
ROUTER_PAD = 128
SWIGLU_ALPHA = 1.702
SWIGLU_LIMIT = 7.0
RWKV_CHUNK = 64
EXPERT_ROWS = 256
COMBINE_ROWS = 128
VMEM_LIMIT = 48 * 1024 * 1024

NEG_INF = float("-inf")
LOG2_E = 1.4426950408889634
MASK_BIAS = -1e30


def _nt(a, b, precision=None):
    return lax.dot_general(a, b, (((1,), (1,)), ((), ())), precision=precision,
                           preferred_element_type=F32)


def _dot(a, b, precision=None):
    return jnp.dot(a, b, precision=precision, preferred_element_type=F32)


def _matmul_kernel(x_ref, w_ref, o_ref):
    o_ref[...] = _dot(x_ref[...], w_ref[...])


def _project(xb, w, tm=512):
    t, d = xb.shape
    n = w.shape[1]
    return pl.pallas_call(
        _matmul_kernel,
        grid=(t // tm,),
        in_specs=[pl.BlockSpec((tm, d), lambda i: (i, 0)),
                  pl.BlockSpec((d, n), lambda i: (0, 0))],
        out_specs=pl.BlockSpec((tm, n), lambda i: (i, 0)),
        out_shape=jax.ShapeDtypeStruct((t, n), F32),
        compiler_params=pltpu.CompilerParams(dimension_semantics=("parallel",),
                                             vmem_limit_bytes=VMEM_LIMIT),
    )(xb, w)


def _qkv_kernel(x_ref, w_ref, cos_ref, sin_ref, qt_ref, kaug_ref, vaug_ref, km_ref, *, tm, steps_per_seq):
    acc = _dot(x_ref[...], w_ref[...])
    cos = jnp.concatenate([cos_ref[...]] * (WIDTH // PAIR), axis=1)
    sin = jnp.concatenate([sin_ref[...]] * (WIDTH // PAIR), axis=1)
    lane = lax.broadcasted_iota(jnp.int32, (tm, WIDTH), 1)
    first_half = (lane & (HEAD_DIM // 2)) == 0

    def rope(t):
        partner = jnp.where(first_half, pltpu.roll(t, WIDTH - HEAD_DIM // 2, 1),
                            pltpu.roll(t, HEAD_DIM // 2, 1))
        return t * cos + partner * sin

    q = rope(acc[:, :WIDTH]) * (HEAD_DIM ** -0.5 * LOG2_E)
    k = rope(acc[:, WIDTH:2 * WIDTH])
    v = acc[:, 2 * WIDTH:]
    for j in range(tm // MOBA_BLOCK):
        km_ref[0, j:j + 1, :] = jnp.mean(k[j * MOBA_BLOCK:(j + 1) * MOBA_BLOCK], axis=0, keepdims=True)

    lane_p = lax.broadcasted_iota(jnp.int32, (tm, PAIR), 1)
    row_p = lax.broadcasted_iota(jnp.int32, (tm, PAIR), 0)
    first_block = (pl.program_id(0) % steps_per_seq) * (tm // MOBA_BLOCK)
    row_block = lax.shift_right_logical(row_p, MOBA_BLOCK.bit_length() - 1)
    block_tag = jnp.where(lane_p - HEAD_DIM == first_block + row_block, 1.0, 0.0)
    ones = jnp.ones((HEAD_DIM, tm), F32)
    for pp in range(N_PAIRS):
        sl = slice(pp * PAIR, (pp + 1) * PAIR)
        q_t = jnp.transpose(q[:, sl])
        v_t = jnp.transpose(v[:, sl])
        k_p = k[:, sl]
        k_sw = pltpu.roll(k_p, HEAD_DIM, 1)
        for h, k_h in ((0, k_p), (1, k_sw)):
            rows = slice(h * HEAD_DIM, (h + 1) * HEAD_DIM)
            qt_ref[0, 2 * pp + h] = q_t[rows].astype(BF16)
            kaug_ref[0, 2 * pp + h] = jnp.where(lane_p < HEAD_DIM, k_h, block_tag).astype(BF16)
            vaug_ref[0, 2 * pp + h] = jnp.concatenate([v_t[rows], ones], axis=0).astype(BF16)


def _project_qkv(xb, w, cos_t, sin_t, batch, seq, tm=512):
    t, d = xb.shape
    steps_per_seq = seq // tm
    assert seq // MOBA_BLOCK <= HEAD_DIM, "block one-hot tags must fit the spare lanes of a head"
    tab_spec = pl.BlockSpec((tm, PAIR), lambda i: (i % steps_per_seq, 0))
    return pl.pallas_call(
        functools.partial(_qkv_kernel, tm=tm, steps_per_seq=steps_per_seq),
        grid=(t // tm,),
        in_specs=[pl.BlockSpec((tm, d), lambda i: (i, 0)),
                  pl.BlockSpec((d, 3 * WIDTH), lambda i: (0, 0)),
                  tab_spec, tab_spec],
        out_specs=[
            pl.BlockSpec((1, N_HEADS, HEAD_DIM, tm), lambda i: (i // steps_per_seq, 0, 0, i % steps_per_seq)),
            pl.BlockSpec((1, N_HEADS, tm, PAIR), lambda i: (i // steps_per_seq, 0, i % steps_per_seq, 0)),
            pl.BlockSpec((1, N_HEADS, PAIR, tm), lambda i: (i // steps_per_seq, 0, 0, i % steps_per_seq)),
            pl.BlockSpec((1, tm // MOBA_BLOCK, WIDTH), lambda i: (i, 0, 0))],
        out_shape=[jax.ShapeDtypeStruct((batch, N_HEADS, HEAD_DIM, seq), BF16),
                   jax.ShapeDtypeStruct((batch, N_HEADS, seq, PAIR), BF16),
                   jax.ShapeDtypeStruct((batch, N_HEADS, PAIR, seq), BF16),
                   jax.ShapeDtypeStruct((t // tm, tm // MOBA_BLOCK, WIDTH), F32)],
        compiler_params=pltpu.CompilerParams(dimension_semantics=("parallel",),
                                             vmem_limit_bytes=VMEM_LIMIT),
    )(xb, w, cos_t, sin_t)


SPLIT_PARTS = 2


def _parts(x, n=SPLIT_PARTS):
    out = []
    for _ in range(n):
        piece = x.astype(BF16)
        out.append(piece)
        x = x - piece.astype(F32)
    return out


def _mm(a_parts, b_parts, f=None):
    f = f or _dot
    order = max(len(a_parts), len(b_parts))
    acc = None
    for i, a in enumerate(a_parts):
        for j, b in enumerate(b_parts):
            if i + j < order:
                term = f(a, b)
                acc = term if acc is None else acc + term
    return acc


def _softplus(z):
    return jnp.maximum(z, 0.0) + jnp.log(1.0 + jnp.exp(-jnp.abs(z)))


def _sigmoid(z):
    return 1.0 / (1.0 + jnp.exp(-z))


def _rwkv_kernel(p_ref, mu_ref, w0_ref, a0_ref, wlora_ref, wg_ref, kk_ref, ka_ref, rk_ref,
                 lng_ref, lnb_ref, y_ref, carry_ref, state_ref):
    c = RWKV_CHUNK
    nbatch = p_ref.shape[0]

    @pl.when(pl.program_id(0) == 0)
    def _():
        carry_ref[...] = jnp.zeros_like(carry_ref)
        state_ref[...] = jnp.zeros_like(state_ref)

    ri = lax.broadcasted_iota(jnp.int32, (PAIR, PAIR), 0)
    ci = lax.broadcasted_iota(jnp.int32, (PAIR, PAIR), 1)
    head_sum = jnp.where((ri // HEAD_DIM) == (ci // HEAD_DIM), 1.0, 0.0).astype(F32)
    eye = jnp.where(ri == ci, 1.0, 0.0).astype(F32)
    strict_lower = ri > ci
    lower = ri >= ci
    rc = lax.broadcasted_iota(jnp.int32, (c, c), 0)
    cc = lax.broadcasted_iota(jnp.int32, (c, c), 1)
    cumsum_mat = jnp.where(rc >= cc, 1.0, 0.0).astype(F32)
    lane_p = lax.broadcasted_iota(jnp.int32, (c, PAIR), 1)
    head0 = lane_p < HEAD_DIM

    def stack(t):
        return jnp.concatenate([jnp.where(head0, t, 0.0), jnp.where(head0, 0.0, t)], axis=0)

    head_sum_b = [head_sum.astype(BF16)]
    cumsum_b = [cumsum_mat.astype(BF16)]

    chains = []
    for bi in range(nbatch):
        p = p_ref[bi]
        row = lax.broadcasted_iota(jnp.int32, p.shape, 0)
        prev = jnp.where(row == 0, carry_ref[bi, 0:1, :], pltpu.roll(p, 1, 0))
        carry_ref[bi] = jnp.broadcast_to(p[c - 1:c, :], carry_ref.shape[1:])
        sh = p + (prev - p) * mu_ref[...]
        r = sh[:, 0:WIDTH]
        k = sh[:, WIDTH:2 * WIDTH]
        v = sh[:, 2 * WIDTH:3 * WIDTH]
        lora = sh[:, 3 * WIDTH:3 * WIDTH + PAIR]
        hg = sh[:, 3 * WIDTH + PAIR:]
        lane_l = lax.broadcasted_iota(jnp.int32, lora.shape, 1)
        lora_act = jnp.where(lane_l < DECAY_LORA, jnp.tanh(lora), lora)
        wa = _dot(lora_act.astype(BF16), wlora_ref[...])
        w_log = -_softplus(-(w0_ref[...] + wa[:, :WIDTH])) - 0.5
        logw = -jnp.exp(w_log)
        a = _sigmoid(a0_ref[...] + wa[:, WIDTH:])
        g = _dot(_sigmoid(hg).astype(BF16), wg_ref[...])
        kkn = k * kk_ref[...]
        k2 = k * (1.0 + (a - 1.0) * ka_ref[...])
        cum_all = _mm(cumsum_b, _parts(logw, 3))
        for pp in range(N_PAIRS):
            sl = slice(pp * PAIR, (pp + 1) * PAIR)
            chains.append(dict(bi=bi, pp=pp, sl=sl, r=r[:, sl], k=k2[:, sl], v=v[:, sl], a=a[:, sl],
                               kk=kkn[:, sl], lw=logw[:, sl], cum=cum_all[:, sl], g=g[:, sl]))

    for ch in chains:
        ch['ss'] = _mm(_parts(ch['kk'] * ch['kk']), head_sum_b)
    for ch in chains:
        kap = ch['kk'] / jnp.maximum(jnp.sqrt(ch['ss']), 1e-12)
        cum = ch['cum']
        ch['pc'] = jnp.exp(cum[c - 1:c, :])
        inv = jnp.exp(-cum)
        rm = stack(ch['r'] * jnp.exp(cum))
        bm = stack(kap * jnp.exp(cum - ch['lw']))
        am = stack(-(kap * ch['a']) * inv)
        km = stack(ch['k'] * inv)
        ch.update(rm=rm, bm=bm, am=am, km=km, vm=stack(ch['v']))
    for ch in chains:
        ch['sb'] = _mm(_parts(jnp.concatenate([ch['bm'], ch['rm']], axis=0)),
                       _parts(jnp.concatenate([ch['am'], ch['km']], axis=0)), _nt)
    for ch in chains:
        sb = ch['sb']
        ch['la'] = jnp.where(strict_lower, sb[:2 * c, :2 * c], 0.0)
        ch['lk'] = jnp.where(strict_lower, sb[:2 * c, 2 * c:], 0.0)
        ch['ma'] = jnp.where(lower, sb[2 * c:, :2 * c], 0.0)
        ch['mk'] = jnp.where(lower, sb[2 * c:, 2 * c:], 0.0)
        ch['tinv'] = eye + ch['la']
        ch['lpow'] = ch['la']
    n = 2
    while n < c:
        for ch in chains:
            lp = _parts(ch['lpow'])
            ch['lpow'] = _mm(lp, lp)
        for ch in chains:
            ch['tinv'] = ch['tinv'] + _mm(_parts(ch['tinv']), _parts(ch['lpow']))
        n *= 2

    for ch in chains:
        ch['h0'] = state_ref[ch['bi'], ch['pp']]
        ch['rhs'] = _mm(_parts(jnp.concatenate([ch['bm'], ch['lk']], axis=1)),
                        _parts(jnp.concatenate([ch['h0'], ch['vm']], axis=0)))
    for ch in chains:
        ch['u'] = _mm(_parts(ch['tinv']), _parts(ch['rhs']))
    for ch in chains:
        yst = _mm(_parts(jnp.concatenate([ch['rm'], ch['ma'], ch['mk']], axis=1)),
                  _parts(jnp.concatenate([ch['h0'], ch['u'], ch['vm']], axis=0)))
        ch['y'] = yst[:c] + yst[c:]
    for ch in chains:
        pc = ch['pc']
        pc_col = jnp.transpose(jnp.broadcast_to(pc, (PAIR, PAIR)))
        upd = _mm(_parts(jnp.concatenate([jnp.transpose(ch['am'] * pc), jnp.transpose(ch['km'] * pc)],
                                         axis=1)),
                  _parts(jnp.concatenate([ch['u'], ch['vm']], axis=0)))
        state_ref[ch['bi'], ch['pp']] = ch['h0'] * pc_col + upd
    for ch in chains:
        ch['mean'] = _mm(_parts(ch['y']), head_sum_b) * (1.0 / HEAD_DIM)
        ch['bonus'] = _mm(_parts(ch['r'] * ch['k'] * rk_ref[:, ch['sl']]), head_sum_b) * ch['v']
    for ch in chains:
        yc = ch['y'] - ch['mean']
        ch['yc'] = yc
        ch['var'] = _mm(_parts(yc * yc), head_sum_b) * (1.0 / HEAD_DIM)
    for ch in chains:
        sl = ch['sl']
        yn = ch['yc'] * lax.rsqrt(ch['var'] + GN_EPS) * lng_ref[:, sl] + lnb_ref[:, sl]
        y_ref[ch['bi'], :, sl] = ((yn + ch['bonus']) * ch['g']).astype(y_ref.dtype)


def _rwkv(proj_rw, mu, w0, a0, wlora, wg, k_k, k_a, r_k, lnx_g, lnx_b):
    b, s, n = proj_rw.shape
    c = RWKV_CHUNK
    vec = lambda width: pl.BlockSpec((1, width), lambda ci: (0, 0))
    return pl.pallas_call(
        _rwkv_kernel,
        grid=(s // c,),
        in_specs=[pl.BlockSpec((b, c, n), lambda ci: (0, ci, 0)),
                  vec(n), vec(WIDTH), vec(WIDTH),
                  pl.BlockSpec(wlora.shape, lambda ci: (0, 0)),
                  pl.BlockSpec(wg.shape, lambda ci: (0, 0)),
                  vec(WIDTH), vec(WIDTH), vec(WIDTH), vec(WIDTH), vec(WIDTH)],
        out_specs=pl.BlockSpec((b, c, WIDTH), lambda ci: (0, ci, 0)),
        out_shape=jax.ShapeDtypeStruct((b, s, WIDTH), BF16),
        scratch_shapes=[pltpu.VMEM((b, 8, n), F32),
                        pltpu.VMEM((b, N_PAIRS, PAIR, PAIR), F32)],
        compiler_params=pltpu.CompilerParams(dimension_semantics=("arbitrary",),
                                             vmem_limit_bytes=VMEM_LIMIT),
    )(proj_rw, mu, w0, a0, wlora, wg, k_k, k_a, r_k, lnx_g, lnx_b)


def _moba_kernel(qt_ref, k_ref, vt_ref, km_ref, o_ref, s_ref, mx_ref, *, nb):
    bs = MOBA_BLOCK
    i = pl.program_id(2)
    qt = qt_ref[0, 0]
    km = km_ref[0, 0].astype(BF16)
    blk = lax.broadcasted_iota(jnp.int32, (nb, bs), 0).astype(F32)
    gate = jnp.where(blk < i.astype(F32), _dot(km, qt), NEG_INF)
    sel = jnp.zeros((nb, bs), F32)
    for _ in range(MOBA_TOPK):
        mx = jnp.max(gate, axis=0, keepdims=True)
        hit = (gate == mx) & (mx > NEG_INF)
        idx = jnp.min(jnp.where(hit, blk, float(nb)), axis=0, keepdims=True)
        pick = blk == idx
        sel = jnp.where(pick, 1.0, sel)
        gate = jnp.where(pick, NEG_INF, gate)
    bias = jnp.where(sel > 0.0, 0.0, MASK_BIAS)
    if nb < HEAD_DIM:
        bias = jnp.concatenate([bias, jnp.zeros((HEAD_DIM - nb, bs), F32)], axis=0)
    q_sel = jnp.concatenate([qt, bias.astype(BF16)], axis=0)
    q_own = jnp.concatenate([qt, jnp.zeros((HEAD_DIM, bs), BF16)], axis=0)

    start = pl.multiple_of(i * bs, bs)
    key_i = lax.broadcasted_iota(jnp.int32, (bs, bs), 0)
    qry_i = lax.broadcasted_iota(jnp.int32, (bs, bs), 1)
    s = jnp.where(key_i <= qry_i, _dot(k_ref[0, 0, pl.ds(start, bs), :], q_own), NEG_INF)
    m = jnp.max(s, axis=0, keepdims=True)
    p = jnp.exp2(s - m)
    acc = _dot(vt_ref[0, 0, :, pl.ds(start, bs)], p.astype(BF16))

    span = min(KEY_GROUP, nb) * bs

    n_groups = lax.shift_right_logical(i * bs + span - 1, span.bit_length() - 1)

    def scores(g):
        off = pl.multiple_of(g * span, span)
        return _dot(k_ref[0, 0, pl.ds(off, span), :], q_sel)

    def col_max(sc):
        return jnp.max(jnp.max(sc.reshape(span // bs, bs, bs), axis=0), axis=0, keepdims=True)

    @pl.when(n_groups > 0)
    def _():
        s0 = scores(0)
        s_ref[0] = s0
        mx_ref[...] = col_max(s0)

    def body(g, carry):
        m, acc = carry
        slot = lax.rem(g, 2)
        s = s_ref[slot]
        m_new = jnp.maximum(m, mx_ref[...])
        s_next = scores(jnp.minimum(g + 1, n_groups - 1))
        off = pl.multiple_of(g * span, span)
        p = jnp.exp2(s - m_new)
        acc = jnp.exp2(m - m_new) * acc + _dot(vt_ref[0, 0, :, pl.ds(off, span)], p.astype(BF16))
        s_ref[1 - slot] = s_next
        mx_ref[...] = col_max(s_next)
        return m_new, acc

    m, acc = lax.fori_loop(0, n_groups, body, (m, acc))
    out_t = acc[:HEAD_DIM] / acc[HEAD_DIM:HEAD_DIM + 1]
    o_ref[0, 0] = jnp.transpose(out_t).astype(o_ref.dtype)


def _moba(qt, kaug, vaug, kmean):
    b, nh, _, s = qt.shape
    nb = s // MOBA_BLOCK
    group = min(KEY_GROUP, nb)
    assert nb % group == 0
    return pl.pallas_call(
        functools.partial(_moba_kernel, nb=nb),
        grid=(b, nh, nb),
        in_specs=[pl.BlockSpec((1, 1, HEAD_DIM, MOBA_BLOCK), lambda bi, hi, qi: (bi, hi, 0, qi)),
                  pl.BlockSpec((1, 1, s, PAIR), lambda bi, hi, qi: (bi, hi, 0, 0)),
                  pl.BlockSpec((1, 1, PAIR, s), lambda bi, hi, qi: (bi, hi, 0, 0)),
                  pl.BlockSpec((1, 1, nb, HEAD_DIM), lambda bi, hi, qi: (bi, hi, 0, 0))],
        out_specs=pl.BlockSpec((1, 1, MOBA_BLOCK, HEAD_DIM), lambda bi, hi, qi: (bi, hi, qi, 0)),
        out_shape=jax.ShapeDtypeStruct((b, nh, s, HEAD_DIM), BF16),
        scratch_shapes=[pltpu.VMEM((2, group * MOBA_BLOCK, MOBA_BLOCK), F32),
                        pltpu.VMEM((1, MOBA_BLOCK), F32)],
        compiler_params=pltpu.CompilerParams(
            dimension_semantics=("parallel", "parallel", "arbitrary"),
            vmem_limit_bytes=VMEM_LIMIT),
    )(qt, kaug, vaug, kmean)


def _layer_norm(z, g, b):
    mu = jnp.mean(z, axis=1, keepdims=True)
    zc = z - mu
    var = jnp.mean(zc * zc, axis=1, keepdims=True)
    return zc * lax.rsqrt(var + LN_EPS) * g + b


def _merge_kernel(x_ref, ya_ref, yr_ref, wgate_ref, wab_ref, wrb_ref, wout_ref, g1_ref, b1_ref,
                  wr_ref, br_ref, h_ref, idx_ref, gate_ref, *, alpha):
    x = x_ref[...]
    d = x.shape[1]
    gates = _sigmoid(_dot(x.astype(BF16), wgate_ref[...]))
    mixed = (gates[:, :d] * _dot(ya_ref[...], wab_ref[...])
             + gates[:, d:] * _dot(yr_ref[...], wrb_ref[...]))
    h = _layer_norm(alpha * x + _dot(mixed.astype(BF16), wout_ref[...]), g1_ref[...], b1_ref[...])
    h_ref[...] = h

    logits = _dot(h, wr_ref[...], HI) + br_ref[...]
    col = lax.broadcasted_iota(jnp.int32, logits.shape, 1).astype(F32)
    idx_out = jnp.zeros(logits.shape, F32)
    val_out = jnp.zeros(logits.shape, F32)
    top = None
    denom = None
    for t in range(TOP_K):
        mx = jnp.max(logits, axis=1, keepdims=True)
        idx = jnp.min(jnp.where(logits == mx, col, float(ROUTER_PAD)), axis=1, keepdims=True)
        if t == 0:
            top = mx
        e = jnp.exp(mx - top)
        denom = e if t == 0 else denom + e
        idx_out = jnp.where(col == float(t), idx, idx_out)
        val_out = jnp.where(col == float(t), e, val_out)
        logits = jnp.where(col == idx, NEG_INF, logits)
    idx_ref[...] = idx_out.astype(jnp.int32)
    gate_ref[...] = val_out / denom


def _merge(x2, ya, yr, wgate, wab, wrb, wout, g1, b1, wr, br, alpha, tm=256):
    t, d = x2.shape
    row = lambda width: pl.BlockSpec((tm, width), lambda i: (i, 0))
    full = lambda arr: pl.BlockSpec(arr.shape, lambda i: (0, 0))
    return pl.pallas_call(
        functools.partial(_merge_kernel, alpha=alpha),
        grid=(t // tm,),
        in_specs=[row(d), row(WIDTH), row(WIDTH), full(wgate), full(wab), full(wrb), full(wout),
                  full(g1), full(b1), full(wr), full(br)],
        out_specs=[row(d), row(ROUTER_PAD), row(ROUTER_PAD)],
        out_shape=[jax.ShapeDtypeStruct((t, d), F32),
                   jax.ShapeDtypeStruct((t, ROUTER_PAD), jnp.int32),
                   jax.ShapeDtypeStruct((t, ROUTER_PAD), F32)],
        compiler_params=pltpu.CompilerParams(dimension_semantics=("parallel",),
                                             vmem_limit_bytes=VMEM_LIMIT),
    )(x2, ya, yr, wgate, wab, wrb, wout, g1, b1, wr, br)


DEINTERLEAVE_GROUP = 256


def _deinterleave_kernel(w_ref, perm_ref, o_ref):
    g = DEINTERLEAVE_GROUP
    n = w_ref.shape[2]
    half = n // 2
    for c in range(n // g):
        res = _dot(w_ref[0, :, c * g:(c + 1) * g].astype(BF16), perm_ref[...])
        o_ref[0, :, c * g // 2:(c + 1) * g // 2] = res[:, :g // 2].astype(BF16)
        o_ref[0, :, half + c * g // 2:half + (c + 1) * g // 2] = res[:, g // 2:].astype(BF16)


def _deinterleave(w1, tr=256):
    e, d, n = w1.shape
    g = DEINTERLEAVE_GROUP
    src = jnp.arange(g)
    dst = jnp.where(src % 2 == 0, src // 2, g // 2 + src // 2)
    perm = (dst[:, None] == jnp.arange(g)[None, :]).astype(BF16)
    return pl.pallas_call(
        _deinterleave_kernel,
        grid=(e, d // tr),
        in_specs=[pl.BlockSpec((1, tr, n), lambda ei, ri: (ei, ri, 0)),
                  pl.BlockSpec((g, g), lambda ei, ri: (0, 0))],
        out_specs=pl.BlockSpec((1, tr, n), lambda ei, ri: (ei, ri, 0)),
        out_shape=jax.ShapeDtypeStruct((e, d, n), BF16),
        compiler_params=pltpu.CompilerParams(dimension_semantics=("parallel", "parallel"),
                                             vmem_limit_bytes=VMEM_LIMIT),
    )(w1, perm)

def _row_copy(src_hbm, dst_buf, sem, src_row, slot, dst_row):
    return pltpu.make_async_copy(src_hbm.at[pl.ds(src_row, 1), :],
                                 dst_buf.at[slot, pl.ds(dst_row, 1), :],
                                 sem.at[slot])


def _gather_rows(src_hbm, dst_buf, sem, idx_ref, slot, n_rows):
    def issue(r, _):
        _row_copy(src_hbm, dst_buf, sem, idx_ref[0, 0, r], slot, r).start()
        return 0
    lax.fori_loop(0, n_rows, issue, 0)


def _wait_rows(src_hbm, dst_buf, sem, slot, n_rows):
    def wait(r, _):
        _row_copy(src_hbm, dst_buf, sem, 0, slot, r).wait()
        return 0
    lax.fori_loop(0, n_rows, wait, 0)


def _expert_kernel(be_ref, nact_ref, tok_ref, tok_next_ref, h_hbm, w1_ref, b1_ref, w2_ref, b2_ref,
                   y_ref, xbuf, sem):
    del be_ref
    rows = EXPERT_ROWS
    b = pl.program_id(0)
    nact = nact_ref[0]
    slot = b % 2

    @pl.when((b == 0) & (nact > 0))
    def _():
        _gather_rows(h_hbm, xbuf, sem, tok_ref, 0, rows)

    @pl.when(b + 1 < nact)
    def _():
        _gather_rows(h_hbm, xbuf, sem, tok_next_ref, 1 - slot, rows)

    @pl.when(b < nact)
    def _():
        _wait_rows(h_hbm, xbuf, sem, slot, rows)
        xs = xbuf[slot].astype(BF16)
        f = w2_ref.shape[1]
        hid = _dot(xs, w1_ref[0]) + b1_ref[0]
        x_glu = jnp.minimum(hid[:, :f], SWIGLU_LIMIT)
        x_lin = jnp.clip(hid[:, f:], -SWIGLU_LIMIT, SWIGLU_LIMIT)
        act = x_glu * _sigmoid(SWIGLU_ALPHA * x_glu) * (x_lin + 1.0)
        y_ref[...] = _dot(act.astype(BF16), w2_ref[0]) + b2_ref[0]

    @pl.when(b >= nact)
    def _():
        y_ref[...] = jnp.zeros_like(y_ref)


def _experts(block_expert, n_active, slot_tok, h, w1p, b1p, w2b, b2):
    t, d = h.shape
    nblk = block_expert.shape[0]
    rows = EXPERT_ROWS
    f = w2b.shape[1]
    tok3 = slot_tok.reshape(nblk, 1, rows)
    grid_spec = pltpu.PrefetchScalarGridSpec(
        num_scalar_prefetch=2,
        grid=(nblk,),
        in_specs=[
            pl.BlockSpec((1, 1, rows), lambda b, be, na: (b, 0, 0), memory_space=pltpu.SMEM),
            pl.BlockSpec((1, 1, rows), lambda b, be, na: (jnp.minimum(b + 1, nblk - 1), 0, 0),
                         memory_space=pltpu.SMEM),
            pl.BlockSpec(memory_space=pl.ANY),
            pl.BlockSpec((1, d, 2 * f), lambda b, be, na: (be[b], 0, 0)),
            pl.BlockSpec((1, 1, 2 * f), lambda b, be, na: (be[b], 0, 0)),
            pl.BlockSpec((1, f, d), lambda b, be, na: (be[b], 0, 0)),
            pl.BlockSpec((1, 1, d), lambda b, be, na: (be[b], 0, 0)),
        ],
        out_specs=pl.BlockSpec((rows, d), lambda b, be, na: (b, 0)),
        scratch_shapes=[pltpu.VMEM((2, rows, d), F32), pltpu.SemaphoreType.DMA((2,))],
    )
    return pl.pallas_call(
        _expert_kernel,
        grid_spec=grid_spec,
        out_shape=jax.ShapeDtypeStruct((nblk * rows, d), F32),
        compiler_params=pltpu.CompilerParams(dimension_semantics=("arbitrary",),
                                             vmem_limit_bytes=VMEM_LIMIT),
    )(block_expert, n_active, tok3, tok3, h, w1p, b1p, w2b, b2)


def _combine_kernel(dest_ref, dest_next_ref, h_ref, gate_ref, g2_ref, b2_ref, y_hbm, o_ref, ybuf, sem,
                    *, alpha, nsteps):
    tm = COMBINE_ROWS
    n = TOP_K * tm
    s = pl.program_id(0)
    slot = s % 2

    @pl.when(s == 0)
    def _():
        _gather_rows(y_hbm, ybuf, sem, dest_ref, 0, n)

    @pl.when(s + 1 < nsteps)
    def _():
        _gather_rows(y_hbm, ybuf, sem, dest_next_ref, 1 - slot, n)

    _wait_rows(y_hbm, ybuf, sem, slot, n)
    gates = gate_ref[...]
    moe = gates[:, 0:1] * ybuf[slot, 0:tm, :]
    for kk in range(1, TOP_K):
        moe = moe + gates[:, kk:kk + 1] * ybuf[slot, kk * tm:(kk + 1) * tm, :]
    o_ref[...] = _layer_norm(alpha * h_ref[...] + moe, g2_ref[...], b2_ref[...])


def _combine(dest_km, h, gate_pad, g2, b2, y_sorted, alpha):
    t, d = h.shape
    tm = COMBINE_ROWS
    nsteps = t // tm
    n = TOP_K * tm
    return pl.pallas_call(
        functools.partial(_combine_kernel, alpha=alpha, nsteps=nsteps),
        grid=(nsteps,),
        in_specs=[
            pl.BlockSpec((1, 1, n), lambda s: (s, 0, 0), memory_space=pltpu.SMEM),
            pl.BlockSpec((1, 1, n), lambda s: (jnp.minimum(s + 1, nsteps - 1), 0, 0),
                         memory_space=pltpu.SMEM),
            pl.BlockSpec((tm, d), lambda s: (s, 0)),
            pl.BlockSpec((tm, ROUTER_PAD), lambda s: (s, 0)),
            pl.BlockSpec((1, d), lambda s: (0, 0)),
            pl.BlockSpec((1, d), lambda s: (0, 0)),
            pl.BlockSpec(memory_space=pl.ANY),
        ],
        out_specs=pl.BlockSpec((tm, d), lambda s: (s, 0)),
        out_shape=jax.ShapeDtypeStruct((t, d), F32),
        scratch_shapes=[pltpu.VMEM((2, n, d), F32), pltpu.SemaphoreType.DMA((2,))],
        compiler_params=pltpu.CompilerParams(dimension_semantics=("arbitrary",),
                                             vmem_limit_bytes=VMEM_LIMIT),
    )(dest_km, dest_km, h, gate_pad, g2, b2, y_sorted)


def _routing_tables(top_idx, t):
    rows = EXPERT_ROWS
    n_slots = t * TOP_K
    nblk = n_slots // rows + N_EXPERTS
    flat_e = top_idx.reshape(-1)
    order = jnp.argsort(flat_e)
    e_sorted = flat_e[order]
    counts = jnp.bincount(flat_e, length=N_EXPERTS)
    starts = jnp.cumsum(counts) - counts
    padded = (counts + rows - 1) // rows * rows
    pad_ends = jnp.cumsum(padded)
    pad_starts = pad_ends - padded
    dest_sorted = (pad_starts[e_sorted] + jnp.arange(n_slots) - starts[e_sorted]).astype(jnp.int32)
    slot_tok = jnp.zeros((nblk * rows,), jnp.int32).at[dest_sorted].set((order // TOP_K).astype(jnp.int32))
    dest = jnp.zeros((n_slots,), jnp.int32).at[order].set(dest_sorted)
    block_start = jnp.arange(nblk) * rows
    block_expert = jnp.minimum(jnp.searchsorted(pad_ends, block_start, side='right'),
                               N_EXPERTS - 1).astype(jnp.int32)
    n_active = (pad_ends[-1] // rows).astype(jnp.int32).reshape(1)
    return slot_tok, dest, block_expert, n_active


def _layer(x, w_in, mu_shift, w0, w_decay_up, a0, w_aaa_up, w_gate_up, k_k, k_a, r_k, lnx_g, lnx_b,
           w_attn_br, w_rwkv_br, w_out, ln1_g, ln1_b, w_router, b_router, w1, b1, w2, b2, ln2_g, ln2_b,
           alpha):
    b, s, d = x.shape
    t = b * s
    x2 = x.reshape(t, d)
    xb = x2.astype(BF16)
    row = lambda vec: vec.reshape(1, -1)

    off_q = RW_COLS
    off_gate = off_q + 3 * WIDTH
    pad_cols = RW_COLS_PAD - RW_COLS
    w_rw = jnp.pad(w_in[:, :RW_COLS], ((0, 0), (0, pad_cols))).astype(BF16)
    mu = jnp.pad(mu_shift, (0, pad_cols)).reshape(1, -1)
    w_qkv = w_in[:, off_q:off_gate].astype(BF16)
    w_gate = w_in[:, off_gate:].astype(BF16)

    half = HEAD_DIM // 2
    inv_freq = ROPE_THETA ** (-jnp.arange(half, dtype=F32) / half)
    ang = jnp.arange(s, dtype=F32)[:, None] * inv_freq[None, :]
    cos, sin = jnp.cos(ang), jnp.sin(ang)
    cos_t = jnp.concatenate([cos, cos, cos, cos], axis=1)
    sin_t = jnp.concatenate([-sin, sin, -sin, sin], axis=1)

    proj_rw = _project(xb, w_rw).reshape(b, s, RW_COLS_PAD)
    qt, kaug, vaug, kmean = _project_qkv(xb, w_qkv, cos_t, sin_t, b, s)
    kmean = kmean.reshape(b, s // MOBA_BLOCK, N_HEADS, HEAD_DIM).transpose(0, 2, 1, 3)

    zeros = jnp.zeros((DECAY_LORA, WIDTH), F32)
    w_lora = jnp.concatenate([jnp.concatenate([w_decay_up, zeros], axis=1),
                              jnp.concatenate([zeros, w_aaa_up], axis=1)], axis=0).astype(BF16)
    w_g = jnp.pad(w_gate_up, ((0, GATE_LORA_PAD - GATE_LORA), (0, 0))).astype(BF16)
    y_rwkv = _rwkv(proj_rw, mu, row(w0), row(a0), w_lora, w_g, row(k_k), row(k_a), row(r_k),
                   row(lnx_g), row(lnx_b))
    y_attn = _moba(qt, kaug, vaug, kmean).transpose(0, 2, 1, 3)

    w_r = jnp.pad(w_router, ((0, 0), (0, ROUTER_PAD - N_EXPERTS)))
    b_r = jnp.pad(b_router, (0, ROUTER_PAD - N_EXPERTS), constant_values=NEG_INF).reshape(1, -1)
    h, idx_pad, gate_pad = _merge(x2, y_attn.reshape(t, WIDTH), y_rwkv.reshape(t, WIDTH), w_gate,
                                  w_attn_br.astype(BF16), w_rwkv_br.astype(BF16), w_out.astype(BF16),
                                  row(ln1_g), row(ln1_b), w_r, b_r, alpha)

    slot_tok, dest, block_expert, n_active = _routing_tables(idx_pad[:, :TOP_K], t)
    w1p = _deinterleave(w1)
    b1p = jnp.concatenate([b1[:, 0::2], b1[:, 1::2]], axis=1)[:, None, :]
    y_sorted = _experts(block_expert, n_active, slot_tok, h, w1p, b1p, w2.astype(BF16), b2[:, None, :])

    tm = COMBINE_ROWS
    dest_km = dest.reshape(t // tm, tm, TOP_K).transpose(0, 2, 1).reshape(t // tm, 1, TOP_K * tm)
    out = _combine(dest_km, h, gate_pad, row(ln2_g), row(ln2_b), y_sorted, alpha)
    return out.reshape(b, s, d)


def kernel(x, w_in, mu_shift, w0, w_decay_up, a0, w_aaa_up, w_gate_up, k_k, k_a, r_k, lnx_g, lnx_b,
           w_attn_br, w_rwkv_br, w_out, ln1_g, ln1_b, w_router, b_router, w1, b1, w2, b2, ln2_g, ln2_b):
    depth = w_in.shape[0]
    alpha = (2 * depth) ** 0.25
    for l in range(depth):
        x = _layer(x, w_in[l], mu_shift[l], w0[l], w_decay_up[l], a0[l], w_aaa_up[l], w_gate_up[l],
                   k_k[l], k_a[l], r_k[l].reshape(-1), lnx_g[l], lnx_b[l], w_attn_br[l], w_rwkv_br[l],
                   w_out[l], ln1_g[l], ln1_b[l], w_router[l], b_router[l], w1[l], b1[l], w2[l], b2[l],
                   ln2_g[l], ln2_b[l], alpha)
    return x
```

```python
import functools

import jax
import jax.numpy as jnp
from jax import lax
from jax.experimental import pallas as pl
from jax.experimental.pallas import tpu as pltpu

F32 = jnp.float32
BF16 = jnp.bfloat16
HI = lax.Precision.HIGHEST

HEAD_DIM = 64
N_HEADS = 8
WIDTH = N_HEADS * HEAD_DIM
PAIR = 2 * HEAD_DIM
N_PAIRS = N_HEADS // 2
MOBA_BLOCK = 256
MOBA_TOPK = 3
KEY_GROUP = 8
ROPE_THETA = 10000.0
DECAY_LORA = 64
AAA_LORA = 64
GATE_LORA = 160
GATE_LORA_PAD = 256
RW_COLS = 3 * WIDTH + DECAY_LORA + AAA_LORA + GATE_LORA
RW_COLS_PAD = 3 * WIDTH + DECAY_LORA + AAA_LORA + GATE_LORA_PAD
GN_EPS = 64e-5
LN_EPS = 1e-5
N_EXPERTS = 32
TOP_K = 4
ROUTER_PAD = 128
SWIGLU_ALPHA = 1.702
SWIGLU_LIMIT = 7.0
RWKV_CHUNK = 64
EXPERT_ROWS = 256
COMBINE_ROWS = 128
VMEM_LIMIT = 48 * 1024 * 1024

NEG_INF = float("-inf")
LOG2_E = 1.4426950408889634
MASK_BIAS = -1e30


def _nt(a, b, precision=None):
    return lax.dot_general(a, b, (((1,), (1,)), ((), ())), precision=precision,
                           preferred_element_type=F32)


def _dot(a, b, precision=None):
    return jnp.dot(a, b, precision=precision, preferred_element_type=F32)


def _matmul_kernel(x_ref, w_ref, o_ref):
    o_ref[...] = _dot(x_ref[...], w_ref[...])


def _project(xb, w, tm=512):
    t, d = xb.shape
    n = w.shape[1]
    return pl.pallas_call(
        _matmul_kernel,
        grid=(t // tm,),
        in_specs=[pl.BlockSpec((tm, d), lambda i: (i, 0)),
                  pl.BlockSpec((d, n), lambda i: (0, 0))],
        out_specs=pl.BlockSpec((tm, n), lambda i: (i, 0)),
        out_shape=jax.ShapeDtypeStruct((t, n), F32),
        compiler_params=pltpu.CompilerParams(dimension_semantics=("parallel",),
                                             vmem_limit_bytes=VMEM_LIMIT),
    )(xb, w)


def _qkv_kernel(x_ref, w_ref, cos_ref, sin_ref, qt_ref, kaug_ref, vaug_ref, km_ref, *, tm, steps_per_seq):
    acc = _dot(x_ref[...], w_ref[...])
    cos = jnp.concatenate([cos_ref[...]] * (WIDTH // PAIR), axis=1)
    sin = jnp.concatenate([sin_ref[...]] * (WIDTH // PAIR), axis=1)
    lane = lax.broadcasted_iota(jnp.int32, (tm, WIDTH), 1)
    first_half = (lane & (HEAD_DIM // 2)) == 0

    def rope(t):
        partner = jnp.where(first_half, pltpu.roll(t, WIDTH - HEAD_DIM // 2, 1),
                            pltpu.roll(t, HEAD_DIM // 2, 1))
        return t * cos + partner * sin

    q = rope(acc[:, :WIDTH]) * (HEAD_DIM ** -0.5 * LOG2_E)
    k = rope(acc[:, WIDTH:2 * WIDTH])
    v = acc[:, 2 * WIDTH:]
    for j in range(tm // MOBA_BLOCK):
        km_ref[0, j:j + 1, :] = jnp.mean(k[j * MOBA_BLOCK:(j + 1) * MOBA_BLOCK], axis=0, keepdims=True)

    lane_p = lax.broadcasted_iota(jnp.int32, (tm, PAIR), 1)
    row_p = lax.broadcasted_iota(jnp.int32, (tm, PAIR), 0)
    first_block = (pl.program_id(0) % steps_per_seq) * (tm // MOBA_BLOCK)
    row_block = lax.shift_right_logical(row_p, MOBA_BLOCK.bit_length() - 1)
    block_tag = jnp.where(lane_p - HEAD_DIM == first_block + row_block, 1.0, 0.0)
    ones = jnp.ones((HEAD_DIM, tm), F32)
    for pp in range(N_PAIRS):
        sl = slice(pp * PAIR, (pp + 1) * PAIR)
        q_t = jnp.transpose(q[:, sl])
        v_t = jnp.transpose(v[:, sl])
        k_p = k[:, sl]
        k_sw = pltpu.roll(k_p, HEAD_DIM, 1)
        for h, k_h in ((0, k_p), (1, k_sw)):
            rows = slice(h * HEAD_DIM, (h + 1) * HEAD_DIM)
            qt_ref[0, 2 * pp + h] = q_t[rows].astype(BF16)
            kaug_ref[0, 2 * pp + h] = jnp.where(lane_p < HEAD_DIM, k_h, block_tag).astype(BF16)
            vaug_ref[0, 2 * pp + h] = jnp.concatenate([v_t[rows], ones], axis=0).astype(BF16)


def _project_qkv(xb, w, cos_t, sin_t, batch, seq, tm=512):
    t, d = xb.shape
    steps_per_seq = seq // tm
    assert seq // MOBA_BLOCK <= HEAD_DIM, "block one-hot tags must fit the spare lanes of a head"
    tab_spec = pl.BlockSpec((tm, PAIR), lambda i: (i % steps_per_seq, 0))
    return pl.pallas_call(
        functools.partial(_qkv_kernel, tm=tm, steps_per_seq=steps_per_seq),
        grid=(t // tm,),
        in_specs=[pl.BlockSpec((tm, d), lambda i: (i, 0)),
                  pl.BlockSpec((d, 3 * WIDTH), lambda i: (0, 0)),
                  tab_spec, tab_spec],
        out_specs=[
            pl.BlockSpec((1, N_HEADS, HEAD_DIM, tm), lambda i: (i // steps_per_seq, 0, 0, i % steps_per_seq)),
            pl.BlockSpec((1, N_HEADS, tm, PAIR), lambda i: (i // steps_per_seq, 0, i % steps_per_seq, 0)),
            pl.BlockSpec((1, N_HEADS, PAIR, tm), lambda i: (i // steps_per_seq, 0, 0, i % steps_per_seq)),
            pl.BlockSpec((1, tm // MOBA_BLOCK, WIDTH), lambda i: (i, 0, 0))],
        out_shape=[jax.ShapeDtypeStruct((batch, N_HEADS, HEAD_DIM, seq), BF16),
                   jax.ShapeDtypeStruct((batch, N_HEADS, seq, PAIR), BF16),
                   jax.ShapeDtypeStruct((batch, N_HEADS, PAIR, seq), BF16),
                   jax.ShapeDtypeStruct((t // tm, tm // MOBA_BLOCK, WIDTH), F32)],
        compiler_params=pltpu.CompilerParams(dimension_semantics=("parallel",),
                                             vmem_limit_bytes=VMEM_LIMIT),
    )(xb, w, cos_t, sin_t)


SPLIT_PARTS = 2


def _parts(x, n=SPLIT_PARTS):
    out = []
    for _ in range(n):
        piece = x.astype(BF16)
        out.append(piece)
        x = x - piece.astype(F32)
    return out


def _mm(a_parts, b_parts, f=None):
    f = f or _dot
    order = max(len(a_parts), len(b_parts))
    acc = None
    for i, a in enumerate(a_parts):
        for j, b in enumerate(b_parts):
            if i + j < order:
                term = f(a, b)
                acc = term if acc is None else acc + term
    return acc


def _softplus(z):
    return jnp.maximum(z, 0.0) + jnp.log(1.0 + jnp.exp(-jnp.abs(z)))


def _sigmoid(z):
    return 1.0 / (1.0 + jnp.exp(-z))


def _rwkv_kernel(p_ref, mu_ref, w0_ref, a0_ref, wlora_ref, wg_ref, kk_ref, ka_ref, rk_ref,
                 lng_ref, lnb_ref, y_ref, carry_ref, state_ref):
    c = RWKV_CHUNK
    nbatch = p_ref.shape[0]

    @pl.when(pl.program_id(0) == 0)
    def _():
        carry_ref[...] = jnp.zeros_like(carry_ref)
        state_ref[...] = jnp.zeros_like(state_ref)

    ri = lax.broadcasted_iota(jnp.int32, (PAIR, PAIR), 0)
    ci = lax.broadcasted_iota(jnp.int32, (PAIR, PAIR), 1)
    head_sum = jnp.where((ri // HEAD_DIM) == (ci // HEAD_DIM), 1.0, 0.0).astype(F32)
    eye = jnp.where(ri == ci, 1.0, 0.0).astype(F32)
    strict_lower = ri > ci
    lower = ri >= ci
    rc = lax.broadcasted_iota(jnp.int32, (c, c), 0)
    cc = lax.broadcasted_iota(jnp.int32, (c, c), 1)
    cumsum_mat = jnp.where(rc >= cc, 1.0, 0.0).astype(F32)
    lane_p = lax.broadcasted_iota(jnp.int32, (c, PAIR), 1)
    head0 = lane_p < HEAD_DIM

    def stack(t):
        return jnp.concatenate([jnp.where(head0, t, 0.0), jnp.where(head0, 0.0, t)], axis=0)

    head_sum_b = [head_sum.astype(BF16)]
    cumsum_b = [cumsum_mat.astype(BF16)]

    chains = []
    for bi in range(nbatch):
        p = p_ref[bi]
        row = lax.broadcasted_iota(jnp.int32, p.shape, 0)
        prev = jnp.where(row == 0, carry_ref[bi, 0:1, :], pltpu.roll(p, 1, 0))
        carry_ref[bi] = jnp.broadcast_to(p[c - 1:c, :], carry_ref.shape[1:])
        sh = p + (prev - p) * mu_ref[...]
        r = sh[:, 0:WIDTH]
        k = sh[:, WIDTH:2 * WIDTH]
        v = sh[:, 2 * WIDTH:3 * WIDTH]
        lora = sh[:, 3 * WIDTH:3 * WIDTH + PAIR]
        hg = sh[:, 3 * WIDTH + PAIR:]
        lane_l = lax.broadcasted_iota(jnp.int32, lora.shape, 1)
        lora_act = jnp.where(lane_l < DECAY_LORA, jnp.tanh(lora), lora)
        wa = _dot(lora_act.astype(BF16), wlora_ref[...])
        w_log = -_softplus(-(w0_ref[...] + wa[:, :WIDTH])) - 0.5
        logw = -jnp.exp(w_log)
        a = _sigmoid(a0_ref[...] + wa[:, WIDTH:])
        g = _dot(_sigmoid(hg).astype(BF16), wg_ref[...])
        kkn = k * kk_ref[...]
        k2 = k * (1.0 + (a - 1.0) * ka_ref[...])
        cum_all = _mm(cumsum_b, _parts(logw, 3))
        for pp in range(N_PAIRS):
            sl = slice(pp * PAIR, (pp + 1) * PAIR)
            chains.append(dict(bi=bi, pp=pp, sl=sl, r=r[:, sl], k=k2[:, sl], v=v[:, sl], a=a[:, sl],
                               kk=kkn[:, sl], lw=logw[:, sl], cum=cum_all[:, sl], g=g[:, sl]))

    for ch in chains:
        ch['ss'] = _mm(_parts(ch['kk'] * ch['kk']), head_sum_b)
    for ch in chains:
        kap = ch['kk'] / jnp.maximum(jnp.sqrt(ch['ss']), 1e-12)
        cum = ch['cum']
        ch['pc'] = jnp.exp(cum[c - 1:c, :])
        inv = jnp.exp(-cum)
        rm = stack(ch['r'] * jnp.exp(cum))
        bm = stack(kap * jnp.exp(cum - ch['lw']))
        am = stack(-(kap * ch['a']) * inv)
        km = stack(ch['k'] * inv)
        ch.update(rm=rm, bm=bm, am=am, km=km, vm=stack(ch['v']))
    for ch in chains:
        ch['sb'] = _mm(_parts(jnp.concatenate([ch['bm'], ch['rm']], axis=0)),
                       _parts(jnp.concatenate([ch['am'], ch['km']], axis=0)), _nt)
    for ch in chains:
        sb = ch['sb']
        ch['la'] = jnp.where(strict_lower, sb[:2 * c, :2 * c], 0.0)
        ch['lk'] = jnp.where(strict_lower, sb[:2 * c, 2 * c:], 0.0)
        ch['ma'] = jnp.where(lower, sb[2 * c:, :2 * c], 0.0)
        ch['mk'] = jnp.where(lower, sb[2 * c:, 2 * c:], 0.0)
        ch['tinv'] = eye + ch['la']
        ch['lpow'] = ch['la']
    n = 2
    while n < c:
        for ch in chains:
            lp = _parts(ch['lpow'])
            ch['lpow'] = _mm(lp, lp)
        for ch in chains:
            ch['tinv'] = ch['tinv'] + _mm(_parts(ch['tinv']), _parts(ch['lpow']))
        n *= 2

    for ch in chains:
        ch['h0'] = state_ref[ch['bi'], ch['pp']]
        ch['rhs'] = _mm(_parts(jnp.concatenate([ch['bm'], ch['lk']], axis=1)),
                        _parts(jnp.concatenate([ch['h0'], ch['vm']], axis=0)))
    for ch in chains:
        ch['u'] = _mm(_parts(ch['tinv']), _parts(ch['rhs']))
    for ch in chains:
        yst = _mm(_parts(jnp.concatenate([ch['rm'], ch['ma'], ch['mk']], axis=1)),
                  _parts(jnp.concatenate([ch['h0'], ch['u'], ch['vm']], axis=0)))
        ch['y'] = yst[:c] + yst[c:]
    for ch in chains:
        pc = ch['pc']
        pc_col = jnp.transpose(jnp.broadcast_to(pc, (PAIR, PAIR)))
        upd = _mm(_parts(jnp.concatenate([jnp.transpose(ch['am'] * pc), jnp.transpose(ch['km'] * pc)],
                                         axis=1)),
                  _parts(jnp.concatenate([ch['u'], ch['vm']], axis=0)))
        state_ref[ch['bi'], ch['pp']] = ch['h0'] * pc_col + upd
    for ch in chains:
        ch['mean'] = _mm(_parts(ch['y']), head_sum_b) * (1.0 / HEAD_DIM)
        ch['bonus'] = _mm(_parts(ch['r'] * ch['k'] * rk_ref[:, ch['sl']]), head_sum_b) * ch['v']
    for ch in chains:
        yc = ch['y'] - ch['mean']
        ch['yc'] = yc
        ch['var'] = _mm(_parts(yc * yc), head_sum_b) * (1.0 / HEAD_DIM)
    for ch in chains:
        sl = ch['sl']
        yn = ch['yc'] * lax.rsqrt(ch['var'] + GN_EPS) * lng_ref[:, sl] + lnb_ref[:, sl]
        y_ref[ch['bi'], :, sl] = ((yn + ch['bonus']) * ch['g']).astype(y_ref.dtype)


def _rwkv(proj_rw, mu, w0, a0, wlora, wg, k_k, k_a, r_k, lnx_g, lnx_b):
    b, s, n = proj_rw.shape
    c = RWKV_CHUNK
    vec = lambda width: pl.BlockSpec((1, width), lambda ci: (0, 0))
    return pl.pallas_call(
        _rwkv_kernel,
        grid=(s // c,),
        in_specs=[pl.BlockSpec((b, c, n), lambda ci: (0, ci, 0)),
                  vec(n), vec(WIDTH), vec(WIDTH),
                  pl.BlockSpec(wlora.shape, lambda ci: (0, 0)),
                  pl.BlockSpec(wg.shape, lambda ci: (0, 0)),
                  vec(WIDTH), vec(WIDTH), vec(WIDTH), vec(WIDTH), vec(WIDTH)],
        out_specs=pl.BlockSpec((b, c, WIDTH), lambda ci: (0, ci, 0)),
        out_shape=jax.ShapeDtypeStruct((b, s, WIDTH), BF16),
        scratch_shapes=[pltpu.VMEM((b, 8, n), F32),
                        pltpu.VMEM((b, N_PAIRS, PAIR, PAIR), F32)],
        compiler_params=pltpu.CompilerParams(dimension_semantics=("arbitrary",),
                                             vmem_limit_bytes=VMEM_LIMIT),
    )(proj_rw, mu, w0, a0, wlora, wg, k_k, k_a, r_k, lnx_g, lnx_b)


def _moba_kernel(qt_ref, k_ref, vt_ref, km_ref, o_ref, s_ref, mx_ref, *, nb):
    bs = MOBA_BLOCK
    i = pl.program_id(2)
    qt = qt_ref[0, 0]
    km = km_ref[0, 0].astype(BF16)
    blk = lax.broadcasted_iota(jnp.int32, (nb, bs), 0).astype(F32)
    gate = jnp.where(blk < i.astype(F32), _dot(km, qt), NEG_INF)
    sel = jnp.zeros((nb, bs), F32)
    for _ in range(MOBA_TOPK):
        mx = jnp.max(gate, axis=0, keepdims=True)
        hit = (gate == mx) & (mx > NEG_INF)
        idx = jnp.min(jnp.where(hit, blk, float(nb)), axis=0, keepdims=True)
        pick = blk == idx
        sel = jnp.where(pick, 1.0, sel)
        gate = jnp.where(pick, NEG_INF, gate)
    bias = jnp.where(sel > 0.0, 0.0, MASK_BIAS)
    if nb < HEAD_DIM:
        bias = jnp.concatenate([bias, jnp.zeros((HEAD_DIM - nb, bs), F32)], axis=0)
    q_sel = jnp.concatenate([qt, bias.astype(BF16)], axis=0)
    q_own = jnp.concatenate([qt, jnp.zeros((HEAD_DIM, bs), BF16)], axis=0)

    start = pl.multiple_of(i * bs, bs)
    key_i = lax.broadcasted_iota(jnp.int32, (bs, bs), 0)
    qry_i = lax.broadcasted_iota(jnp.int32, (bs, bs), 1)
    s = jnp.where(key_i <= qry_i, _dot(k_ref[0, 0, pl.ds(start, bs), :], q_own), NEG_INF)
    m = jnp.max(s, axis=0, keepdims=True)
    p = jnp.exp2(s - m)
    acc = _dot(vt_ref[0, 0, :, pl.ds(start, bs)], p.astype(BF16))

    span = min(KEY_GROUP, nb) * bs

    n_groups = lax.shift_right_logical(i * bs + span - 1, span.bit_length() - 1)

    def scores(g):
        off = pl.multiple_of(g * span, span)
        return _dot(k_ref[0, 0, pl.ds(off, span), :], q_sel)

    def col_max(sc):
        return jnp.max(jnp.max(sc.reshape(span // bs, bs, bs), axis=0), axis=0, keepdims=True)

    @pl.when(n_groups > 0)
    def _():
        s0 = scores(0)
        s_ref[0] = s0
        mx_ref[...] = col_max(s0)

    def body(g, carry):
        m, acc = carry
        slot = lax.rem(g, 2)
        s = s_ref[slot]
        m_new = jnp.maximum(m, mx_ref[...])
        s_next = scores(jnp.minimum(g + 1, n_groups - 1))
        off = pl.multiple_of(g * span, span)
        p = jnp.exp2(s - m_new)
        acc = jnp.exp2(m - m_new) * acc + _dot(vt_ref[0, 0, :, pl.ds(off, span)], p.astype(BF16))
        s_ref[1 - slot] = s_next
        mx_ref[...] = col_max(s_next)
        return m_new, acc

    m, acc = lax.fori_loop(0, n_groups, body, (m, acc))
    out_t = acc[:HEAD_DIM] / acc[HEAD_DIM:HEAD_DIM + 1]
    o_ref[0, 0] = jnp.transpose(out_t).astype(o_ref.dtype)


def _moba(qt, kaug, vaug, kmean):
    b, nh, _, s = qt.shape
    nb = s // MOBA_BLOCK
    group = min(KEY_GROUP, nb)
    assert nb % group == 0
    return pl.pallas_call(
        functools.partial(_moba_kernel, nb=nb),
        grid=(b, nh, nb),
        in_specs=[pl.BlockSpec((1, 1, HEAD_DIM, MOBA_BLOCK), lambda bi, hi, qi: (bi, hi, 0, qi)),
                  pl.BlockSpec((1, 1, s, PAIR), lambda bi, hi, qi: (bi, hi, 0, 0)),
                  pl.BlockSpec((1, 1, PAIR, s), lambda bi, hi, qi: (bi, hi, 0, 0)),
                  pl.BlockSpec((1, 1, nb, HEAD_DIM), lambda bi, hi, qi: (bi, hi, 0, 0))],
        out_specs=pl.BlockSpec((1, 1, MOBA_BLOCK, HEAD_DIM), lambda bi, hi, qi: (bi, hi, qi, 0)),
        out_shape=jax.ShapeDtypeStruct((b, nh, s, HEAD_DIM), BF16),
        scratch_shapes=[pltpu.VMEM((2, group * MOBA_BLOCK, MOBA_BLOCK), F32),
                        pltpu.VMEM((1, MOBA_BLOCK), F32)],
        compiler_params=pltpu.CompilerParams(
            dimension_semantics=("parallel", "parallel", "arbitrary"),
            vmem_limit_bytes=VMEM_LIMIT),
    )(qt, kaug, vaug, kmean)


def _layer_norm(z, g, b):
    mu = jnp.mean(z, axis=1, keepdims=True)
    zc = z - mu
    var = jnp.mean(zc * zc, axis=1, keepdims=True)
    return zc * lax.rsqrt(var + LN_EPS) * g + b


def _merge_kernel(x_ref, ya_ref, yr_ref, wgate_ref, wab_ref, wrb_ref, wout_ref, g1_ref, b1_ref,
                  wr_ref, br_ref, h_ref, idx_ref, gate_ref, *, alpha):
    x = x_ref[...]
    d = x.shape[1]
    gates = _sigmoid(_dot(x.astype(BF16), wgate_ref[...]))
    mixed = (gates[:, :d] * _dot(ya_ref[...], wab_ref[...])
             + gates[:, d:] * _dot(yr_ref[...], wrb_ref[...]))
    h = _layer_norm(alpha * x + _dot(mixed.astype(BF16), wout_ref[...]), g1_ref[...], b1_ref[...])
    h_ref[...] = h

    logits = _dot(h, wr_ref[...], HI) + br_ref[...]
    col = lax.broadcasted_iota(jnp.int32, logits.shape, 1).astype(F32)
    idx_out = jnp.zeros(logits.shape, F32)
    val_out = jnp.zeros(logits.shape, F32)
    top = None
    denom = None
    for t in range(TOP_K):
        mx = jnp.max(logits, axis=1, keepdims=True)
        idx = jnp.min(jnp.where(logits == mx, col, float(ROUTER_PAD)), axis=1, keepdims=True)
        if t == 0:
            top = mx
        e = jnp.exp(mx - top)
        denom = e if t == 0 else denom + e
        idx_out = jnp.where(col == float(t), idx, idx_out)
        val_out = jnp.where(col == float(t), e, val_out)
        logits = jnp.where(col == idx, NEG_INF, logits)
    idx_ref[...] = idx_out.astype(jnp.int32)
    gate_ref[...] = val_out / denom


def _merge(x2, ya, yr, wgate, wab, wrb, wout, g1, b1, wr, br, alpha, tm=256):
    t, d = x2.shape
    row = lambda width: pl.BlockSpec((tm, width), lambda i: (i, 0))
    full = lambda arr: pl.BlockSpec(arr.shape, lambda i: (0, 0))
    return pl.pallas_call(
        functools.partial(_merge_kernel, alpha=alpha),
        grid=(t // tm,),
        in_specs=[row(d), row(WIDTH), row(WIDTH), full(wgate), full(wab), full(wrb), full(wout),
                  full(g1), full(b1), full(wr), full(br)],
        out_specs=[row(d), row(ROUTER_PAD), row(ROUTER_PAD)],
        out_shape=[jax.ShapeDtypeStruct((t, d), F32),
                   jax.ShapeDtypeStruct((t, ROUTER_PAD), jnp.int32),
                   jax.ShapeDtypeStruct((t, ROUTER_PAD), F32)],
        compiler_params=pltpu.CompilerParams(dimension_semantics=("parallel",),
                                             vmem_limit_bytes=VMEM_LIMIT),
    )(x2, ya, yr, wgate, wab, wrb, wout, g1, b1, wr, br)


DEINTERLEAVE_GROUP = 256


def _deinterleave_kernel(w_ref, perm_ref, o_ref):
    g = DEINTERLEAVE_GROUP
    n = w_ref.shape[2]
    half = n // 2
    for c in range(n // g):
        res = _dot(w_ref[0, :, c * g:(c + 1) * g].astype(BF16), perm_ref[...])
        o_ref[0, :, c * g // 2:(c + 1) * g // 2] = res[:, :g // 2].astype(BF16)
        o_ref[0, :, half + c * g // 2:half + (c + 1) * g // 2] = res[:, g // 2:].astype(BF16)


def _deinterleave(w1, tr=256):
    e, d, n = w1.shape
    g = DEINTERLEAVE_GROUP
    src = jnp.arange(g)
    dst = jnp.where(src % 2 == 0, src // 2, g // 2 + src // 2)
    perm = (dst[:, None] == jnp.arange(g)[None, :]).astype(BF16)
    return pl.pallas_call(
        _deinterleave_kernel,
        grid=(e, d // tr),
        in_specs=[pl.BlockSpec((1, tr, n), lambda ei, ri: (ei, ri, 0)),
                  pl.BlockSpec((g, g), lambda ei, ri: (0, 0))],
        out_specs=pl.BlockSpec((1, tr, n), lambda ei, ri: (ei, ri, 0)),
        out_shape=jax.ShapeDtypeStruct((e, d, n), BF16),
        compiler_params=pltpu.CompilerParams(dimension_semantics=("parallel", "parallel"),
                                             vmem_limit_bytes=VMEM_LIMIT),
    )(w1, perm)

def _row_copy(src_hbm, dst_buf, sem, src_row, slot, dst_row):
    return pltpu.make_async_copy(src_hbm.at[pl.ds(src_row, 1), :],
                                 dst_buf.at[slot, pl.ds(dst_row, 1), :],
                                 sem.at[slot])


GATHER_UNROLL = 8


def _gather_rows(src_hbm, dst_buf, sem, idx_ref, slot, n_rows):
    def issue(r, _):
        _row_copy(src_hbm, dst_buf, sem, idx_ref[0, 0, r], slot, r).start()
        return 0
    lax.fori_loop(0, n_rows, issue, 0, unroll=GATHER_UNROLL)


def _wait_rows(src_hbm, dst_buf, sem, slot, n_rows):
    pltpu.make_async_copy(src_hbm.at[pl.ds(0, n_rows), :], dst_buf.at[slot], sem.at[slot]).wait()


ROUTE_ROWS = 512
DISPATCH_ROWS = 256


def _route_kernel(idx_ref, dest_ref, cnt_ref, run_ref, start_ref, *, tm):
    phase = pl.program_id(0)
    i = pl.program_id(1)
    idx = idx_ref[...]
    lane = lax.broadcasted_iota(jnp.int32, idx.shape, 1)
    hot = [jnp.where(lane == idx[:, k:k + 1], 1.0, 0.0) for k in range(TOP_K)]
    cnt = hot[0] + hot[1] + hot[2] + hot[3]
    tile_total = jnp.sum(cnt, axis=0, keepdims=True)

    @pl.when((phase == 0) & (i == 0))
    def _():
        run_ref[...] = jnp.zeros_like(run_ref)

    @pl.when(phase == 0)
    def _():
        run_ref[...] += tile_total
        dest_ref[...] = jnp.zeros_like(dest_ref)

    @pl.when((phase == 1) & (i == 0))
    def _():
        counts = run_ref[...]
        padded = jnp.floor((counts + (EXPERT_ROWS - 1)) * (1.0 / EXPERT_ROWS)) * EXPERT_ROWS
        ri = lax.broadcasted_iota(jnp.int32, (ROUTER_PAD, ROUTER_PAD), 0)
        ci = lax.broadcasted_iota(jnp.int32, (ROUTER_PAD, ROUTER_PAD), 1)
        before = jnp.where(ri < ci, 1.0, 0.0).astype(BF16)
        start = _mm(_parts(jnp.broadcast_to(padded, (8, ROUTER_PAD)), 3), [before])
        start_ref[...] = start[0:1]
        cnt_ref[...] = counts
        run_ref[...] = jnp.zeros_like(run_ref)

    @pl.when(phase == 1)
    def _():
        rt = lax.broadcasted_iota(jnp.int32, (tm, tm), 0)
        ct = lax.broadcasted_iota(jnp.int32, (tm, tm), 1)
        earlier = jnp.where(ct < rt, 1.0, 0.0).astype(BF16)
        pos = start_ref[...] + run_ref[...] + _dot(earlier, cnt.astype(BF16))
        dest = jnp.zeros(idx.shape, F32)
        for k in range(TOP_K):
            d_k = jnp.sum(hot[k] * pos, axis=1, keepdims=True)
            dest = jnp.where(lane == k, d_k, dest)
            pos = pos + hot[k]
        dest_ref[...] = dest.astype(jnp.int32)
        run_ref[...] += tile_total


def _route(idx_pad):
    t = idx_pad.shape[0]
    tm = ROUTE_ROWS
    return pl.pallas_call(
        functools.partial(_route_kernel, tm=tm),
        grid=(2, t // tm),
        in_specs=[pl.BlockSpec((tm, ROUTER_PAD), lambda ph, i: (i, 0))],
        out_specs=[pl.BlockSpec((tm, ROUTER_PAD), lambda ph, i: (ph * i, 0)),
                   pl.BlockSpec((1, ROUTER_PAD), lambda ph, i: (0, 0))],
        out_shape=[jax.ShapeDtypeStruct((t, ROUTER_PAD), jnp.int32),
                   jax.ShapeDtypeStruct((1, ROUTER_PAD), F32)],
        scratch_shapes=[pltpu.VMEM((1, ROUTER_PAD), F32), pltpu.VMEM((1, ROUTER_PAD), F32)],
        compiler_params=pltpu.CompilerParams(dimension_semantics=("arbitrary", "arbitrary"),
                                             vmem_limit_bytes=VMEM_LIMIT),
    )(idx_pad)


def _dispatch_kernel(dest_ref, h_ref, init_hbm, xs_hbm, sem, *, tm):
    del init_hbm

    def issue(r, _):
        for k in range(TOP_K):
            pltpu.make_async_copy(h_ref.at[pl.ds(r, 1), :],
                                  xs_hbm.at[pl.ds(dest_ref[0, 0, TOP_K * r + k], 1), :],
                                  sem.at[0]).start()
        return 0

    lax.fori_loop(0, tm, issue, 0, unroll=2)
    for _ in range(TOP_K):
        pltpu.make_async_copy(h_ref, xs_hbm.at[pl.ds(0, tm), :], sem.at[0]).wait()


def _dispatch(dest, h, n_rows):
    t, d = h.shape
    tm = DISPATCH_ROWS
    n = TOP_K * tm
    return pl.pallas_call(
        functools.partial(_dispatch_kernel, tm=tm),
        grid=(t // tm,),
        in_specs=[pl.BlockSpec((1, 1, n), lambda i: (i, 0, 0), memory_space=pltpu.SMEM),
                  pl.BlockSpec((tm, d), lambda i: (i, 0)),
                  pl.BlockSpec(memory_space=pl.ANY)],
        out_specs=pl.BlockSpec(memory_space=pl.ANY),
        out_shape=jax.ShapeDtypeStruct((n_rows, d), F32),
        scratch_shapes=[pltpu.SemaphoreType.DMA((1,))],
        input_output_aliases={2: 0},
        compiler_params=pltpu.CompilerParams(dimension_semantics=("arbitrary",),
                                             vmem_limit_bytes=VMEM_LIMIT),
    )(dest.reshape(t // tm, 1, n), h, jnp.zeros((n_rows, d), F32))


def _expert_kernel(be_ref, nact_ref, x_ref, w1_ref, b1_ref, w2_ref, b2_ref, y_ref):
    del be_ref

    @pl.when(pl.program_id(0) < nact_ref[0])
    def _():
        f = w2_ref.shape[1]
        hid = _dot(x_ref[...].astype(BF16), w1_ref[0]) + b1_ref[0]
        x_glu = jnp.minimum(hid[:, :f], SWIGLU_LIMIT)
        x_lin = jnp.clip(hid[:, f:], -SWIGLU_LIMIT, SWIGLU_LIMIT)
        act = x_glu * _sigmoid(SWIGLU_ALPHA * x_glu) * (x_lin + 1.0)
        y_ref[...] = _dot(act.astype(BF16), w2_ref[0]) + b2_ref[0]

    @pl.when(pl.program_id(0) >= nact_ref[0])
    def _():
        y_ref[...] = jnp.zeros_like(y_ref)


def _experts(block_expert, n_active, x_sorted, w1p, b1p, w2b, b2):
    n_rows, d = x_sorted.shape
    rows = EXPERT_ROWS
    nblk = n_rows // rows
    f = w2b.shape[1]
    grid_spec = pltpu.PrefetchScalarGridSpec(
        num_scalar_prefetch=2,
        grid=(nblk,),
        in_specs=[
            pl.BlockSpec((rows, d), lambda b, be, na: (jnp.minimum(b, jnp.maximum(na[0] - 1, 0)), 0)),
            pl.BlockSpec((1, d, 2 * f), lambda b, be, na: (be[b], 0, 0)),
            pl.BlockSpec((1, 1, 2 * f), lambda b, be, na: (be[b], 0, 0)),
            pl.BlockSpec((1, f, d), lambda b, be, na: (be[b], 0, 0)),
            pl.BlockSpec((1, 1, d), lambda b, be, na: (be[b], 0, 0)),
        ],
        out_specs=pl.BlockSpec((rows, d), lambda b, be, na: (b, 0)),
    )
    return pl.pallas_call(
        _expert_kernel,
        grid_spec=grid_spec,
        out_shape=jax.ShapeDtypeStruct((n_rows, d), F32),
        compiler_params=pltpu.CompilerParams(dimension_semantics=("arbitrary",),
                                             vmem_limit_bytes=VMEM_LIMIT),
    )(block_expert, n_active, x_sorted, w1p, b1p, w2b, b2)


def _combine_kernel(dest_ref, dest_next_ref, h_ref, gate_ref, g2_ref, b2_ref, y_hbm, o_ref, ybuf, sem,
                    *, alpha, nsteps):
    tm = COMBINE_ROWS
    n = TOP_K * tm
    s = pl.program_id(0)
    slot = s % 2

    @pl.when(s == 0)
    def _():
        _gather_rows(y_hbm, ybuf, sem, dest_ref, 0, n)

    @pl.when(s + 1 < nsteps)
    def _():
        _gather_rows(y_hbm, ybuf, sem, dest_next_ref, 1 - slot, n)

    _wait_rows(y_hbm, ybuf, sem, slot, n)
    gates = gate_ref[...]
    moe = gates[:, 0:1] * ybuf[slot, 0:tm, :]
    for kk in range(1, TOP_K):
        moe = moe + gates[:, kk:kk + 1] * ybuf[slot, kk * tm:(kk + 1) * tm, :]
    o_ref[...] = _layer_norm(alpha * h_ref[...] + moe, g2_ref[...], b2_ref[...])


def _combine(dest_km, h, gate_pad, g2, b2, y_sorted, alpha):
    t, d = h.shape
    tm = COMBINE_ROWS
    nsteps = t // tm
    n = TOP_K * tm
    return pl.pallas_call(
        functools.partial(_combine_kernel, alpha=alpha, nsteps=nsteps),
        grid=(nsteps,),
        in_specs=[
            pl.BlockSpec((1, 1, n), lambda s: (s, 0, 0), memory_space=pltpu.SMEM),
            pl.BlockSpec((1, 1, n), lambda s: (jnp.minimum(s + 1, nsteps - 1), 0, 0),
                         memory_space=pltpu.SMEM),
            pl.BlockSpec((tm, d), lambda s: (s, 0)),
            pl.BlockSpec((tm, ROUTER_PAD), lambda s: (s, 0)),
            pl.BlockSpec((1, d), lambda s: (0, 0)),
            pl.BlockSpec((1, d), lambda s: (0, 0)),
            pl.BlockSpec(memory_space=pl.ANY),
        ],
        out_specs=pl.BlockSpec((tm, d), lambda s: (s, 0)),
        out_shape=jax.ShapeDtypeStruct((t, d), F32),
        scratch_shapes=[pltpu.VMEM((2, n, d), F32), pltpu.SemaphoreType.DMA((2,))],
        compiler_params=pltpu.CompilerParams(dimension_semantics=("arbitrary",),
                                             vmem_limit_bytes=VMEM_LIMIT),
    )(dest_km, dest_km, h, gate_pad, g2, b2, y_sorted)


def _block_tables(counts, nblk):
    rows = EXPERT_ROWS
    counts = counts.astype(jnp.int32)
    padded = (counts + rows - 1) // rows * rows
    pad_ends = jnp.cumsum(padded)
    block_start = jnp.arange(nblk, dtype=jnp.int32) * rows
    block_expert = jnp.sum((pad_ends[None, :] <= block_start[:, None]).astype(jnp.int32), axis=1)
    block_expert = jnp.minimum(block_expert, N_EXPERTS - 1).astype(jnp.int32)
    n_active = (pad_ends[-1] // rows).astype(jnp.int32).reshape(1)
    return block_expert, n_active


def _layer(x, w_in, mu_shift, w0, w_decay_up, a0, w_aaa_up, w_gate_up, k_k, k_a, r_k, lnx_g, lnx_b,
           w_attn_br, w_rwkv_br, w_out, ln1_g, ln1_b, w_router, b_router, w1, b1, w2, b2, ln2_g, ln2_b,
           alpha):
    b, s, d = x.shape
    t = b * s
    x2 = x.reshape(t, d)
    xb = x2.astype(BF16)
    row = lambda vec: vec.reshape(1, -1)

    off_q = RW_COLS
    off_gate = off_q + 3 * WIDTH
    pad_cols = RW_COLS_PAD - RW_COLS
    w_rw = jnp.pad(w_in[:, :RW_COLS], ((0, 0), (0, pad_cols))).astype(BF16)
    mu = jnp.pad(mu_shift, (0, pad_cols)).reshape(1, -1)
    w_qkv = w_in[:, off_q:off_gate].astype(BF16)
    w_gate = w_in[:, off_gate:].astype(BF16)

    half = HEAD_DIM // 2
    inv_freq = ROPE_THETA ** (-jnp.arange(half, dtype=F32) / half)
    ang = jnp.arange(s, dtype=F32)[:, None] * inv_freq[None, :]
    cos, sin = jnp.cos(ang), jnp.sin(ang)
    cos_t = jnp.concatenate([cos, cos, cos, cos], axis=1)
    sin_t = jnp.concatenate([-sin, sin, -sin, sin], axis=1)

    proj_rw = _project(xb, w_rw).reshape(b, s, RW_COLS_PAD)
    qt, kaug, vaug, kmean = _project_qkv(xb, w_qkv, cos_t, sin_t, b, s)
    kmean = kmean.reshape(b, s // MOBA_BLOCK, N_HEADS, HEAD_DIM).transpose(0, 2, 1, 3)

    zeros = jnp.zeros((DECAY_LORA, WIDTH), F32)
    w_lora = jnp.concatenate([jnp.concatenate([w_decay_up, zeros], axis=1),
                              jnp.concatenate([zeros, w_aaa_up], axis=1)], axis=0).astype(BF16)
    w_g = jnp.pad(w_gate_up, ((0, GATE_LORA_PAD - GATE_LORA), (0, 0))).astype(BF16)
    y_rwkv = _rwkv(proj_rw, mu, row(w0), row(a0), w_lora, w_g, row(k_k), row(k_a), row(r_k),
                   row(lnx_g), row(lnx_b))
    y_attn = _moba(qt, kaug, vaug, kmean).transpose(0, 2, 1, 3)

    w_r = jnp.pad(w_router, ((0, 0), (0, ROUTER_PAD - N_EXPERTS)))
    b_r = jnp.pad(b_router, (0, ROUTER_PAD - N_EXPERTS), constant_values=NEG_INF).reshape(1, -1)
    h, idx_pad, gate_pad = _merge(x2, y_attn.reshape(t, WIDTH), y_rwkv.reshape(t, WIDTH), w_gate,
                                  w_attn_br.astype(BF16), w_rwkv_br.astype(BF16), w_out.astype(BF16),
                                  row(ln1_g), row(ln1_b), w_r, b_r, alpha)

    dest_pad, counts = _route(idx_pad)
    dest = dest_pad[:, :TOP_K]
    nblk = t * TOP_K // EXPERT_ROWS + N_EXPERTS
    block_expert, n_active = _block_tables(counts[0, :N_EXPERTS], nblk)
    x_sorted = _dispatch(dest, h, nblk * EXPERT_ROWS)
    w1p = _deinterleave(w1)
    b1p = jnp.concatenate([b1[:, 0::2], b1[:, 1::2]], axis=1)[:, None, :]
    y_sorted = _experts(block_expert, n_active, x_sorted, w1p, b1p, w2.astype(BF16), b2[:, None, :])

    tm = COMBINE_ROWS
    dest_km = dest.reshape(t // tm, tm, TOP_K).transpose(0, 2, 1).reshape(t // tm, 1, TOP_K * tm)
    out = _combine(dest_km, h, gate_pad, row(ln2_g), row(ln2_b), y_sorted, alpha)
    return out.reshape(b, s, d)


def kernel(x, w_in, mu_shift, w0, w_decay_up, a0, w_aaa_up, w_gate_up, k_k, k_a, r_k, lnx_g, lnx_b,
           w_attn_br, w_rwkv_br, w_out, ln1_g, ln1_b, w_router, b_router, w1, b1, w2, b2, ln2_g, ln2_b):
    depth = w_in.shape[0]
    alpha = (2 * depth) ** 0.25
    for l in range(depth):
        x = _layer(x, w_in[l], mu_shift[l], w0[l], w_decay_up[l], a0[l], w_aaa_up[l], w_gate_up[l],
                   k_k[l], k_a[l], r_k[l].reshape(-1), lnx_g[l], lnx_b[l], w_attn_br[l], w_rwkv_br[l],
                   w_out[l], ln1_g[l], ln1_b[l], w_router[l], b_router[l], w1[l], b1[l], w2[l], b2[l],
                   ln2_g[l], ln2_b[l], alpha)
    return x
```

```python
import functools

import jax
import jax.numpy as jnp
from jax import lax
from jax.experimental import pallas as pl
from jax.experimental.pallas import tpu as pltpu

F32 = jnp.float32
BF16 = jnp.bfloat16
HI = lax.Precision.HIGHEST

HEAD_DIM = 64
N_HEADS = 8
WIDTH = N_HEADS * HEAD_DIM
PAIR = 2 * HEAD_DIM
N_PAIRS = N_HEADS // 2
MOBA_BLOCK = 256
MOBA_TOPK = 3
KEY_GROUP = 8
VALUE_ROWS = 2 * HEAD_DIM
ROPE_THETA = 10000.0
DECAY_LORA = 64
AAA_LORA = 64
GATE_LORA = 160
GATE_LORA_PAD = 256
RW_COLS = 3 * WIDTH + DECAY_LORA + AAA_LORA + GATE_LORA
RW_COLS_PAD = 3 * WIDTH + DECAY_LORA + AAA_LORA + GATE_LORA_PAD
GN_EPS = 64e-5
LN_EPS = 1e-5
N_EXPERTS = 32
TOP_K = 4
ROUTER_PAD = 128
SWIGLU_ALPHA = 1.702
SWIGLU_LIMIT = 7.0
RWKV_CHUNK = 64
EXPERT_ROWS = 256
COMBINE_ROWS = 128
VMEM_LIMIT = 48 * 1024 * 1024

NEG_INF = float("-inf")
LOG2_E = 1.4426950408889634
MASK_BIAS = -1e30


def _nt(a, b, precision=None):
    return lax.dot_general(a, b, (((1,), (1,)), ((), ())), precision=precision,
                           preferred_element_type=F32)


def _dot(a, b, precision=None):
    return jnp.dot(a, b, precision=precision, preferred_element_type=F32)


def _matmul_kernel(x_ref, w_ref, o_ref):
    o_ref[...] = _dot(x_ref[...], w_ref[...])


def _project(xb, w, tm=512):
    t, d = xb.shape
    n = w.shape[1]
    return pl.pallas_call(
        _matmul_kernel,
        grid=(t // tm,),
        in_specs=[pl.BlockSpec((tm, d), lambda i: (i, 0)),
                  pl.BlockSpec((d, n), lambda i: (0, 0))],
        out_specs=pl.BlockSpec((tm, n), lambda i: (i, 0)),
        out_shape=jax.ShapeDtypeStruct((t, n), F32),
        compiler_params=pltpu.CompilerParams(dimension_semantics=("parallel",),
                                             vmem_limit_bytes=VMEM_LIMIT),
    )(xb, w)


def _qkv_kernel(x_ref, w_ref, cos_ref, sin_ref, qt_ref, kaug_ref, vaug_ref, km_ref, *, tm, steps_per_seq):
    acc = _dot(x_ref[...], w_ref[...])
    cos = jnp.concatenate([cos_ref[...]] * (WIDTH // PAIR), axis=1)
    sin = jnp.concatenate([sin_ref[...]] * (WIDTH // PAIR), axis=1)
    lane = lax.broadcasted_iota(jnp.int32, (tm, WIDTH), 1)
    first_half = (lane & (HEAD_DIM // 2)) == 0

    def rope(t):
        partner = jnp.where(first_half, pltpu.roll(t, WIDTH - HEAD_DIM // 2, 1),
                            pltpu.roll(t, HEAD_DIM // 2, 1))
        return t * cos + partner * sin

    q = rope(acc[:, :WIDTH]) * (HEAD_DIM ** -0.5 * LOG2_E)
    k = rope(acc[:, WIDTH:2 * WIDTH])
    v = acc[:, 2 * WIDTH:]
    for j in range(tm // MOBA_BLOCK):
        km_ref[0, j:j + 1, :] = jnp.mean(k[j * MOBA_BLOCK:(j + 1) * MOBA_BLOCK], axis=0, keepdims=True)

    lane_p = lax.broadcasted_iota(jnp.int32, (tm, PAIR), 1)
    row_p = lax.broadcasted_iota(jnp.int32, (tm, PAIR), 0)
    first_block = (pl.program_id(0) % steps_per_seq) * (tm // MOBA_BLOCK)
    row_block = lax.shift_right_logical(row_p, MOBA_BLOCK.bit_length() - 1)
    block_tag = jnp.where(lane_p - HEAD_DIM == first_block + row_block, 1.0, 0.0)
    ones = jnp.ones((VALUE_ROWS - HEAD_DIM, tm), F32)
    for pp in range(N_PAIRS):
        sl = slice(pp * PAIR, (pp + 1) * PAIR)
        q_t = jnp.transpose(q[:, sl])
        v_t = jnp.transpose(v[:, sl])
        k_p = k[:, sl]
        k_sw = pltpu.roll(k_p, HEAD_DIM, 1)
        for h, k_h in ((0, k_p), (1, k_sw)):
            rows = slice(h * HEAD_DIM, (h + 1) * HEAD_DIM)
            qt_ref[0, 2 * pp + h] = q_t[rows].astype(BF16)
            kaug_ref[0, 2 * pp + h] = jnp.where(lane_p < HEAD_DIM, k_h, block_tag).astype(BF16)
            vaug_ref[0, 2 * pp + h] = jnp.concatenate([v_t[rows], ones], axis=0).astype(BF16)


def _project_qkv(xb, w, cos_t, sin_t, batch, seq, tm=512):
    t, d = xb.shape
    steps_per_seq = seq // tm
    assert seq // MOBA_BLOCK <= HEAD_DIM, "block one-hot tags must fit the spare lanes of a head"
    tab_spec = pl.BlockSpec((tm, PAIR), lambda i: (i % steps_per_seq, 0))
    return pl.pallas_call(
        functools.partial(_qkv_kernel, tm=tm, steps_per_seq=steps_per_seq),
        grid=(t // tm,),
        in_specs=[pl.BlockSpec((tm, d), lambda i: (i, 0)),
                  pl.BlockSpec((d, 3 * WIDTH), lambda i: (0, 0)),
                  tab_spec, tab_spec],
        out_specs=[
            pl.BlockSpec((1, N_HEADS, HEAD_DIM, tm), lambda i: (i // steps_per_seq, 0, 0, i % steps_per_seq)),
            pl.BlockSpec((1, N_HEADS, tm, PAIR), lambda i: (i // steps_per_seq, 0, i % steps_per_seq, 0)),
            pl.BlockSpec((1, N_HEADS, VALUE_ROWS, tm),
                         lambda i: (i // steps_per_seq, 0, 0, i % steps_per_seq)),
            pl.BlockSpec((1, tm // MOBA_BLOCK, WIDTH), lambda i: (i, 0, 0))],
        out_shape=[jax.ShapeDtypeStruct((batch, N_HEADS, HEAD_DIM, seq), BF16),
                   jax.ShapeDtypeStruct((batch, N_HEADS, seq, PAIR), BF16),
                   jax.ShapeDtypeStruct((batch, N_HEADS, VALUE_ROWS, seq), BF16),
                   jax.ShapeDtypeStruct((t // tm, tm // MOBA_BLOCK, WIDTH), F32)],
        compiler_params=pltpu.CompilerParams(dimension_semantics=("parallel",),
                                             vmem_limit_bytes=VMEM_LIMIT),
    )(xb, w, cos_t, sin_t)


SPLIT_PARTS = 1


def _parts(x, n=SPLIT_PARTS):
    out = []
    for _ in range(n):
        piece = x.astype(BF16)
        out.append(piece)
        x = x - piece.astype(F32)
    return out


def _mm(a_parts, b_parts, f=None):
    f = f or _dot
    order = max(len(a_parts), len(b_parts))
    acc = None
    for i, a in enumerate(a_parts):
        for j, b in enumerate(b_parts):
            if i + j < order:
                term = f(a, b)
                acc = term if acc is None else acc + term
    return acc


def _softplus(z):
    return jnp.maximum(z, 0.0) + jnp.log(1.0 + jnp.exp(-jnp.abs(z)))


def _sigmoid(z):
    return 1.0 / (1.0 + jnp.exp(-z))


def _rwkv_kernel(p_ref, mu_ref, w0_ref, a0_ref, wlora_ref, wg_ref, kk_ref, ka_ref, rk_ref,
                 lng_ref, lnb_ref, y_ref, carry_ref, state_ref):
    c = RWKV_CHUNK
    nbatch = p_ref.shape[0]

    @pl.when(pl.program_id(0) == 0)
    def _():
        carry_ref[...] = jnp.zeros_like(carry_ref)
        state_ref[...] = jnp.zeros_like(state_ref)

    ri = lax.broadcasted_iota(jnp.int32, (PAIR, PAIR), 0)
    ci = lax.broadcasted_iota(jnp.int32, (PAIR, PAIR), 1)
    head_sum = jnp.where((ri // HEAD_DIM) == (ci // HEAD_DIM), 1.0, 0.0).astype(F32)
    eye = jnp.where(ri == ci, 1.0, 0.0).astype(F32)
    strict_lower = ri > ci
    lower = ri >= ci
    rc = lax.broadcasted_iota(jnp.int32, (c, c), 0)
    cc = lax.broadcasted_iota(jnp.int32, (c, c), 1)
    cumsum_mat = jnp.where(rc >= cc, 1.0, 0.0).astype(F32)
    lane_p = lax.broadcasted_iota(jnp.int32, (c, PAIR), 1)
    head0 = lane_p < HEAD_DIM

    def stack(t):
        return jnp.concatenate([jnp.where(head0, t, 0.0), jnp.where(head0, 0.0, t)], axis=0)

    head_sum_b = [head_sum.astype(BF16)]
    cumsum_b = [cumsum_mat.astype(BF16)]

    chains = []
    for bi in range(nbatch):
        p = p_ref[bi]
        row = lax.broadcasted_iota(jnp.int32, p.shape, 0)
        prev = jnp.where(row == 0, carry_ref[bi, 0:1, :], pltpu.roll(p, 1, 0))
        carry_ref[bi] = jnp.broadcast_to(p[c - 1:c, :], carry_ref.shape[1:])
        sh = p + (prev - p) * mu_ref[...]
        r = sh[:, 0:WIDTH]
        k = sh[:, WIDTH:2 * WIDTH]
        v = sh[:, 2 * WIDTH:3 * WIDTH]
        lora = sh[:, 3 * WIDTH:3 * WIDTH + PAIR]
        hg = sh[:, 3 * WIDTH + PAIR:]
        lane_l = lax.broadcasted_iota(jnp.int32, lora.shape, 1)
        lora_act = jnp.where(lane_l < DECAY_LORA, jnp.tanh(lora), lora)
        wa = _dot(lora_act.astype(BF16), wlora_ref[...])
        w_log = -_softplus(-(w0_ref[...] + wa[:, :WIDTH])) - 0.5
        logw = -jnp.exp(w_log)
        a = _sigmoid(a0_ref[...] + wa[:, WIDTH:])
        g = _dot(_sigmoid(hg).astype(BF16), wg_ref[...])
        kkn = k * kk_ref[...]
        k2 = k * (1.0 + (a - 1.0) * ka_ref[...])
        cum_all = _mm(cumsum_b, _parts(logw, 3))
        for pp in range(N_PAIRS):
            sl = slice(pp * PAIR, (pp + 1) * PAIR)
            chains.append(dict(bi=bi, pp=pp, sl=sl, r=r[:, sl], k=k2[:, sl], v=v[:, sl], a=a[:, sl],
                               kk=kkn[:, sl], lw=logw[:, sl], cum=cum_all[:, sl], g=g[:, sl]))

    for ch in chains:
        ch['ss'] = _mm(_parts(ch['kk'] * ch['kk'], 2), head_sum_b)
    for ch in chains:
        kap = ch['kk'] / jnp.maximum(jnp.sqrt(ch['ss']), 1e-12)
        cum = ch['cum']
        ch['pc'] = jnp.exp(cum[c - 1:c, :])
        inv = jnp.exp(-cum)
        rm = stack(ch['r'] * jnp.exp(cum))
        bm = stack(kap * jnp.exp(cum - ch['lw']))
        am = stack(-(kap * ch['a']) * inv)
        km = stack(ch['k'] * inv)
        ch.update(rm=rm, bm=bm, am=am, km=km, vm=stack(ch['v']))
    for ch in chains:
        ch['sb'] = _mm(_parts(jnp.concatenate([ch['bm'], ch['rm']], axis=0)),
                       _parts(jnp.concatenate([ch['am'], ch['km']], axis=0)), _nt)
    for ch in chains:
        sb = ch['sb']
        ch['la'] = jnp.where(strict_lower, sb[:2 * c, :2 * c], 0.0)
        ch['lk'] = jnp.where(strict_lower, sb[:2 * c, 2 * c:], 0.0)
        ch['ma'] = jnp.where(lower, sb[2 * c:, :2 * c], 0.0)
        ch['mk'] = jnp.where(lower, sb[2 * c:, 2 * c:], 0.0)
        ch['tinv'] = eye + ch['la']
        ch['lpow'] = ch['la']
    n = 2
    while n < c:
        for ch in chains:
            lp = _parts(ch['lpow'])
            ch['lpow'] = _mm(lp, lp)
        for ch in chains:
            ch['tinv'] = ch['tinv'] + _mm(_parts(ch['tinv']), _parts(ch['lpow']))
        n *= 2

    for ch in chains:
        ch['h0'] = state_ref[ch['bi'], ch['pp']]
        ch['rhs'] = _mm(_parts(jnp.concatenate([ch['bm'], ch['lk']], axis=1)),
                        _parts(jnp.concatenate([ch['h0'], ch['vm']], axis=0)))
    for ch in chains:
        ch['u'] = _mm(_parts(ch['tinv']), _parts(ch['rhs']))
    for ch in chains:
        yst = _mm(_parts(jnp.concatenate([ch['rm'], ch['ma'], ch['mk']], axis=1)),
                  _parts(jnp.concatenate([ch['h0'], ch['u'], ch['vm']], axis=0)))
        ch['y'] = yst[:c] + yst[c:]
    for ch in chains:
        pc = ch['pc']
        pc_col = jnp.transpose(jnp.broadcast_to(pc, (PAIR, PAIR)))
        upd = _mm(_parts(jnp.concatenate([jnp.transpose(ch['am'] * pc), jnp.transpose(ch['km'] * pc)],
                                         axis=1)),
                  _parts(jnp.concatenate([ch['u'], ch['vm']], axis=0)))
        state_ref[ch['bi'], ch['pp']] = ch['h0'] * pc_col + upd
    for ch in chains:
        ch['mean'] = _mm(_parts(ch['y'], 2), head_sum_b) * (1.0 / HEAD_DIM)
        ch['bonus'] = _mm(_parts(ch['r'] * ch['k'] * rk_ref[:, ch['sl']], 2), head_sum_b) * ch['v']
    for ch in chains:
        yc = ch['y'] - ch['mean']
        ch['yc'] = yc
        ch['var'] = _mm(_parts(yc * yc, 2), head_sum_b) * (1.0 / HEAD_DIM)
    for ch in chains:
        sl = ch['sl']
        yn = ch['yc'] * lax.rsqrt(ch['var'] + GN_EPS) * lng_ref[:, sl] + lnb_ref[:, sl]
        y_ref[ch['bi'], :, sl] = ((yn + ch['bonus']) * ch['g']).astype(y_ref.dtype)


def _rwkv(proj_rw, mu, w0, a0, wlora, wg, k_k, k_a, r_k, lnx_g, lnx_b):
    b, s, n = proj_rw.shape
    c = RWKV_CHUNK
    vec = lambda width: pl.BlockSpec((1, width), lambda ci: (0, 0))
    return pl.pallas_call(
        _rwkv_kernel,
        grid=(s // c,),
        in_specs=[pl.BlockSpec((b, c, n), lambda ci: (0, ci, 0)),
                  vec(n), vec(WIDTH), vec(WIDTH),
                  pl.BlockSpec(wlora.shape, lambda ci: (0, 0)),
                  pl.BlockSpec(wg.shape, lambda ci: (0, 0)),
                  vec(WIDTH), vec(WIDTH), vec(WIDTH), vec(WIDTH), vec(WIDTH)],
        out_specs=pl.BlockSpec((b, c, WIDTH), lambda ci: (0, ci, 0)),
        out_shape=jax.ShapeDtypeStruct((b, s, WIDTH), BF16),
        scratch_shapes=[pltpu.VMEM((b, 8, n), F32),
                        pltpu.VMEM((b, N_PAIRS, PAIR, PAIR), F32)],
        compiler_params=pltpu.CompilerParams(dimension_semantics=("arbitrary",),
                                             vmem_limit_bytes=VMEM_LIMIT),
    )(proj_rw, mu, w0, a0, wlora, wg, k_k, k_a, r_k, lnx_g, lnx_b)


def _moba_kernel(qt_ref, k_ref, vt_ref, km_ref, o_ref, s_ref, mx_ref, *, nb):
    bs = MOBA_BLOCK
    i = pl.program_id(2)
    qt = qt_ref[0, 0]
    km = km_ref[0, 0].astype(BF16)
    blk = lax.broadcasted_iota(jnp.int32, (nb, bs), 0).astype(F32)
    gate = jnp.where(blk < i.astype(F32), _dot(km, qt), NEG_INF)
    sel = jnp.zeros((nb, bs), F32)
    for _ in range(MOBA_TOPK):
        mx = jnp.max(gate, axis=0, keepdims=True)
        hit = (gate == mx) & (mx > NEG_INF)
        idx = jnp.min(jnp.where(hit, blk, float(nb)), axis=0, keepdims=True)
        pick = blk == idx
        sel = jnp.where(pick, 1.0, sel)
        gate = jnp.where(pick, NEG_INF, gate)
    bias = jnp.where(sel > 0.0, 0.0, MASK_BIAS)
    if nb < HEAD_DIM:
        bias = jnp.concatenate([bias, jnp.zeros((HEAD_DIM - nb, bs), F32)], axis=0)
    q_sel = jnp.concatenate([qt, bias.astype(BF16)], axis=0)
    q_own = jnp.concatenate([qt, jnp.zeros((HEAD_DIM, bs), BF16)], axis=0)

    start = pl.multiple_of(i * bs, bs)
    key_i = lax.broadcasted_iota(jnp.int32, (bs, bs), 0)
    qry_i = lax.broadcasted_iota(jnp.int32, (bs, bs), 1)
    s = jnp.where(key_i <= qry_i, _dot(k_ref[0, 0, pl.ds(start, bs), :], q_own), NEG_INF)
    m = jnp.max(s, axis=0, keepdims=True)
    p = jnp.exp2(s - m)
    acc = _dot(vt_ref[0, 0, :, pl.ds(start, bs)], p.astype(BF16))

    span = min(KEY_GROUP, nb) * bs

    n_groups = lax.shift_right_logical(i * bs + span - 1, span.bit_length() - 1)

    def scores(g):
        off = pl.multiple_of(g * span, span)
        return _dot(k_ref[0, 0, pl.ds(off, span), :], q_sel)

    def col_max(sc):
        return jnp.max(jnp.max(sc.reshape(span // bs, bs, bs), axis=0), axis=0, keepdims=True)

    @pl.when(n_groups > 0)
    def _():
        s0 = scores(0)
        s_ref[0] = s0
        mx_ref[...] = col_max(s0)

    def absorb(g, m, acc):
        s = s_ref[lax.rem(g, 2)]
        m_new = jnp.maximum(m, mx_ref[...])
        off = pl.multiple_of(g * span, span)
        p = jnp.exp2(s - m_new)
        acc = jnp.exp2(m - m_new) * acc + _dot(vt_ref[0, 0, :, pl.ds(off, span)], p.astype(BF16))
        return m_new, acc

    def body(g, carry):
        s_next = scores(g + 1)
        m, acc = absorb(g, *carry)
        s_ref[1 - lax.rem(g, 2)] = s_next
        mx_ref[...] = col_max(s_next)
        return m, acc

    m, acc = lax.fori_loop(0, jnp.maximum(n_groups - 1, 0), body, (m, acc))
    m, acc = lax.cond(n_groups > 0, lambda m, acc: absorb(n_groups - 1, m, acc),
                      lambda m, acc: (m, acc), m, acc)
    out_t = acc[:HEAD_DIM] / acc[HEAD_DIM:HEAD_DIM + 1]
    o_ref[0, 0] = jnp.transpose(out_t).astype(o_ref.dtype)


def _moba(qt, kaug, vaug, kmean):
    b, nh, _, s = qt.shape
    nb = s // MOBA_BLOCK
    group = min(KEY_GROUP, nb)
    assert nb % group == 0
    return pl.pallas_call(
        functools.partial(_moba_kernel, nb=nb),
        grid=(b, nh, nb),
        in_specs=[pl.BlockSpec((1, 1, HEAD_DIM, MOBA_BLOCK), lambda bi, hi, qi: (bi, hi, 0, qi)),
                  pl.BlockSpec((1, 1, s, PAIR), lambda bi, hi, qi: (bi, hi, 0, 0)),
                  pl.BlockSpec((1, 1, VALUE_ROWS, s), lambda bi, hi, qi: (bi, hi, 0, 0)),
                  pl.BlockSpec((1, 1, nb, HEAD_DIM), lambda bi, hi, qi: (bi, hi, 0, 0))],
        out_specs=pl.BlockSpec((1, 1, MOBA_BLOCK, HEAD_DIM), lambda bi, hi, qi: (bi, hi, qi, 0)),
        out_shape=jax.ShapeDtypeStruct((b, nh, s, HEAD_DIM), BF16),
        scratch_shapes=[pltpu.VMEM((2, group * MOBA_BLOCK, MOBA_BLOCK), F32),
                        pltpu.VMEM((1, MOBA_BLOCK), F32)],
        compiler_params=pltpu.CompilerParams(
            dimension_semantics=("parallel", "parallel", "arbitrary"),
            vmem_limit_bytes=VMEM_LIMIT),
    )(qt, kaug, vaug, kmean)


def _layer_norm(z, g, b):
    mu = jnp.mean(z, axis=1, keepdims=True)
    zc = z - mu
    var = jnp.mean(zc * zc, axis=1, keepdims=True)
    return zc * lax.rsqrt(var + LN_EPS) * g + b


def _merge_kernel(x_ref, ya_ref, yr_ref, wgate_ref, wab_ref, wrb_ref, wout_ref, g1_ref, b1_ref,
                  wr_hi_ref, wr_lo_ref, br_ref, h_ref, idx_ref, gate_ref, *, alpha):
    x = x_ref[...]
    d = x.shape[1]
    gates = _sigmoid(_dot(x.astype(BF16), wgate_ref[...]))
    mixed = (gates[:, :d] * _dot(ya_ref[...], wab_ref[...])
             + gates[:, d:] * _dot(yr_ref[...], wrb_ref[...]))
    h = _layer_norm(alpha * x + _dot(mixed.astype(BF16), wout_ref[...]), g1_ref[...], b1_ref[...])
    h_ref[...] = h

    logits = _mm(_parts(h, 2), [wr_hi_ref[...], wr_lo_ref[...]]) + br_ref[...]
    col = lax.broadcasted_iota(jnp.int32, logits.shape, 1).astype(F32)
    idx_out = jnp.zeros(logits.shape, F32)
    val_out = jnp.zeros(logits.shape, F32)
    top = None
    denom = None
    for t in range(TOP_K):
        mx = jnp.max(logits, axis=1, keepdims=True)
        idx = jnp.min(jnp.where(logits == mx, col, float(ROUTER_PAD)), axis=1, keepdims=True)
        if t == 0:
            top = mx
        e = jnp.exp(mx - top)
        denom = e if t == 0 else denom + e
        idx_out = jnp.where(col == float(t), idx, idx_out)
        val_out = jnp.where(col == float(t), e, val_out)
        logits = jnp.where(col == idx, NEG_INF, logits)
    idx_ref[...] = idx_out.astype(jnp.int32)
    gate_ref[...] = val_out / denom


def _merge(x2, ya, yr, wgate, wab, wrb, wout, g1, b1, wr, br, alpha, tm=256):
    wr_hi, wr_lo = _parts(wr, 2)
    t, d = x2.shape
    row = lambda width: pl.BlockSpec((tm, width), lambda i: (i, 0))
    full = lambda arr: pl.BlockSpec(arr.shape, lambda i: (0, 0))
    return pl.pallas_call(
        functools.partial(_merge_kernel, alpha=alpha),
        grid=(t // tm,),
        in_specs=[row(d), row(WIDTH), row(WIDTH), full(wgate), full(wab), full(wrb), full(wout),
                  full(g1), full(b1), full(wr_hi), full(wr_lo), full(br)],
        out_specs=[row(d), row(ROUTER_PAD), row(ROUTER_PAD)],
        out_shape=[jax.ShapeDtypeStruct((t, d), F32),
                   jax.ShapeDtypeStruct((t, ROUTER_PAD), jnp.int32),
                   jax.ShapeDtypeStruct((t, ROUTER_PAD), F32)],
        compiler_params=pltpu.CompilerParams(dimension_semantics=("parallel",),
                                             vmem_limit_bytes=VMEM_LIMIT),
    )(x2, ya, yr, wgate, wab, wrb, wout, g1, b1, wr_hi, wr_lo, br)


DEINTERLEAVE_GROUP = 256


def _deinterleave_kernel(w_ref, perm_ref, o_ref):
    g = DEINTERLEAVE_GROUP
    n = w_ref.shape[2]
    half = n // 2
    for c in range(n // g):
        res = _dot(w_ref[0, :, c * g:(c + 1) * g].astype(BF16), perm_ref[...])
        o_ref[0, :, c * g // 2:(c + 1) * g // 2] = res[:, :g // 2].astype(BF16)
        o_ref[0, :, half + c * g // 2:half + (c + 1) * g // 2] = res[:, g // 2:].astype(BF16)


def _deinterleave(w1, tr=256):
    e, d, n = w1.shape
    g = DEINTERLEAVE_GROUP
    src = jnp.arange(g)
    dst = jnp.where(src % 2 == 0, src // 2, g // 2 + src // 2)
    perm = (dst[:, None] == jnp.arange(g)[None, :]).astype(BF16)
    return pl.pallas_call(
        _deinterleave_kernel,
        grid=(e, d // tr),
        in_specs=[pl.BlockSpec((1, tr, n), lambda ei, ri: (ei, ri, 0)),
                  pl.BlockSpec((g, g), lambda ei, ri: (0, 0))],
        out_specs=pl.BlockSpec((1, tr, n), lambda ei, ri: (ei, ri, 0)),
        out_shape=jax.ShapeDtypeStruct((e, d, n), BF16),
        compiler_params=pltpu.CompilerParams(dimension_semantics=("parallel", "parallel"),
                                             vmem_limit_bytes=VMEM_LIMIT),
    )(w1, perm)

def _row_copy(src_hbm, dst_buf, sem, src_row, slot, dst_row):
    return pltpu.make_async_copy(src_hbm.at[pl.ds(src_row, 1), :],
                                 dst_buf.at[slot, pl.ds(dst_row, 1), :],
                                 sem.at[slot])


GATHER_UNROLL = 8


def _gather_rows(src_hbm, dst_buf, sem, idx_ref, slot, n_rows):
    def issue(r, _):
        _row_copy(src_hbm, dst_buf, sem, idx_ref[0, 0, r], slot, r).start()
        return 0
    lax.fori_loop(0, n_rows, issue, 0, unroll=GATHER_UNROLL)


def _wait_rows(src_hbm, dst_buf, sem, slot, n_rows):
    pltpu.make_async_copy(src_hbm.at[pl.ds(0, n_rows), :], dst_buf.at[slot], sem.at[slot]).wait()


ROUTE_ROWS = 512
DISPATCH_ROWS = 256


def _route_kernel(idx_ref, dest_ref, cnt_ref, run_ref, start_ref, *, tm):
    phase = pl.program_id(0)
    i = pl.program_id(1)
    idx = idx_ref[...]
    lane = lax.broadcasted_iota(jnp.int32, idx.shape, 1)
    hot = [jnp.where(lane == idx[:, k:k + 1], 1.0, 0.0) for k in range(TOP_K)]
    cnt = hot[0] + hot[1] + hot[2] + hot[3]
    tile_total = jnp.sum(cnt, axis=0, keepdims=True)

    @pl.when((phase == 0) & (i == 0))
    def _():
        run_ref[...] = jnp.zeros_like(run_ref)

    @pl.when(phase == 0)
    def _():
        run_ref[...] += tile_total
        dest_ref[...] = jnp.zeros_like(dest_ref)

    @pl.when((phase == 1) & (i == 0))
    def _():
        counts = run_ref[...]
        padded = jnp.floor((counts + (EXPERT_ROWS - 1)) * (1.0 / EXPERT_ROWS)) * EXPERT_ROWS
        ri = lax.broadcasted_iota(jnp.int32, (ROUTER_PAD, ROUTER_PAD), 0)
        ci = lax.broadcasted_iota(jnp.int32, (ROUTER_PAD, ROUTER_PAD), 1)
        before = jnp.where(ri < ci, 1.0, 0.0).astype(BF16)
        start = _mm(_parts(jnp.broadcast_to(padded, (8, ROUTER_PAD)), 3), [before])
        start_ref[...] = start[0:1]
        cnt_ref[...] = counts
        run_ref[...] = jnp.zeros_like(run_ref)

    @pl.when(phase == 1)
    def _():
        rt = lax.broadcasted_iota(jnp.int32, (tm, tm), 0)
        ct = lax.broadcasted_iota(jnp.int32, (tm, tm), 1)
        earlier = jnp.where(ct < rt, 1.0, 0.0).astype(BF16)
        pos = start_ref[...] + run_ref[...] + _dot(earlier, cnt.astype(BF16))
        dest = jnp.zeros(idx.shape, F32)
        for k in range(TOP_K):
            d_k = jnp.sum(hot[k] * pos, axis=1, keepdims=True)
            dest = jnp.where(lane == k, d_k, dest)
            pos = pos + hot[k]
        dest_ref[...] = dest.astype(jnp.int32)
        run_ref[...] += tile_total


def _route(idx_pad):
    t = idx_pad.shape[0]
    tm = ROUTE_ROWS
    return pl.pallas_call(
        functools.partial(_route_kernel, tm=tm),
        grid=(2, t // tm),
        in_specs=[pl.BlockSpec((tm, ROUTER_PAD), lambda ph, i: (i, 0))],
        out_specs=[pl.BlockSpec((tm, ROUTER_PAD), lambda ph, i: (ph * i, 0)),
                   pl.BlockSpec((1, ROUTER_PAD), lambda ph, i: (0, 0))],
        out_shape=[jax.ShapeDtypeStruct((t, ROUTER_PAD), jnp.int32),
                   jax.ShapeDtypeStruct((1, ROUTER_PAD), F32)],
        scratch_shapes=[pltpu.VMEM((1, ROUTER_PAD), F32), pltpu.VMEM((1, ROUTER_PAD), F32)],
        compiler_params=pltpu.CompilerParams(dimension_semantics=("arbitrary", "arbitrary"),
                                             vmem_limit_bytes=VMEM_LIMIT),
    )(idx_pad)


def _dispatch_kernel(pad_end_ref, dest_ref, h_ref, xs_hbm, zero_ref, sem, *, tm):
    @pl.when(pl.program_id(0) == 0)
    def _():
        zero_ref[...] = jnp.zeros_like(zero_ref)

        def last_block(e):
            end = pad_end_ref[e]
            start = pl.multiple_of(jnp.maximum(end - EXPERT_ROWS, 0), EXPERT_ROWS)
            return pltpu.make_async_copy(zero_ref, xs_hbm.at[pl.ds(start, EXPERT_ROWS), :], sem.at[0])

        for e in range(N_EXPERTS):
            last_block(e).start()
        for e in range(N_EXPERTS):
            last_block(e).wait()

        def unused_block(b):
            start = pl.multiple_of(b * EXPERT_ROWS, EXPERT_ROWS)
            return pltpu.make_async_copy(zero_ref, xs_hbm.at[pl.ds(start, EXPERT_ROWS), :], sem.at[0])

        first_unused = lax.shift_right_logical(pad_end_ref[N_EXPERTS - 1], EXPERT_ROWS.bit_length() - 1)
        n_blocks = xs_hbm.shape[0] // EXPERT_ROWS

        def start_one(b, _):
            unused_block(b).start()
            return 0

        def wait_one(b, _):
            unused_block(b).wait()
            return 0

        lax.fori_loop(first_unused, n_blocks, start_one, 0)
        lax.fori_loop(first_unused, n_blocks, wait_one, 0)

    def issue(r, _):
        for k in range(TOP_K):
            pltpu.make_async_copy(h_ref.at[pl.ds(r, 1), :],
                                  xs_hbm.at[pl.ds(dest_ref[0, 0, TOP_K * r + k], 1), :],
                                  sem.at[0]).start()
        return 0

    lax.fori_loop(0, tm, issue, 0, unroll=2)
    for _ in range(TOP_K):
        pltpu.make_async_copy(h_ref, xs_hbm.at[pl.ds(0, tm), :], sem.at[0]).wait()


def _dispatch(pad_end, dest, h, n_rows):
    t, d = h.shape
    tm = DISPATCH_ROWS
    n = TOP_K * tm
    grid_spec = pltpu.PrefetchScalarGridSpec(
        num_scalar_prefetch=1,
        grid=(t // tm,),
        in_specs=[pl.BlockSpec((1, 1, n), lambda i, pe: (i, 0, 0), memory_space=pltpu.SMEM),
                  pl.BlockSpec((tm, d), lambda i, pe: (i, 0))],
        out_specs=pl.BlockSpec(memory_space=pl.ANY),
        scratch_shapes=[pltpu.VMEM((EXPERT_ROWS, d), F32), pltpu.SemaphoreType.DMA((1,))],
    )
    return pl.pallas_call(
        functools.partial(_dispatch_kernel, tm=tm),
        grid_spec=grid_spec,
        out_shape=jax.ShapeDtypeStruct((n_rows, d), F32),
        compiler_params=pltpu.CompilerParams(dimension_semantics=("arbitrary",),
                                             vmem_limit_bytes=VMEM_LIMIT),
    )(pad_end, dest.reshape(t // tm, 1, n), h)


def _expert_kernel(be_ref, nact_ref, x_ref, w1_ref, b1_ref, w2_ref, b2_ref, y_ref):
    del be_ref

    @pl.when(pl.program_id(0) < nact_ref[0])
    def _():
        f = w2_ref.shape[1]
        hid = _dot(x_ref[...].astype(BF16), w1_ref[0]) + b1_ref[0]
        x_glu = jnp.minimum(hid[:, :f], SWIGLU_LIMIT)
        x_lin = jnp.clip(hid[:, f:], -SWIGLU_LIMIT, SWIGLU_LIMIT)
        act = x_glu * _sigmoid(SWIGLU_ALPHA * x_glu) * (x_lin + 1.0)
        y_ref[...] = _dot(act.astype(BF16), w2_ref[0]) + b2_ref[0]

    @pl.when(pl.program_id(0) >= nact_ref[0])
    def _():
        y_ref[...] = jnp.zeros_like(y_ref)


def _experts(block_expert, n_active, x_sorted, w1p, b1p, w2b, b2):
    n_rows, d = x_sorted.shape
    rows = EXPERT_ROWS
    nblk = n_rows // rows
    f = w2b.shape[1]
    grid_spec = pltpu.PrefetchScalarGridSpec(
        num_scalar_prefetch=2,
        grid=(nblk,),
        in_specs=[
            pl.BlockSpec((rows, d), lambda b, be, na: (jnp.minimum(b, jnp.maximum(na[0] - 1, 0)), 0)),
            pl.BlockSpec((1, d, 2 * f), lambda b, be, na: (be[b], 0, 0)),
            pl.BlockSpec((1, 1, 2 * f), lambda b, be, na: (be[b], 0, 0)),
            pl.BlockSpec((1, f, d), lambda b, be, na: (be[b], 0, 0)),
            pl.BlockSpec((1, 1, d), lambda b, be, na: (be[b], 0, 0)),
        ],
        out_specs=pl.BlockSpec((rows, d), lambda b, be, na: (b, 0)),
    )
    return pl.pallas_call(
        _expert_kernel,
        grid_spec=grid_spec,
        out_shape=jax.ShapeDtypeStruct((n_rows, d), F32),
        compiler_params=pltpu.CompilerParams(dimension_semantics=("arbitrary",),
                                             vmem_limit_bytes=VMEM_LIMIT),
    )(block_expert, n_active, x_sorted, w1p, b1p, w2b, b2)


def _combine_kernel(dest_ref, dest_next_ref, h_ref, gate_ref, g2_ref, b2_ref, y_hbm, o_ref, ybuf, sem,
                    *, alpha, nsteps):
    tm = COMBINE_ROWS
    n = TOP_K * tm
    s = pl.program_id(0)
    slot = s % 2

    @pl.when(s == 0)
    def _():
        _gather_rows(y_hbm, ybuf, sem, dest_ref, 0, n)

    @pl.when(s + 1 < nsteps)
    def _():
        _gather_rows(y_hbm, ybuf, sem, dest_next_ref, 1 - slot, n)

    _wait_rows(y_hbm, ybuf, sem, slot, n)
    gates = gate_ref[...]
    moe = gates[:, 0:1] * ybuf[slot, 0:tm, :]
    for kk in range(1, TOP_K):
        moe = moe + gates[:, kk:kk + 1] * ybuf[slot, kk * tm:(kk + 1) * tm, :]
    o_ref[...] = _layer_norm(alpha * h_ref[...] + moe, g2_ref[...], b2_ref[...])


def _combine(dest_km, h, gate_pad, g2, b2, y_sorted, alpha):
    t, d = h.shape
    tm = COMBINE_ROWS
    nsteps = t // tm
    n = TOP_K * tm
    return pl.pallas_call(
        functools.partial(_combine_kernel, alpha=alpha, nsteps=nsteps),
        grid=(nsteps,),
        in_specs=[
            pl.BlockSpec((1, 1, n), lambda s: (s, 0, 0), memory_space=pltpu.SMEM),
            pl.BlockSpec((1, 1, n), lambda s: (jnp.minimum(s + 1, nsteps - 1), 0, 0),
                         memory_space=pltpu.SMEM),
            pl.BlockSpec((tm, d), lambda s: (s, 0)),
            pl.BlockSpec((tm, ROUTER_PAD), lambda s: (s, 0)),
            pl.BlockSpec((1, d), lambda s: (0, 0)),
            pl.BlockSpec((1, d), lambda s: (0, 0)),
            pl.BlockSpec(memory_space=pl.ANY),
        ],
        out_specs=pl.BlockSpec((tm, d), lambda s: (s, 0)),
        out_shape=jax.ShapeDtypeStruct((t, d), F32),
        scratch_shapes=[pltpu.VMEM((2, n, d), F32), pltpu.SemaphoreType.DMA((2,))],
        compiler_params=pltpu.CompilerParams(dimension_semantics=("arbitrary",),
                                             vmem_limit_bytes=VMEM_LIMIT),
    )(dest_km, dest_km, h, gate_pad, g2, b2, y_sorted)


def _block_tables(counts, nblk):
    rows = EXPERT_ROWS
    counts = counts.astype(jnp.int32)
    padded = (counts + rows - 1) // rows * rows
    pad_ends = jnp.cumsum(padded)
    block_start = jnp.arange(nblk, dtype=jnp.int32) * rows
    block_expert = jnp.sum((pad_ends[None, :] <= block_start[:, None]).astype(jnp.int32), axis=1)
    block_expert = jnp.minimum(block_expert, N_EXPERTS - 1).astype(jnp.int32)
    n_active = (pad_ends[-1] // rows).astype(jnp.int32).reshape(1)
    return block_expert, n_active, pad_ends.astype(jnp.int32)


def _layer(x, w_in, mu_shift, w0, w_decay_up, a0, w_aaa_up, w_gate_up, k_k, k_a, r_k, lnx_g, lnx_b,
           w_attn_br, w_rwkv_br, w_out, ln1_g, ln1_b, w_router, b_router, w1, b1, w2, b2, ln2_g, ln2_b,
           alpha):
    b, s, d = x.shape
    t = b * s
    x2 = x.reshape(t, d)
    xb = x2.astype(BF16)
    row = lambda vec: vec.reshape(1, -1)

    off_q = RW_COLS
    off_gate = off_q + 3 * WIDTH
    pad_cols = RW_COLS_PAD - RW_COLS
    w_rw = jnp.pad(w_in[:, :RW_COLS], ((0, 0), (0, pad_cols))).astype(BF16)
    mu = jnp.pad(mu_shift, (0, pad_cols)).reshape(1, -1)
    w_qkv = w_in[:, off_q:off_gate].astype(BF16)
    w_gate = w_in[:, off_gate:].astype(BF16)

    half = HEAD_DIM // 2
    inv_freq = ROPE_THETA ** (-jnp.arange(half, dtype=F32) / half)
    ang = jnp.arange(s, dtype=F32)[:, None] * inv_freq[None, :]
    cos, sin = jnp.cos(ang), jnp.sin(ang)
    cos_t = jnp.concatenate([cos, cos, cos, cos], axis=1)
    sin_t = jnp.concatenate([-sin, sin, -sin, sin], axis=1)

    proj_rw = _project(xb, w_rw).reshape(b, s, RW_COLS_PAD)
    qt, kaug, vaug, kmean = _project_qkv(xb, w_qkv, cos_t, sin_t, b, s)
    kmean = kmean.reshape(b, s // MOBA_BLOCK, N_HEADS, HEAD_DIM).transpose(0, 2, 1, 3)

    zeros = jnp.zeros((DECAY_LORA, WIDTH), F32)
    w_lora = jnp.concatenate([jnp.concatenate([w_decay_up, zeros], axis=1),
                              jnp.concatenate([zeros, w_aaa_up], axis=1)], axis=0).astype(BF16)
    w_g = jnp.pad(w_gate_up, ((0, GATE_LORA_PAD - GATE_LORA), (0, 0))).astype(BF16)
    y_rwkv = _rwkv(proj_rw, mu, row(w0), row(a0), w_lora, w_g, row(k_k), row(k_a), row(r_k),
                   row(lnx_g), row(lnx_b))
    y_attn = _moba(qt, kaug, vaug, kmean).transpose(0, 2, 1, 3)

    w_r = jnp.pad(w_router, ((0, 0), (0, ROUTER_PAD - N_EXPERTS)))
    b_r = jnp.pad(b_router, (0, ROUTER_PAD - N_EXPERTS), constant_values=NEG_INF).reshape(1, -1)
    h, idx_pad, gate_pad = _merge(x2, y_attn.reshape(t, WIDTH), y_rwkv.reshape(t, WIDTH), w_gate,
                                  w_attn_br.astype(BF16), w_rwkv_br.astype(BF16), w_out.astype(BF16),
                                  row(ln1_g), row(ln1_b), w_r, b_r, alpha)

    dest_pad, counts = _route(idx_pad)
    dest = dest_pad[:, :TOP_K]
    nblk = t * TOP_K // EXPERT_ROWS + N_EXPERTS
    block_expert, n_active, pad_end = _block_tables(counts[0, :N_EXPERTS], nblk)
    x_sorted = _dispatch(pad_end, dest, h, nblk * EXPERT_ROWS)
    w1p = _deinterleave(w1)
    b1p = jnp.concatenate([b1[:, 0::2], b1[:, 1::2]], axis=1)[:, None, :]
    y_sorted = _experts(block_expert, n_active, x_sorted, w1p, b1p, w2.astype(BF16), b2[:, None, :])

    tm = COMBINE_ROWS
    dest_km = dest.reshape(t // tm, tm, TOP_K).transpose(0, 2, 1).reshape(t // tm, 1, TOP_K * tm)
    out = _combine(dest_km, h, gate_pad, row(ln2_g), row(ln2_b), y_sorted, alpha)
    return out.reshape(b, s, d)


def kernel(x, w_in, mu_shift, w0, w_decay_up, a0, w_aaa_up, w_gate_up, k_k, k_a, r_k, lnx_g, lnx_b,
           w_attn_br, w_rwkv_br, w_out, ln1_g, ln1_b, w_router, b_router, w1, b1, w2, b2, ln2_g, ln2_b):
    depth = w_in.shape[0]
    alpha = (2 * depth) ** 0.25
    for l in range(depth):
        x = _layer(x, w_in[l], mu_shift[l], w0[l], w_decay_up[l], a0[l], w_aaa_up[l], w_gate_up[l],
                   k_k[l], k_a[l], r_k[l].reshape(-1), lnx_g[l], lnx_b[l], w_attn_br[l], w_rwkv_br[l],
                   w_out[l], ln1_g[l], ln1_b[l], w_router[l], b_router[l], w1[l], b1[l], w2[l], b2[l],
                   ln2_g[l], ln2_b[l], alpha)
    return x
```

```python
import functools

import jax
import jax.numpy as jnp
from jax import lax
from jax.experimental import pallas as pl
from jax.experimental.pallas import tpu as pltpu

F32 = jnp.float32
BF16 = jnp.bfloat16
HI = lax.Precision.HIGHEST

HEAD_DIM = 64
N_HEADS = 8
WIDTH = N_HEADS * HEAD_DIM
PAIR = 2 * HEAD_DIM
N_PAIRS = N_HEADS // 2
MOBA_BLOCK = 256
MOBA_TOPK = 3
KEY_GROUP = 8
VALUE_ROWS = 2 * HEAD_DIM
ROPE_THETA = 10000.0
DECAY_LORA = 64
AAA_LORA = 64
GATE_LORA = 160
GATE_LORA_PAD = 256
RW_COLS = 3 * WIDTH + DECAY_LORA + AAA_LORA + GATE_LORA
RW_COLS_PAD = 3 * WIDTH + DECAY_LORA + AAA_LORA + GATE_LORA_PAD
GN_EPS = 64e-5
LN_EPS = 1e-5
N_EXPERTS = 32
TOP_K = 4
ROUTER_PAD = 128
SWIGLU_ALPHA = 1.702
SWIGLU_LIMIT = 7.0
RWKV_CHUNK = 64
EXPERT_ROWS = 256
COMBINE_ROWS = 128
VMEM_LIMIT = 48 * 1024 * 1024

NEG_INF = float("-inf")
LOG2_E = 1.4426950408889634
MASK_BIAS = -1e30


def _nt(a, b, precision=None):
    return lax.dot_general(a, b, (((1,), (1,)), ((), ())), precision=precision,
                           preferred_element_type=F32)


def _dot(a, b, precision=None):
    return jnp.dot(a, b, precision=precision, preferred_element_type=F32)


def _matmul_kernel(x_ref, w_ref, o_ref):
    o_ref[...] = _dot(x_ref[...], w_ref[...])


def _project(xb, w, tm=512):
    t, d = xb.shape
    n = w.shape[1]
    return pl.pallas_call(
        _matmul_kernel,
        grid=(t // tm,),
        in_specs=[pl.BlockSpec((tm, d), lambda i: (i, 0)),
                  pl.BlockSpec((d, n), lambda i: (0, 0))],
        out_specs=pl.BlockSpec((tm, n), lambda i: (i, 0)),
        out_shape=jax.ShapeDtypeStruct((t, n), F32),
        compiler_params=pltpu.CompilerParams(dimension_semantics=("parallel",),
                                             vmem_limit_bytes=VMEM_LIMIT),
    )(xb, w)


def _qkv_kernel(x_ref, w_ref, cos_ref, sin_ref, qt_ref, kaug_ref, vaug_ref, km_ref, *, tm, steps_per_seq):
    acc = _dot(x_ref[...], w_ref[...])
    cos = jnp.concatenate([cos_ref[...]] * (WIDTH // PAIR), axis=1)
    sin = jnp.concatenate([sin_ref[...]] * (WIDTH // PAIR), axis=1)
    lane = lax.broadcasted_iota(jnp.int32, (tm, WIDTH), 1)
    first_half = (lane & (HEAD_DIM // 2)) == 0

    def rope(t):
        partner = jnp.where(first_half, pltpu.roll(t, WIDTH - HEAD_DIM // 2, 1),
                            pltpu.roll(t, HEAD_DIM // 2, 1))
        return t * cos + partner * sin

    q = rope(acc[:, :WIDTH]) * (HEAD_DIM ** -0.5 * LOG2_E)
    k = rope(acc[:, WIDTH:2 * WIDTH])
    v = acc[:, 2 * WIDTH:]
    for j in range(tm // MOBA_BLOCK):
        km_ref[0, j:j + 1, :] = jnp.mean(k[j * MOBA_BLOCK:(j + 1) * MOBA_BLOCK], axis=0, keepdims=True)

    lane_p = lax.broadcasted_iota(jnp.int32, (tm, PAIR), 1)
    row_p = lax.broadcasted_iota(jnp.int32, (tm, PAIR), 0)
    first_block = (pl.program_id(0) % steps_per_seq) * (tm // MOBA_BLOCK)
    row_block = lax.shift_right_logical(row_p, MOBA_BLOCK.bit_length() - 1)
    block_tag = jnp.where(lane_p - HEAD_DIM == first_block + row_block, 1.0, 0.0)
    ones = jnp.ones((VALUE_ROWS - HEAD_DIM, tm), F32)
    for pp in range(N_PAIRS):
        sl = slice(pp * PAIR, (pp + 1) * PAIR)
        q_t = jnp.transpose(q[:, sl])
        v_t = jnp.transpose(v[:, sl])
        k_p = k[:, sl]
        k_sw = pltpu.roll(k_p, HEAD_DIM, 1)
        for h, k_h in ((0, k_p), (1, k_sw)):
            rows = slice(h * HEAD_DIM, (h + 1) * HEAD_DIM)
            qt_ref[0, 2 * pp + h] = q_t[rows].astype(BF16)
            kaug_ref[0, 2 * pp + h] = jnp.where(lane_p < HEAD_DIM, k_h, block_tag).astype(BF16)
            vaug_ref[0, 2 * pp + h] = jnp.concatenate([v_t[rows], ones], axis=0).astype(BF16)


def _project_qkv(xb, w, cos_t, sin_t, batch, seq, tm=512):
    t, d = xb.shape
    steps_per_seq = seq // tm
    assert seq // MOBA_BLOCK <= HEAD_DIM, "block one-hot tags must fit the spare lanes of a head"
    tab_spec = pl.BlockSpec((tm, PAIR), lambda i: (i % steps_per_seq, 0))
    return pl.pallas_call(
        functools.partial(_qkv_kernel, tm=tm, steps_per_seq=steps_per_seq),
        grid=(t // tm,),
        in_specs=[pl.BlockSpec((tm, d), lambda i: (i, 0)),
                  pl.BlockSpec((d, 3 * WIDTH), lambda i: (0, 0)),
                  tab_spec, tab_spec],
        out_specs=[
            pl.BlockSpec((1, N_HEADS, HEAD_DIM, tm), lambda i: (i // steps_per_seq, 0, 0, i % steps_per_seq)),
            pl.BlockSpec((1, N_HEADS, tm, PAIR), lambda i: (i // steps_per_seq, 0, i % steps_per_seq, 0)),
            pl.BlockSpec((1, N_HEADS, VALUE_ROWS, tm),
                         lambda i: (i // steps_per_seq, 0, 0, i % steps_per_seq)),
            pl.BlockSpec((1, tm // MOBA_BLOCK, WIDTH), lambda i: (i, 0, 0))],
        out_shape=[jax.ShapeDtypeStruct((batch, N_HEADS, HEAD_DIM, seq), BF16),
                   jax.ShapeDtypeStruct((batch, N_HEADS, seq, PAIR), BF16),
                   jax.ShapeDtypeStruct((batch, N_HEADS, VALUE_ROWS, seq), BF16),
                   jax.ShapeDtypeStruct((t // tm, tm // MOBA_BLOCK, WIDTH), F32)],
        compiler_params=pltpu.CompilerParams(dimension_semantics=("parallel",),
                                             vmem_limit_bytes=VMEM_LIMIT),
    )(xb, w, cos_t, sin_t)


SPLIT_PARTS = 1


def _parts(x, n=SPLIT_PARTS):
    out = []
    for _ in range(n):
        piece = x.astype(BF16)
        out.append(piece)
        x = x - piece.astype(F32)
    return out


def _mm(a_parts, b_parts, f=None):
    f = f or _dot
    order = max(len(a_parts), len(b_parts))
    acc = None
    for i, a in enumerate(a_parts):
        for j, b in enumerate(b_parts):
            if i + j < order:
                term = f(a, b)
                acc = term if acc is None else acc + term
    return acc


def _softplus(z):
    return jnp.maximum(z, 0.0) + jnp.log(1.0 + jnp.exp(-jnp.abs(z)))


def _sigmoid(z):
    return 1.0 / (1.0 + jnp.exp(-z))


def _rwkv_kernel(p_ref, mu_ref, w0_ref, a0_ref, wlora_ref, wg_ref, kk_ref, ka_ref, rk_ref,
                 lng_ref, lnb_ref, y_ref, carry_ref, state_ref):
    c = RWKV_CHUNK
    nbatch = p_ref.shape[0]

    @pl.when(pl.program_id(0) == 0)
    def _():
        carry_ref[...] = jnp.zeros_like(carry_ref)
        state_ref[...] = jnp.zeros_like(state_ref)

    ri = lax.broadcasted_iota(jnp.int32, (PAIR, PAIR), 0)
    ci = lax.broadcasted_iota(jnp.int32, (PAIR, PAIR), 1)
    head_sum = jnp.where((ri // HEAD_DIM) == (ci // HEAD_DIM), 1.0, 0.0).astype(F32)
    eye = jnp.where(ri == ci, 1.0, 0.0).astype(F32)
    strict_lower = ri > ci
    lower = ri >= ci
    rc = lax.broadcasted_iota(jnp.int32, (c, c), 0)
    cc = lax.broadcasted_iota(jnp.int32, (c, c), 1)
    cumsum_mat = jnp.where(rc >= cc, 1.0, 0.0).astype(F32)
    lane_p = lax.broadcasted_iota(jnp.int32, (c, PAIR), 1)
    head0 = lane_p < HEAD_DIM

    def stack(t):
        return jnp.concatenate([jnp.where(head0, t, 0.0), jnp.where(head0, 0.0, t)], axis=0)

    head_sum_b = [head_sum.astype(BF16)]
    cumsum_b = [cumsum_mat.astype(BF16)]

    chains = []
    for bi in range(nbatch):
        p = p_ref[bi]
        row = lax.broadcasted_iota(jnp.int32, p.shape, 0)
        prev = jnp.where(row == 0, carry_ref[bi, 0:1, :], pltpu.roll(p, 1, 0))
        carry_ref[bi] = jnp.broadcast_to(p[c - 1:c, :], carry_ref.shape[1:])
        sh = p + (prev - p) * mu_ref[...]
        r = sh[:, 0:WIDTH]
        k = sh[:, WIDTH:2 * WIDTH]
        v = sh[:, 2 * WIDTH:3 * WIDTH]
        lora = sh[:, 3 * WIDTH:3 * WIDTH + PAIR]
        hg = sh[:, 3 * WIDTH + PAIR:]
        lane_l = lax.broadcasted_iota(jnp.int32, lora.shape, 1)
        lora_act = jnp.where(lane_l < DECAY_LORA, jnp.tanh(lora), lora)
        wa = _dot(lora_act.astype(BF16), wlora_ref[...])
        w_log = -_softplus(-(w0_ref[...] + wa[:, :WIDTH])) - 0.5
        logw = -jnp.exp(w_log)
        a = _sigmoid(a0_ref[...] + wa[:, WIDTH:])
        g = _dot(_sigmoid(hg).astype(BF16), wg_ref[...])
        kkn = k * kk_ref[...]
        k2 = k * (1.0 + (a - 1.0) * ka_ref[...])
        cum_all = _mm(cumsum_b, _parts(logw, 3))
        for pp in range(N_PAIRS):
            sl = slice(pp * PAIR, (pp + 1) * PAIR)
            chains.append(dict(bi=bi, pp=pp, sl=sl, r=r[:, sl], k=k2[:, sl], v=v[:, sl], a=a[:, sl],
                               kk=kkn[:, sl], lw=logw[:, sl], cum=cum_all[:, sl], g=g[:, sl]))

    for ch in chains:
        ch['ss'] = _mm(_parts(ch['kk'] * ch['kk'], 2), head_sum_b)
    for ch in chains:
        kap = ch['kk'] / jnp.maximum(jnp.sqrt(ch['ss']), 1e-12)
        cum = ch['cum']
        ch['pc'] = jnp.exp(cum[c - 1:c, :])
        inv = jnp.exp(-cum)
        rm = stack(ch['r'] * jnp.exp(cum))
        bm = stack(kap * jnp.exp(cum - ch['lw']))
        am = stack(-(kap * ch['a']) * inv)
        km = stack(ch['k'] * inv)
        ch.update(rm=rm, bm=bm, am=am, km=km, vm=stack(ch['v']))
    for ch in chains:
        ch['sb'] = _mm(_parts(jnp.concatenate([ch['bm'], ch['rm']], axis=0)),
                       _parts(jnp.concatenate([ch['am'], ch['km']], axis=0)), _nt)
    for ch in chains:
        sb = ch['sb']
        ch['la'] = jnp.where(strict_lower, sb[:2 * c, :2 * c], 0.0)
        ch['lk'] = jnp.where(strict_lower, sb[:2 * c, 2 * c:], 0.0)
        ch['ma'] = jnp.where(lower, sb[2 * c:, :2 * c], 0.0)
        ch['mk'] = jnp.where(lower, sb[2 * c:, 2 * c:], 0.0)
        ch['tinv'] = eye + ch['la']
        ch['lpow'] = ch['la']
    n = 2
    while n < c:
        for ch in chains:
            lp = _parts(ch['lpow'])
            ch['lpow'] = _mm(lp, lp)
        for ch in chains:
            ch['tinv'] = ch['tinv'] + _mm(_parts(ch['tinv']), _parts(ch['lpow']))
        n *= 2

    for ch in chains:
        ch['h0'] = state_ref[ch['bi'], ch['pp']]
        ch['rhs'] = _mm(_parts(jnp.concatenate([ch['bm'], ch['lk']], axis=1)),
                        _parts(jnp.concatenate([ch['h0'], ch['vm']], axis=0)))
    for ch in chains:
        ch['u'] = _mm(_parts(ch['tinv']), _parts(ch['rhs']))
    for ch in chains:
        yst = _mm(_parts(jnp.concatenate([ch['rm'], ch['ma'], ch['mk']], axis=1)),
                  _parts(jnp.concatenate([ch['h0'], ch['u'], ch['vm']], axis=0)))
        ch['y'] = yst[:c] + yst[c:]
    for ch in chains:
        pc = ch['pc']
        pc_col = jnp.transpose(jnp.broadcast_to(pc, (PAIR, PAIR)))
        upd = _mm(_parts(jnp.concatenate([jnp.transpose(ch['am'] * pc), jnp.transpose(ch['km'] * pc)],
                                         axis=1)),
                  _parts(jnp.concatenate([ch['u'], ch['vm']], axis=0)))
        state_ref[ch['bi'], ch['pp']] = ch['h0'] * pc_col + upd
    for ch in chains:
        ch['mean'] = _mm(_parts(ch['y'], 2), head_sum_b) * (1.0 / HEAD_DIM)
        ch['bonus'] = _mm(_parts(ch['r'] * ch['k'] * rk_ref[:, ch['sl']], 2), head_sum_b) * ch['v']
    for ch in chains:
        yc = ch['y'] - ch['mean']
        ch['yc'] = yc
        ch['var'] = _mm(_parts(yc * yc, 2), head_sum_b) * (1.0 / HEAD_DIM)
    for ch in chains:
        sl = ch['sl']
        yn = ch['yc'] * lax.rsqrt(ch['var'] + GN_EPS) * lng_ref[:, sl] + lnb_ref[:, sl]
        y_ref[ch['bi'], :, sl] = ((yn + ch['bonus']) * ch['g']).astype(y_ref.dtype)


def _rwkv(proj_rw, mu, w0, a0, wlora, wg, k_k, k_a, r_k, lnx_g, lnx_b):
    b, s, n = proj_rw.shape
    c = RWKV_CHUNK
    vec = lambda width: pl.BlockSpec((1, width), lambda ci: (0, 0))
    return pl.pallas_call(
        _rwkv_kernel,
        grid=(s // c,),
        in_specs=[pl.BlockSpec((b, c, n), lambda ci: (0, ci, 0)),
                  vec(n), vec(WIDTH), vec(WIDTH),
                  pl.BlockSpec(wlora.shape, lambda ci: (0, 0)),
                  pl.BlockSpec(wg.shape, lambda ci: (0, 0)),
                  vec(WIDTH), vec(WIDTH), vec(WIDTH), vec(WIDTH), vec(WIDTH)],
        out_specs=pl.BlockSpec((b, c, WIDTH), lambda ci: (0, ci, 0)),
        out_shape=jax.ShapeDtypeStruct((b, s, WIDTH), BF16),
        scratch_shapes=[pltpu.VMEM((b, 8, n), F32),
                        pltpu.VMEM((b, N_PAIRS, PAIR, PAIR), F32)],
        compiler_params=pltpu.CompilerParams(dimension_semantics=("arbitrary",),
                                             vmem_limit_bytes=VMEM_LIMIT),
    )(proj_rw, mu, w0, a0, wlora, wg, k_k, k_a, r_k, lnx_g, lnx_b)


def _moba_kernel(qt_ref, k_ref, vt_ref, km_ref, o_ref, s_ref, mx_ref, *, nb):
    bs = MOBA_BLOCK
    i = pl.program_id(2)
    qt = qt_ref[0, 0]
    km = km_ref[0, 0].astype(BF16)
    blk = lax.broadcasted_iota(jnp.int32, (nb, bs), 0).astype(F32)
    gate = jnp.where(blk < i.astype(F32), _dot(km, qt), NEG_INF)
    sel = jnp.zeros((nb, bs), F32)
    for _ in range(MOBA_TOPK):
        mx = jnp.max(gate, axis=0, keepdims=True)
        hit = (gate == mx) & (mx > NEG_INF)
        idx = jnp.min(jnp.where(hit, blk, float(nb)), axis=0, keepdims=True)
        pick = blk == idx
        sel = jnp.where(pick, 1.0, sel)
        gate = jnp.where(pick, NEG_INF, gate)
    bias = jnp.where(sel > 0.0, 0.0, MASK_BIAS)
    if nb < HEAD_DIM:
        bias = jnp.concatenate([bias, jnp.zeros((HEAD_DIM - nb, bs), F32)], axis=0)
    q_sel = jnp.concatenate([qt, bias.astype(BF16)], axis=0)
    q_own = jnp.concatenate([qt, jnp.zeros((HEAD_DIM, bs), BF16)], axis=0)

    start = pl.multiple_of(i * bs, bs)
    key_i = lax.broadcasted_iota(jnp.int32, (bs, bs), 0)
    qry_i = lax.broadcasted_iota(jnp.int32, (bs, bs), 1)
    s = jnp.where(key_i <= qry_i, _dot(k_ref[0, 0, pl.ds(start, bs), :], q_own), NEG_INF)
    m = jnp.max(s, axis=0, keepdims=True)
    p = jnp.exp2(s - m)
    acc = _dot(vt_ref[0, 0, :, pl.ds(start, bs)], p.astype(BF16))

    span = min(KEY_GROUP, nb) * bs

    n_groups = lax.shift_right_logical(i * bs + span - 1, span.bit_length() - 1)

    def scores(g):
        off = pl.multiple_of(g * span, span)
        return _dot(k_ref[0, 0, pl.ds(off, span), :], q_sel)

    def col_max(sc):
        return jnp.max(jnp.max(sc.reshape(span // bs, bs, bs), axis=0), axis=0, keepdims=True)

    @pl.when(n_groups > 0)
    def _():
        s0 = scores(0)
        s_ref[0] = s0
        mx_ref[...] = col_max(s0)

    def absorb(g, m, acc):
        s = s_ref[lax.rem(g, 2)]
        m_new = jnp.maximum(m, mx_ref[...])
        off = pl.multiple_of(g * span, span)
        p = jnp.exp2(s - m_new)
        acc = jnp.exp2(m - m_new) * acc + _dot(vt_ref[0, 0, :, pl.ds(off, span)], p.astype(BF16))
        return m_new, acc

    def body(g, carry):
        s_next = scores(g + 1)
        m, acc = absorb(g, *carry)
        s_ref[1 - lax.rem(g, 2)] = s_next
        mx_ref[...] = col_max(s_next)
        return m, acc

    m, acc = lax.fori_loop(0, jnp.maximum(n_groups - 1, 0), body, (m, acc))
    m, acc = lax.cond(n_groups > 0, lambda m, acc: absorb(n_groups - 1, m, acc),
                      lambda m, acc: (m, acc), m, acc)
    out_t = acc[:HEAD_DIM] / acc[HEAD_DIM:HEAD_DIM + 1]
    o_ref[0, 0] = jnp.transpose(out_t).astype(o_ref.dtype)


def _moba(qt, kaug, vaug, kmean):
    b, nh, _, s = qt.shape
    nb = s // MOBA_BLOCK
    group = min(KEY_GROUP, nb)
    assert nb % group == 0
    return pl.pallas_call(
        functools.partial(_moba_kernel, nb=nb),
        grid=(b, nh, nb),
        in_specs=[pl.BlockSpec((1, 1, HEAD_DIM, MOBA_BLOCK), lambda bi, hi, qi: (bi, hi, 0, qi)),
                  pl.BlockSpec((1, 1, s, PAIR), lambda bi, hi, qi: (bi, hi, 0, 0)),
                  pl.BlockSpec((1, 1, VALUE_ROWS, s), lambda bi, hi, qi: (bi, hi, 0, 0)),
                  pl.BlockSpec((1, 1, nb, HEAD_DIM), lambda bi, hi, qi: (bi, hi, 0, 0))],
        out_specs=pl.BlockSpec((1, 1, MOBA_BLOCK, HEAD_DIM), lambda bi, hi, qi: (bi, hi, qi, 0)),
        out_shape=jax.ShapeDtypeStruct((b, nh, s, HEAD_DIM), BF16),
        scratch_shapes=[pltpu.VMEM((2, group * MOBA_BLOCK, MOBA_BLOCK), F32),
                        pltpu.VMEM((1, MOBA_BLOCK), F32)],
        compiler_params=pltpu.CompilerParams(
            dimension_semantics=("parallel", "parallel", "arbitrary"),
            vmem_limit_bytes=VMEM_LIMIT),
    )(qt, kaug, vaug, kmean)


def _layer_norm(z, g, b):
    mu = jnp.mean(z, axis=1, keepdims=True)
    zc = z - mu
    var = jnp.mean(zc * zc, axis=1, keepdims=True)
    return zc * lax.rsqrt(var + LN_EPS) * g + b


def _merge_kernel(x_ref, ya_ref, yr_ref, wgate_ref, wab_ref, wrb_ref, wout_ref, g1_ref, b1_ref,
                  wr_hi_ref, wr_lo_ref, br_ref, h_ref, idx_ref, gate_ref, *, alpha):
    x = x_ref[...]
    d = x.shape[1]
    gates = _sigmoid(_dot(x.astype(BF16), wgate_ref[...]))
    mixed = (gates[:, :d] * _dot(ya_ref[...], wab_ref[...])
             + gates[:, d:] * _dot(yr_ref[...], wrb_ref[...]))
    h = _layer_norm(alpha * x + _dot(mixed.astype(BF16), wout_ref[...]), g1_ref[...], b1_ref[...])
    h_ref[...] = h

    logits = _mm(_parts(h, 2), [wr_hi_ref[...], wr_lo_ref[...]]) + br_ref[...]
    col = lax.broadcasted_iota(jnp.int32, logits.shape, 1).astype(F32)
    idx_out = jnp.zeros(logits.shape, F32)
    val_out = jnp.zeros(logits.shape, F32)
    top = None
    denom = None
    for t in range(TOP_K):
        mx = jnp.max(logits, axis=1, keepdims=True)
        idx = jnp.min(jnp.where(logits == mx, col, float(ROUTER_PAD)), axis=1, keepdims=True)
        if t == 0:
            top = mx
        e = jnp.exp(mx - top)
        denom = e if t == 0 else denom + e
        idx_out = jnp.where(col == float(t), idx, idx_out)
        val_out = jnp.where(col == float(t), e, val_out)
        logits = jnp.where(col == idx, NEG_INF, logits)
    idx_ref[...] = idx_out.astype(jnp.int32)
    gate_ref[...] = val_out / denom


def _merge(x2, ya, yr, wgate, wab, wrb, wout, g1, b1, wr, br, alpha, tm=256):
    wr_hi, wr_lo = _parts(wr, 2)
    t, d = x2.shape
    row = lambda width: pl.BlockSpec((tm, width), lambda i: (i, 0))
    full = lambda arr: pl.BlockSpec(arr.shape, lambda i: (0, 0))
    return pl.pallas_call(
        functools.partial(_merge_kernel, alpha=alpha),
        grid=(t // tm,),
        in_specs=[row(d), row(WIDTH), row(WIDTH), full(wgate), full(wab), full(wrb), full(wout),
                  full(g1), full(b1), full(wr_hi), full(wr_lo), full(br)],
        out_specs=[row(d), row(ROUTER_PAD), row(ROUTER_PAD)],
        out_shape=[jax.ShapeDtypeStruct((t, d), F32),
                   jax.ShapeDtypeStruct((t, ROUTER_PAD), jnp.int32),
                   jax.ShapeDtypeStruct((t, ROUTER_PAD), F32)],
        compiler_params=pltpu.CompilerParams(dimension_semantics=("parallel",),
                                             vmem_limit_bytes=VMEM_LIMIT),
    )(x2, ya, yr, wgate, wab, wrb, wout, g1, b1, wr_hi, wr_lo, br)


DEINTERLEAVE_GROUP = 256


def _deinterleave_kernel(w_ref, perm_ref, o_ref):
    g = DEINTERLEAVE_GROUP
    n = w_ref.shape[2]
    half = n // 2
    for c in range(n // g):
        res = _dot(w_ref[0, :, c * g:(c + 1) * g].astype(BF16), perm_ref[...])
        o_ref[0, :, c * g // 2:(c + 1) * g // 2] = res[:, :g // 2].astype(BF16)
        o_ref[0, :, half + c * g // 2:half + (c + 1) * g // 2] = res[:, g // 2:].astype(BF16)


def _deinterleave(w1, tr=256):
    e, d, n = w1.shape
    g = DEINTERLEAVE_GROUP
    src = jnp.arange(g)
    dst = jnp.where(src % 2 == 0, src // 2, g // 2 + src // 2)
    perm = (dst[:, None] == jnp.arange(g)[None, :]).astype(BF16)
    return pl.pallas_call(
        _deinterleave_kernel,
        grid=(e, d // tr),
        in_specs=[pl.BlockSpec((1, tr, n), lambda ei, ri: (ei, ri, 0)),
                  pl.BlockSpec((g, g), lambda ei, ri: (0, 0))],
        out_specs=pl.BlockSpec((1, tr, n), lambda ei, ri: (ei, ri, 0)),
        out_shape=jax.ShapeDtypeStruct((e, d, n), BF16),
        compiler_params=pltpu.CompilerParams(dimension_semantics=("parallel", "parallel"),
                                             vmem_limit_bytes=VMEM_LIMIT),
    )(w1, perm)

def _row_copy(src_hbm, dst_buf, sem, src_row, slot, dst_row):
    return pltpu.make_async_copy(src_hbm.at[pl.ds(src_row, 1), :],
                                 dst_buf.at[slot, pl.ds(dst_row, 1), :],
                                 sem.at[slot])


GATHER_UNROLL = 16


def _gather_rows(src_hbm, dst_buf, sem, idx_ref, slot, n_rows):
    def issue(i, _):
        for j in range(2):
            r = 2 * i + j
            _row_copy(src_hbm, dst_buf, sem, idx_ref[0, 0, r], slot, r).start(priority=j)
        return 0
    lax.fori_loop(0, n_rows // 2, issue, 0, unroll=GATHER_UNROLL // 2)


def _wait_rows(src_hbm, dst_buf, sem, slot, n_rows):
    pltpu.make_async_copy(src_hbm.at[pl.ds(0, n_rows), :], dst_buf.at[slot], sem.at[slot]).wait()


ROUTE_ROWS = 512
DISPATCH_ROWS = 256


def _route_kernel(idx_ref, dest_ref, cnt_ref, run_ref, start_ref, *, tm):
    phase = pl.program_id(0)
    i = pl.program_id(1)
    idx = idx_ref[...]
    lane = lax.broadcasted_iota(jnp.int32, idx.shape, 1)
    hot = [jnp.where(lane == idx[:, k:k + 1], 1.0, 0.0) for k in range(TOP_K)]
    cnt = hot[0] + hot[1] + hot[2] + hot[3]
    tile_total = jnp.sum(cnt, axis=0, keepdims=True)

    @pl.when((phase == 0) & (i == 0))
    def _():
        run_ref[...] = jnp.zeros_like(run_ref)

    @pl.when(phase == 0)
    def _():
        run_ref[...] += tile_total
        dest_ref[...] = jnp.zeros_like(dest_ref)

    @pl.when((phase == 1) & (i == 0))
    def _():
        counts = run_ref[...]
        padded = jnp.floor((counts + (EXPERT_ROWS - 1)) * (1.0 / EXPERT_ROWS)) * EXPERT_ROWS
        ri = lax.broadcasted_iota(jnp.int32, (ROUTER_PAD, ROUTER_PAD), 0)
        ci = lax.broadcasted_iota(jnp.int32, (ROUTER_PAD, ROUTER_PAD), 1)
        before = jnp.where(ri < ci, 1.0, 0.0).astype(BF16)
        start = _mm(_parts(jnp.broadcast_to(padded, (8, ROUTER_PAD)), 3), [before])
        start_ref[...] = start[0:1]
        cnt_ref[...] = counts
        run_ref[...] = jnp.zeros_like(run_ref)

    @pl.when(phase == 1)
    def _():
        rt = lax.broadcasted_iota(jnp.int32, (tm, tm), 0)
        ct = lax.broadcasted_iota(jnp.int32, (tm, tm), 1)
        earlier = jnp.where(ct < rt, 1.0, 0.0).astype(BF16)
        pos = start_ref[...] + run_ref[...] + _dot(earlier, cnt.astype(BF16))
        dest = jnp.zeros(idx.shape, F32)
        for k in range(TOP_K):
            d_k = jnp.sum(hot[k] * pos, axis=1, keepdims=True)
            dest = jnp.where(lane == k, d_k, dest)
            pos = pos + hot[k]
        dest_ref[...] = dest.astype(jnp.int32)
        run_ref[...] += tile_total


def _route(idx_pad):
    t = idx_pad.shape[0]
    tm = ROUTE_ROWS
    return pl.pallas_call(
        functools.partial(_route_kernel, tm=tm),
        grid=(2, t // tm),
        in_specs=[pl.BlockSpec((tm, ROUTER_PAD), lambda ph, i: (i, 0))],
        out_specs=[pl.BlockSpec((tm, ROUTER_PAD), lambda ph, i: (ph * i, 0)),
                   pl.BlockSpec((1, ROUTER_PAD), lambda ph, i: (0, 0))],
        out_shape=[jax.ShapeDtypeStruct((t, ROUTER_PAD), jnp.int32),
                   jax.ShapeDtypeStruct((1, ROUTER_PAD), F32)],
        scratch_shapes=[pltpu.VMEM((1, ROUTER_PAD), F32), pltpu.VMEM((1, ROUTER_PAD), F32)],
        compiler_params=pltpu.CompilerParams(dimension_semantics=("arbitrary", "arbitrary"),
                                             vmem_limit_bytes=VMEM_LIMIT),
    )(idx_pad)


def _dispatch_kernel(pad_end_ref, dest_ref, h_ref, xs_hbm, zero_ref, sem, *, tm):
    @pl.when(pl.program_id(0) == 0)
    def _():
        zero_ref[...] = jnp.zeros_like(zero_ref)

        def last_block(e):
            end = pad_end_ref[e]
            start = pl.multiple_of(jnp.maximum(end - EXPERT_ROWS, 0), EXPERT_ROWS)
            return pltpu.make_async_copy(zero_ref, xs_hbm.at[pl.ds(start, EXPERT_ROWS), :], sem.at[0])

        for e in range(N_EXPERTS):
            last_block(e).start()
        for e in range(N_EXPERTS):
            last_block(e).wait()

        def unused_block(b):
            start = pl.multiple_of(b * EXPERT_ROWS, EXPERT_ROWS)
            return pltpu.make_async_copy(zero_ref, xs_hbm.at[pl.ds(start, EXPERT_ROWS), :], sem.at[0])

        first_unused = lax.shift_right_logical(pad_end_ref[N_EXPERTS - 1], EXPERT_ROWS.bit_length() - 1)
        n_blocks = xs_hbm.shape[0] // EXPERT_ROWS

        def start_one(b, _):
            unused_block(b).start()
            return 0

        def wait_one(b, _):
            unused_block(b).wait()
            return 0

        lax.fori_loop(first_unused, n_blocks, start_one, 0)
        lax.fori_loop(first_unused, n_blocks, wait_one, 0)

    def issue(r, _):
        for k in range(TOP_K):
            pltpu.make_async_copy(h_ref.at[pl.ds(r, 1), :],
                                  xs_hbm.at[pl.ds(dest_ref[0, 0, TOP_K * r + k], 1), :],
                                  sem.at[0]).start(priority=k % 2)
        return 0

    lax.fori_loop(0, tm, issue, 0, unroll=4)
    for _ in range(TOP_K):
        pltpu.make_async_copy(h_ref, xs_hbm.at[pl.ds(0, tm), :], sem.at[0]).wait()


def _dispatch(pad_end, dest, h, n_rows):
    t, d = h.shape
    tm = DISPATCH_ROWS
    n = TOP_K * tm
    grid_spec = pltpu.PrefetchScalarGridSpec(
        num_scalar_prefetch=1,
        grid=(t // tm,),
        in_specs=[pl.BlockSpec((1, 1, n), lambda i, pe: (i, 0, 0), memory_space=pltpu.SMEM),
                  pl.BlockSpec((tm, d), lambda i, pe: (i, 0))],
        out_specs=pl.BlockSpec(memory_space=pl.ANY),
        scratch_shapes=[pltpu.VMEM((EXPERT_ROWS, d), F32), pltpu.SemaphoreType.DMA((1,))],
    )
    return pl.pallas_call(
        functools.partial(_dispatch_kernel, tm=tm),
        grid_spec=grid_spec,
        out_shape=jax.ShapeDtypeStruct((n_rows, d), F32),
        compiler_params=pltpu.CompilerParams(dimension_semantics=("arbitrary",),
                                             vmem_limit_bytes=VMEM_LIMIT),
    )(pad_end, dest.reshape(t // tm, 1, n), h)


def _expert_kernel(be_ref, nact_ref, x_ref, w1_ref, b1_ref, w2_ref, b2_ref, y_ref):
    del be_ref

    @pl.when(pl.program_id(0) < nact_ref[0])
    def _():
        f = w2_ref.shape[1]
        hid = _dot(x_ref[...].astype(BF16), w1_ref[0]) + b1_ref[0]
        x_glu = jnp.minimum(hid[:, :f], SWIGLU_LIMIT)
        x_lin = jnp.clip(hid[:, f:], -SWIGLU_LIMIT, SWIGLU_LIMIT)
        act = x_glu * _sigmoid(SWIGLU_ALPHA * x_glu) * (x_lin + 1.0)
        y_ref[...] = _dot(act.astype(BF16), w2_ref[0]) + b2_ref[0]

    @pl.when(pl.program_id(0) >= nact_ref[0])
    def _():
        y_ref[...] = jnp.zeros_like(y_ref)


def _experts(block_expert, n_active, x_sorted, w1p, b1p, w2b, b2):
    n_rows, d = x_sorted.shape
    rows = EXPERT_ROWS
    nblk = n_rows // rows
    f = w2b.shape[1]
    grid_spec = pltpu.PrefetchScalarGridSpec(
        num_scalar_prefetch=2,
        grid=(nblk,),
        in_specs=[
            pl.BlockSpec((rows, d), lambda b, be, na: (jnp.minimum(b, jnp.maximum(na[0] - 1, 0)), 0)),
            pl.BlockSpec((1, d, 2 * f), lambda b, be, na: (be[b], 0, 0)),
            pl.BlockSpec((1, 1, 2 * f), lambda b, be, na: (be[b], 0, 0)),
            pl.BlockSpec((1, f, d), lambda b, be, na: (be[b], 0, 0)),
            pl.BlockSpec((1, 1, d), lambda b, be, na: (be[b], 0, 0)),
        ],
        out_specs=pl.BlockSpec((rows, d), lambda b, be, na: (b, 0)),
    )
    return pl.pallas_call(
        _expert_kernel,
        grid_spec=grid_spec,
        out_shape=jax.ShapeDtypeStruct((n_rows, d), F32),
        compiler_params=pltpu.CompilerParams(dimension_semantics=("arbitrary",),
                                             vmem_limit_bytes=VMEM_LIMIT),
    )(block_expert, n_active, x_sorted, w1p, b1p, w2b, b2)


def _combine_kernel(dest_ref, dest_next_ref, h_ref, gate_ref, g2_ref, b2_ref, y_hbm, o_ref, ybuf, sem,
                    *, alpha, nsteps):
    tm = COMBINE_ROWS
    n = TOP_K * tm
    s = pl.program_id(0)
    slot = s % 2

    @pl.when(s == 0)
    def _():
        _gather_rows(y_hbm, ybuf, sem, dest_ref, 0, n)

    @pl.when(s + 1 < nsteps)
    def _():
        _gather_rows(y_hbm, ybuf, sem, dest_next_ref, 1 - slot, n)

    _wait_rows(y_hbm, ybuf, sem, slot, n)
    gates = gate_ref[...]
    moe = gates[:, 0:1] * ybuf[slot, 0:tm, :]
    for kk in range(1, TOP_K):
        moe = moe + gates[:, kk:kk + 1] * ybuf[slot, kk * tm:(kk + 1) * tm, :]
    o_ref[...] = _layer_norm(alpha * h_ref[...] + moe, g2_ref[...], b2_ref[...])


def _combine(dest_km, h, gate_pad, g2, b2, y_sorted, alpha):
    t, d = h.shape
    tm = COMBINE_ROWS
    nsteps = t // tm
    n = TOP_K * tm
    return pl.pallas_call(
        functools.partial(_combine_kernel, alpha=alpha, nsteps=nsteps),
        grid=(nsteps,),
        in_specs=[
            pl.BlockSpec((1, 1, n), lambda s: (s, 0, 0), memory_space=pltpu.SMEM),
            pl.BlockSpec((1, 1, n), lambda s: (jnp.minimum(s + 1, nsteps - 1), 0, 0),
                         memory_space=pltpu.SMEM),
            pl.BlockSpec((tm, d), lambda s: (s, 0)),
            pl.BlockSpec((tm, ROUTER_PAD), lambda s: (s, 0)),
            pl.BlockSpec((1, d), lambda s: (0, 0)),
            pl.BlockSpec((1, d), lambda s: (0, 0)),
            pl.BlockSpec(memory_space=pl.ANY),
        ],
        out_specs=pl.BlockSpec((tm, d), lambda s: (s, 0)),
        out_shape=jax.ShapeDtypeStruct((t, d), F32),
        scratch_shapes=[pltpu.VMEM((2, n, d), F32), pltpu.SemaphoreType.DMA((2,))],
        compiler_params=pltpu.CompilerParams(dimension_semantics=("arbitrary",),
                                             vmem_limit_bytes=VMEM_LIMIT),
    )(dest_km, dest_km, h, gate_pad, g2, b2, y_sorted)


def _block_tables(counts, nblk):
    rows = EXPERT_ROWS
    counts = counts.astype(jnp.int32)
    padded = (counts + rows - 1) // rows * rows
    pad_ends = jnp.cumsum(padded)
    block_start = jnp.arange(nblk, dtype=jnp.int32) * rows
    block_expert = jnp.sum((pad_ends[None, :] <= block_start[:, None]).astype(jnp.int32), axis=1)
    block_expert = jnp.minimum(block_expert, N_EXPERTS - 1).astype(jnp.int32)
    n_active = (pad_ends[-1] // rows).astype(jnp.int32).reshape(1)
    return block_expert, n_active, pad_ends.astype(jnp.int32)


def _layer(x, w_in, mu_shift, w0, w_decay_up, a0, w_aaa_up, w_gate_up, k_k, k_a, r_k, lnx_g, lnx_b,
           w_attn_br, w_rwkv_br, w_out, ln1_g, ln1_b, w_router, b_router, w1, b1, w2, b2, ln2_g, ln2_b,
           alpha):
    b, s, d = x.shape
    t = b * s
    x2 = x.reshape(t, d)
    xb = x2.astype(BF16)
    row = lambda vec: vec.reshape(1, -1)

    off_q = RW_COLS
    off_gate = off_q + 3 * WIDTH
    pad_cols = RW_COLS_PAD - RW_COLS
    w_rw = jnp.pad(w_in[:, :RW_COLS], ((0, 0), (0, pad_cols))).astype(BF16)
    mu = jnp.pad(mu_shift, (0, pad_cols)).reshape(1, -1)
    w_qkv = w_in[:, off_q:off_gate].astype(BF16)
    w_gate = w_in[:, off_gate:].astype(BF16)

    half = HEAD_DIM // 2
    inv_freq = ROPE_THETA ** (-jnp.arange(half, dtype=F32) / half)
    ang = jnp.arange(s, dtype=F32)[:, None] * inv_freq[None, :]
    cos, sin = jnp.cos(ang), jnp.sin(ang)
    cos_t = jnp.concatenate([cos, cos, cos, cos], axis=1)
    sin_t = jnp.concatenate([-sin, sin, -sin, sin], axis=1)

    proj_rw = _project(xb, w_rw).reshape(b, s, RW_COLS_PAD)
    qt, kaug, vaug, kmean = _project_qkv(xb, w_qkv, cos_t, sin_t, b, s)
    kmean = kmean.reshape(b, s // MOBA_BLOCK, N_HEADS, HEAD_DIM).transpose(0, 2, 1, 3)

    zeros = jnp.zeros((DECAY_LORA, WIDTH), F32)
    w_lora = jnp.concatenate([jnp.concatenate([w_decay_up, zeros], axis=1),
                              jnp.concatenate([zeros, w_aaa_up], axis=1)], axis=0).astype(BF16)
    w_g = jnp.pad(w_gate_up, ((0, GATE_LORA_PAD - GATE_LORA), (0, 0))).astype(BF16)
    y_rwkv = _rwkv(proj_rw, mu, row(w0), row(a0), w_lora, w_g, row(k_k), row(k_a), row(r_k),
                   row(lnx_g), row(lnx_b))
    y_attn = _moba(qt, kaug, vaug, kmean).transpose(0, 2, 1, 3)

    w_r = jnp.pad(w_router, ((0, 0), (0, ROUTER_PAD - N_EXPERTS)))
    b_r = jnp.pad(b_router, (0, ROUTER_PAD - N_EXPERTS), constant_values=NEG_INF).reshape(1, -1)
    h, idx_pad, gate_pad = _merge(x2, y_attn.reshape(t, WIDTH), y_rwkv.reshape(t, WIDTH), w_gate,
                                  w_attn_br.astype(BF16), w_rwkv_br.astype(BF16), w_out.astype(BF16),
                                  row(ln1_g), row(ln1_b), w_r, b_r, alpha)

    dest_pad, counts = _route(idx_pad)
    dest = dest_pad[:, :TOP_K]
    nblk = t * TOP_K // EXPERT_ROWS + N_EXPERTS
    block_expert, n_active, pad_end = _block_tables(counts[0, :N_EXPERTS], nblk)
    x_sorted = _dispatch(pad_end, dest, h, nblk * EXPERT_ROWS)
    w1p = _deinterleave(w1)
    b1p = jnp.concatenate([b1[:, 0::2], b1[:, 1::2]], axis=1)[:, None, :]
    y_sorted = _experts(block_expert, n_active, x_sorted, w1p, b1p, w2.astype(BF16), b2[:, None, :])

    tm = COMBINE_ROWS
    dest_km = dest.reshape(t // tm, tm, TOP_K).transpose(0, 2, 1).reshape(t // tm, 1, TOP_K * tm)
    out = _combine(dest_km, h, gate_pad, row(ln2_g), row(ln2_b), y_sorted, alpha)
    return out.reshape(b, s, d)


def kernel(x, w_in, mu_shift, w0, w_decay_up, a0, w_aaa_up, w_gate_up, k_k, k_a, r_k, lnx_g, lnx_b,
           w_attn_br, w_rwkv_br, w_out, ln1_g, ln1_b, w_router, b_router, w1, b1, w2, b2, ln2_g, ln2_b):
    depth = w_in.shape[0]
    alpha = (2 * depth) ** 0.25
    for l in range(depth):
        x = _layer(x, w_in[l], mu_shift[l], w0[l], w_decay_up[l], a0[l], w_aaa_up[l], w_gate_up[l],
                   k_k[l], k_a[l], r_k[l].reshape(-1), lnx_g[l], lnx_b[l], w_attn_br[l], w_rwkv_br[l],
                   w_out[l], ln1_g[l], ln1_b[l], w_router[l], b_router[l], w1[l], b1[l], w2[l], b2[l],
                   ln2_g[l], ln2_b[l], alpha)
    return x
```

```python
import functools

import jax
import jax.numpy as jnp
from jax import lax
from jax.experimental import pallas as pl
from jax.experimental.pallas import tpu as pltpu

F32 = jnp.float32
BF16 = jnp.bfloat16
HI = lax.Precision.HIGHEST

HEAD_DIM = 64
N_HEADS = 8
WIDTH = N_HEADS * HEAD_DIM
PAIR = 2 * HEAD_DIM
N_PAIRS = N_HEADS // 2
MOBA_BLOCK = 256
MOBA_TOPK = 3
KEY_GROUP = 8
VALUE_ROWS = 2 * HEAD_DIM
ROPE_THETA = 10000.0
DECAY_LORA = 64
AAA_LORA = 64
GATE_LORA = 160
GATE_LORA_PAD = 256
RW_COLS = 3 * WIDTH + DECAY_LORA + AAA_LORA + GATE_LORA
RW_COLS_PAD = 3 * WIDTH + DECAY_LORA + AAA_LORA + GATE_LORA_PAD
GN_EPS = 64e-5
LN_EPS = 1e-5
N_EXPERTS = 32
TOP_K = 4
ROUTER_PAD = 128
SWIGLU_ALPHA = 1.702
SWIGLU_LIMIT = 7.0
RWKV_CHUNK = 64
EXPERT_ROWS = 256
COMBINE_ROWS = 128
VMEM_LIMIT = 48 * 1024 * 1024

NEG_INF = float("-inf")
LOG2_E = 1.4426950408889634
MASK_BIAS = -1e30


def _nt(a, b, precision=None):
    return lax.dot_general(a, b, (((1,), (1,)), ((), ())), precision=precision,
                           preferred_element_type=F32)


def _dot(a, b, precision=None):
    return jnp.dot(a, b, precision=precision, preferred_element_type=F32)


def _matmul_kernel(x_ref, w_ref, o_ref):
    o_ref[...] = _dot(x_ref[...], w_ref[...])


def _project(xb, w, tm=512):
    t, d = xb.shape
    n = w.shape[1]
    return pl.pallas_call(
        _matmul_kernel,
        grid=(t // tm,),
        in_specs=[pl.BlockSpec((tm, d), lambda i: (i, 0)),
                  pl.BlockSpec((d, n), lambda i: (0, 0))],
        out_specs=pl.BlockSpec((tm, n), lambda i: (i, 0)),
        out_shape=jax.ShapeDtypeStruct((t, n), F32),
        compiler_params=pltpu.CompilerParams(dimension_semantics=("parallel",),
                                             vmem_limit_bytes=VMEM_LIMIT),
    )(xb, w)


def _qkv_kernel(x_ref, w_ref, cos_ref, sin_ref, qt_ref, kaug_ref, vaug_ref, km_ref, *, tm, steps_per_seq):
    acc = _dot(x_ref[...], w_ref[...])
    cos = jnp.concatenate([cos_ref[...]] * (WIDTH // PAIR), axis=1)
    sin = jnp.concatenate([sin_ref[...]] * (WIDTH // PAIR), axis=1)
    lane = lax.broadcasted_iota(jnp.int32, (tm, WIDTH), 1)
    first_half = (lane & (HEAD_DIM // 2)) == 0

    def rope(t):
        partner = jnp.where(first_half, pltpu.roll(t, WIDTH - HEAD_DIM // 2, 1),
                            pltpu.roll(t, HEAD_DIM // 2, 1))
        return t * cos + partner * sin

    q = rope(acc[:, :WIDTH]) * (HEAD_DIM ** -0.5 * LOG2_E)
    k = rope(acc[:, WIDTH:2 * WIDTH])
    v = acc[:, 2 * WIDTH:]
    for j in range(tm // MOBA_BLOCK):
        km_ref[0, j:j + 1, :] = jnp.mean(k[j * MOBA_BLOCK:(j + 1) * MOBA_BLOCK], axis=0, keepdims=True)

    lane_p = lax.broadcasted_iota(jnp.int32, (tm, PAIR), 1)
    row_p = lax.broadcasted_iota(jnp.int32, (tm, PAIR), 0)
    first_block = (pl.program_id(0) % steps_per_seq) * (tm // MOBA_BLOCK)
    row_block = lax.shift_right_logical(row_p, MOBA_BLOCK.bit_length() - 1)
    block_tag = jnp.where(lane_p - HEAD_DIM == first_block + row_block, 1.0, 0.0)
    ones = jnp.ones((VALUE_ROWS - HEAD_DIM, tm), F32)
    for pp in range(N_PAIRS):
        sl = slice(pp * PAIR, (pp + 1) * PAIR)
        q_t = jnp.transpose(q[:, sl])
        v_t = jnp.transpose(v[:, sl])
        k_p = k[:, sl]
        k_sw = pltpu.roll(k_p, HEAD_DIM, 1)
        for h, k_h in ((0, k_p), (1, k_sw)):
            rows = slice(h * HEAD_DIM, (h + 1) * HEAD_DIM)
            qt_ref[0, 2 * pp + h] = q_t[rows].astype(BF16)
            kaug_ref[0, 2 * pp + h] = jnp.where(lane_p < HEAD_DIM, k_h, block_tag).astype(BF16)
            vaug_ref[0, 2 * pp + h] = jnp.concatenate([v_t[rows], ones], axis=0).astype(BF16)


def _project_qkv(xb, w, cos_t, sin_t, batch, seq, tm=512):
    t, d = xb.shape
    steps_per_seq = seq // tm
    assert seq // MOBA_BLOCK <= HEAD_DIM, "block one-hot tags must fit the spare lanes of a head"
    tab_spec = pl.BlockSpec((tm, PAIR), lambda i: (i % steps_per_seq, 0))
    return pl.pallas_call(
        functools.partial(_qkv_kernel, tm=tm, steps_per_seq=steps_per_seq),
        grid=(t // tm,),
        in_specs=[pl.BlockSpec((tm, d), lambda i: (i, 0)),
                  pl.BlockSpec((d, 3 * WIDTH), lambda i: (0, 0)),
                  tab_spec, tab_spec],
        out_specs=[
            pl.BlockSpec((1, N_HEADS, HEAD_DIM, tm), lambda i: (i // steps_per_seq, 0, 0, i % steps_per_seq)),
            pl.BlockSpec((1, N_HEADS, tm, PAIR), lambda i: (i // steps_per_seq, 0, i % steps_per_seq, 0)),
            pl.BlockSpec((1, N_HEADS, VALUE_ROWS, tm),
                         lambda i: (i // steps_per_seq, 0, 0, i % steps_per_seq)),
            pl.BlockSpec((1, tm // MOBA_BLOCK, WIDTH), lambda i: (i, 0, 0))],
        out_shape=[jax.ShapeDtypeStruct((batch, N_HEADS, HEAD_DIM, seq), BF16),
                   jax.ShapeDtypeStruct((batch, N_HEADS, seq, PAIR), BF16),
                   jax.ShapeDtypeStruct((batch, N_HEADS, VALUE_ROWS, seq), BF16),
                   jax.ShapeDtypeStruct((t // tm, tm // MOBA_BLOCK, WIDTH), F32)],
        compiler_params=pltpu.CompilerParams(dimension_semantics=("parallel",),
                                             vmem_limit_bytes=VMEM_LIMIT),
    )(xb, w, cos_t, sin_t)


SPLIT_PARTS = 1


def _parts(x, n=SPLIT_PARTS):
    out = []
    for _ in range(n):
        piece = x.astype(BF16)
        out.append(piece)
        x = x - piece.astype(F32)
    return out


def _mm(a_parts, b_parts, f=None):
    f = f or _dot
    order = max(len(a_parts), len(b_parts))
    acc = None
    for i, a in enumerate(a_parts):
        for j, b in enumerate(b_parts):
            if i + j < order:
                term = f(a, b)
                acc = term if acc is None else acc + term
    return acc


def _softplus(z):
    return jnp.maximum(z, 0.0) + jnp.log(1.0 + jnp.exp(-jnp.abs(z)))


def _sigmoid(z):
    return 1.0 / (1.0 + jnp.exp(-z))


def _rwkv_kernel(p_ref, mu_ref, w0_ref, a0_ref, wlora_ref, wg_ref, kk_ref, ka_ref, rk_ref,
                 lng_ref, lnb_ref, y_ref, carry_ref, state_ref):
    c = RWKV_CHUNK
    nbatch = p_ref.shape[0]

    @pl.when(pl.program_id(0) == 0)
    def _():
        carry_ref[...] = jnp.zeros_like(carry_ref)
        state_ref[...] = jnp.zeros_like(state_ref)

    ri = lax.broadcasted_iota(jnp.int32, (PAIR, PAIR), 0)
    ci = lax.broadcasted_iota(jnp.int32, (PAIR, PAIR), 1)
    head_sum = jnp.where((ri // HEAD_DIM) == (ci // HEAD_DIM), 1.0, 0.0).astype(F32)
    eye = jnp.where(ri == ci, 1.0, 0.0).astype(F32)
    strict_lower = ri > ci
    lower = ri >= ci
    rc = lax.broadcasted_iota(jnp.int32, (c, c), 0)
    cc = lax.broadcasted_iota(jnp.int32, (c, c), 1)
    cumsum_mat = jnp.where(rc >= cc, 1.0, 0.0).astype(F32)
    lane_p = lax.broadcasted_iota(jnp.int32, (c, PAIR), 1)
    head0 = lane_p < HEAD_DIM

    def stack(t):
        return jnp.concatenate([jnp.where(head0, t, 0.0), jnp.where(head0, 0.0, t)], axis=0)

    head_sum_b = [head_sum.astype(BF16)]
    cumsum_b = [cumsum_mat.astype(BF16)]

    chains = []
    for bi in range(nbatch):
        p = p_ref[bi]
        row = lax.broadcasted_iota(jnp.int32, p.shape, 0)
        prev = jnp.where(row == 0, carry_ref[bi, 0:1, :], pltpu.roll(p, 1, 0))
        carry_ref[bi] = jnp.broadcast_to(p[c - 1:c, :], carry_ref.shape[1:])
        sh = p + (prev - p) * mu_ref[...]
        r = sh[:, 0:WIDTH]
        k = sh[:, WIDTH:2 * WIDTH]
        v = sh[:, 2 * WIDTH:3 * WIDTH]
        lora = sh[:, 3 * WIDTH:3 * WIDTH + PAIR]
        hg = sh[:, 3 * WIDTH + PAIR:]
        lane_l = lax.broadcasted_iota(jnp.int32, lora.shape, 1)
        lora_act = jnp.where(lane_l < DECAY_LORA, jnp.tanh(lora), lora)
        wa = _dot(lora_act.astype(BF16), wlora_ref[...])
        w_log = -_softplus(-(w0_ref[...] + wa[:, :WIDTH])) - 0.5
        logw = -jnp.exp(w_log)
        a = _sigmoid(a0_ref[...] + wa[:, WIDTH:])
        g = _dot(_sigmoid(hg).astype(BF16), wg_ref[...])
        kkn = k * kk_ref[...]
        k2 = k * (1.0 + (a - 1.0) * ka_ref[...])
        cum_all = _mm(cumsum_b, _parts(logw, 3))
        for pp in range(N_PAIRS):
            sl = slice(pp * PAIR, (pp + 1) * PAIR)
            chains.append(dict(bi=bi, pp=pp, sl=sl, r=r[:, sl], k=k2[:, sl], v=v[:, sl], a=a[:, sl],
                               kk=kkn[:, sl], lw=logw[:, sl], cum=cum_all[:, sl], g=g[:, sl]))

    for ch in chains:
        ch['ss'] = _mm(_parts(ch['kk'] * ch['kk'], 2), head_sum_b)
    for ch in chains:
        kap = ch['kk'] / jnp.maximum(jnp.sqrt(ch['ss']), 1e-12)
        cum = ch['cum']
        ch['pc'] = jnp.exp(cum[c - 1:c, :])
        inv = jnp.exp(-cum)
        rm = stack(ch['r'] * jnp.exp(cum))
        bm = stack(kap * jnp.exp(cum - ch['lw']))
        am = stack(-(kap * ch['a']) * inv)
        km = stack(ch['k'] * inv)
        ch.update(rm=rm, bm=bm, am=am, km=km, vm=stack(ch['v']))
    for ch in chains:
        ch['sb'] = _mm(_parts(jnp.concatenate([ch['bm'], ch['rm']], axis=0)),
                       _parts(jnp.concatenate([ch['am'], ch['km']], axis=0)), _nt)
    for ch in chains:
        sb = ch['sb']
        ch['la'] = jnp.where(strict_lower, sb[:2 * c, :2 * c], 0.0)
        ch['lk'] = jnp.where(strict_lower, sb[:2 * c, 2 * c:], 0.0)
        ch['ma'] = jnp.where(lower, sb[2 * c:, :2 * c], 0.0)
        ch['mk'] = jnp.where(lower, sb[2 * c:, 2 * c:], 0.0)
        ch['tinv'] = eye + ch['la']
        ch['lpow'] = ch['la']
    n = 2
    while n < c:
        for ch in chains:
            lp = _parts(ch['lpow'])
            ch['lpow'] = _mm(lp, lp)
        for ch in chains:
            ch['tinv'] = ch['tinv'] + _mm(_parts(ch['tinv']), _parts(ch['lpow']))
        n *= 2

    for ch in chains:
        ch['h0'] = state_ref[ch['bi'], ch['pp']]
        ch['rhs'] = _mm(_parts(jnp.concatenate([ch['bm'], ch['lk']], axis=1)),
                        _parts(jnp.concatenate([ch['h0'], ch['vm']], axis=0)))
    for ch in chains:
        ch['u'] = _mm(_parts(ch['tinv']), _parts(ch['rhs']))
    for ch in chains:
        yst = _mm(_parts(jnp.concatenate([ch['rm'], ch['ma'], ch['mk']], axis=1)),
                  _parts(jnp.concatenate([ch['h0'], ch['u'], ch['vm']], axis=0)))
        ch['y'] = yst[:c] + yst[c:]
    for ch in chains:
        pc = ch['pc']
        pc_col = jnp.transpose(jnp.broadcast_to(pc, (PAIR, PAIR)))
        upd = _mm(_parts(jnp.concatenate([jnp.transpose(ch['am'] * pc), jnp.transpose(ch['km'] * pc)],
                                         axis=1)),
                  _parts(jnp.concatenate([ch['u'], ch['vm']], axis=0)))
        state_ref[ch['bi'], ch['pp']] = ch['h0'] * pc_col + upd
    for ch in chains:
        ch['mean'] = _mm(_parts(ch['y'], 2), head_sum_b) * (1.0 / HEAD_DIM)
        ch['bonus'] = _mm(_parts(ch['r'] * ch['k'] * rk_ref[:, ch['sl']], 2), head_sum_b) * ch['v']
    for ch in chains:
        yc = ch['y'] - ch['mean']
        ch['yc'] = yc
        ch['var'] = _mm(_parts(yc * yc, 2), head_sum_b) * (1.0 / HEAD_DIM)
    for ch in chains:
        sl = ch['sl']
        yn = ch['yc'] * lax.rsqrt(ch['var'] + GN_EPS) * lng_ref[:, sl] + lnb_ref[:, sl]
        y_ref[ch['bi'], :, sl] = ((yn + ch['bonus']) * ch['g']).astype(y_ref.dtype)


def _rwkv(proj_rw, mu, w0, a0, wlora, wg, k_k, k_a, r_k, lnx_g, lnx_b):
    b, s, n = proj_rw.shape
    c = RWKV_CHUNK
    vec = lambda width: pl.BlockSpec((1, width), lambda ci: (0, 0))
    return pl.pallas_call(
        _rwkv_kernel,
        grid=(s // c,),
        in_specs=[pl.BlockSpec((b, c, n), lambda ci: (0, ci, 0)),
                  vec(n), vec(WIDTH), vec(WIDTH),
                  pl.BlockSpec(wlora.shape, lambda ci: (0, 0)),
                  pl.BlockSpec(wg.shape, lambda ci: (0, 0)),
                  vec(WIDTH), vec(WIDTH), vec(WIDTH), vec(WIDTH), vec(WIDTH)],
        out_specs=pl.BlockSpec((b, c, WIDTH), lambda ci: (0, ci, 0)),
        out_shape=jax.ShapeDtypeStruct((b, s, WIDTH), BF16),
        scratch_shapes=[pltpu.VMEM((b, 8, n), F32),
                        pltpu.VMEM((b, N_PAIRS, PAIR, PAIR), F32)],
        compiler_params=pltpu.CompilerParams(dimension_semantics=("arbitrary",),
                                             vmem_limit_bytes=VMEM_LIMIT),
    )(proj_rw, mu, w0, a0, wlora, wg, k_k, k_a, r_k, lnx_g, lnx_b)


def _moba_kernel(qt_ref, k_ref, vt_ref, km_ref, o_ref, sa_ref, sb_ref, mxa_ref, mxb_ref, *, nb):
    bs = MOBA_BLOCK
    i = pl.program_id(2)
    qt = qt_ref[0, 0]
    km = km_ref[0, 0].astype(BF16)
    blk = lax.broadcasted_iota(jnp.int32, (nb, bs), 0).astype(F32)
    gate = jnp.where(blk < i.astype(F32), _dot(km, qt), NEG_INF)
    sel = jnp.zeros((nb, bs), F32)
    for _ in range(MOBA_TOPK):
        mx = jnp.max(gate, axis=0, keepdims=True)
        hit = (gate == mx) & (mx > NEG_INF)
        idx = jnp.min(jnp.where(hit, blk, float(nb)), axis=0, keepdims=True)
        pick = blk == idx
        sel = jnp.where(pick, 1.0, sel)
        gate = jnp.where(pick, NEG_INF, gate)
    bias = jnp.where(sel > 0.0, 0.0, MASK_BIAS)
    if nb < HEAD_DIM:
        bias = jnp.concatenate([bias, jnp.zeros((HEAD_DIM - nb, bs), F32)], axis=0)
    q_sel = jnp.concatenate([qt, bias.astype(BF16)], axis=0)
    q_own = jnp.concatenate([qt, jnp.zeros((HEAD_DIM, bs), BF16)], axis=0)

    start = pl.multiple_of(i * bs, bs)
    key_i = lax.broadcasted_iota(jnp.int32, (bs, bs), 0)
    qry_i = lax.broadcasted_iota(jnp.int32, (bs, bs), 1)
    s = jnp.where(key_i <= qry_i, _dot(k_ref[0, 0, pl.ds(start, bs), :], q_own), NEG_INF)
    m = jnp.max(s, axis=0, keepdims=True)
    p = jnp.exp2(s - m)
    acc = _dot(vt_ref[0, 0, :, pl.ds(start, bs)], p.astype(BF16))

    span = min(KEY_GROUP, nb) * bs

    n_groups = lax.shift_right_logical(i * bs + span - 1, span.bit_length() - 1)

    def scores(g):
        off = pl.multiple_of(g * span, span)
        return _dot(k_ref[0, 0, pl.ds(off, span), :], q_sel)

    def col_max(sc):
        return jnp.max(jnp.max(sc.reshape(span // bs, bs, bs), axis=0), axis=0, keepdims=True)

    def put(s_buf, mx_buf, g):
        s_new = scores(g)
        s_buf[...] = s_new
        mx_buf[...] = col_max(s_new)

    def absorb(s_buf, mx_buf, g, m, acc):
        m_new = jnp.maximum(m, mx_buf[...])
        off = pl.multiple_of(g * span, span)
        p = jnp.exp2(s_buf[...] - m_new)
        acc = jnp.exp2(m - m_new) * acc + _dot(vt_ref[0, 0, :, pl.ds(off, span)], p.astype(BF16))
        return m_new, acc

    @pl.when(n_groups > 0)
    def _():
        put(sa_ref, mxa_ref, 0)

    def pair(t, carry):
        put(sb_ref, mxb_ref, 2 * t + 1)
        m, acc = absorb(sa_ref, mxa_ref, 2 * t, *carry)
        put(sa_ref, mxa_ref, 2 * t + 2)
        return absorb(sb_ref, mxb_ref, 2 * t + 1, m, acc)

    n_pairs = lax.shift_right_logical(jnp.maximum(n_groups - 1, 0), 1)
    m, acc = lax.fori_loop(0, n_pairs, pair, (m, acc))
    last_even = 2 * n_pairs

    def tail_two(m, acc):
        put(sb_ref, mxb_ref, last_even + 1)
        m, acc = absorb(sa_ref, mxa_ref, last_even, m, acc)
        return absorb(sb_ref, mxb_ref, last_even + 1, m, acc)

    def tail_one(m, acc):
        return absorb(sa_ref, mxa_ref, last_even, m, acc)

    def tail(m, acc):
        return lax.cond(n_groups - last_even == 2, tail_two, tail_one, m, acc)

    m, acc = lax.cond(n_groups > 0, tail, lambda m, acc: (m, acc), m, acc)
    out_t = acc[:HEAD_DIM] / acc[HEAD_DIM:HEAD_DIM + 1]
    o_ref[0, 0] = jnp.transpose(out_t).astype(o_ref.dtype)


def _moba(qt, kaug, vaug, kmean):
    b, nh, _, s = qt.shape
    nb = s // MOBA_BLOCK
    group = min(KEY_GROUP, nb)
    assert nb % group == 0
    return pl.pallas_call(
        functools.partial(_moba_kernel, nb=nb),
        grid=(b, nh, nb),
        in_specs=[pl.BlockSpec((1, 1, HEAD_DIM, MOBA_BLOCK), lambda bi, hi, qi: (bi, hi, 0, qi)),
                  pl.BlockSpec((1, 1, s, PAIR), lambda bi, hi, qi: (bi, hi, 0, 0)),
                  pl.BlockSpec((1, 1, VALUE_ROWS, s), lambda bi, hi, qi: (bi, hi, 0, 0)),
                  pl.BlockSpec((1, 1, nb, HEAD_DIM), lambda bi, hi, qi: (bi, hi, 0, 0))],
        out_specs=pl.BlockSpec((1, 1, MOBA_BLOCK, HEAD_DIM), lambda bi, hi, qi: (bi, hi, qi, 0)),
        out_shape=jax.ShapeDtypeStruct((b, nh, s, HEAD_DIM), BF16),
        scratch_shapes=[pltpu.VMEM((group * MOBA_BLOCK, MOBA_BLOCK), F32),
                        pltpu.VMEM((group * MOBA_BLOCK, MOBA_BLOCK), F32),
                        pltpu.VMEM((1, MOBA_BLOCK), F32),
                        pltpu.VMEM((1, MOBA_BLOCK), F32)],
        compiler_params=pltpu.CompilerParams(
            dimension_semantics=("parallel", "parallel", "arbitrary"),
            vmem_limit_bytes=VMEM_LIMIT),
    )(qt, kaug, vaug, kmean)


def _layer_norm(z, g, b):
    mu = jnp.mean(z, axis=1, keepdims=True)
    zc = z - mu
    var = jnp.mean(zc * zc, axis=1, keepdims=True)
    return zc * lax.rsqrt(var + LN_EPS) * g + b


def _merge_kernel(x_ref, ya_ref, yr_ref, wgate_ref, wab_ref, wrb_ref, wout_ref, g1_ref, b1_ref,
                  wr_hi_ref, wr_lo_ref, br_ref, h_ref, idx_ref, gate_ref, *, alpha):
    x = x_ref[...]
    d = x.shape[1]
    gates = _sigmoid(_dot(x.astype(BF16), wgate_ref[...]))
    mixed = (gates[:, :d] * _dot(ya_ref[...], wab_ref[...])
             + gates[:, d:] * _dot(yr_ref[...], wrb_ref[...]))
    h = _layer_norm(alpha * x + _dot(mixed.astype(BF16), wout_ref[...]), g1_ref[...], b1_ref[...])
    h_ref[...] = h

    logits = _mm(_parts(h, 2), [wr_hi_ref[...], wr_lo_ref[...]]) + br_ref[...]
    col = lax.broadcasted_iota(jnp.int32, logits.shape, 1).astype(F32)
    idx_out = jnp.zeros(logits.shape, F32)
    val_out = jnp.zeros(logits.shape, F32)
    top = None
    denom = None
    for t in range(TOP_K):
        mx = jnp.max(logits, axis=1, keepdims=True)
        idx = jnp.min(jnp.where(logits == mx, col, float(ROUTER_PAD)), axis=1, keepdims=True)
        if t == 0:
            top = mx
        e = jnp.exp(mx - top)
        denom = e if t == 0 else denom + e
        idx_out = jnp.where(col == float(t), idx, idx_out)
        val_out = jnp.where(col == float(t), e, val_out)
        logits = jnp.where(col == idx, NEG_INF, logits)
    idx_ref[...] = idx_out.astype(jnp.int32)
    gate_ref[...] = val_out / denom


def _merge(x2, ya, yr, wgate, wab, wrb, wout, g1, b1, wr, br, alpha, tm=256):
    wr_hi, wr_lo = _parts(wr, 2)
    t, d = x2.shape
    row = lambda width: pl.BlockSpec((tm, width), lambda i: (i, 0))
    full = lambda arr: pl.BlockSpec(arr.shape, lambda i: (0, 0))
    return pl.pallas_call(
        functools.partial(_merge_kernel, alpha=alpha),
        grid=(t // tm,),
        in_specs=[row(d), row(WIDTH), row(WIDTH), full(wgate), full(wab), full(wrb), full(wout),
                  full(g1), full(b1), full(wr_hi), full(wr_lo), full(br)],
        out_specs=[row(d), row(ROUTER_PAD), row(ROUTER_PAD)],
        out_shape=[jax.ShapeDtypeStruct((t, d), F32),
                   jax.ShapeDtypeStruct((t, ROUTER_PAD), jnp.int32),
                   jax.ShapeDtypeStruct((t, ROUTER_PAD), F32)],
        compiler_params=pltpu.CompilerParams(dimension_semantics=("parallel",),
                                             vmem_limit_bytes=VMEM_LIMIT),
    )(x2, ya, yr, wgate, wab, wrb, wout, g1, b1, wr_hi, wr_lo, br)


DEINTERLEAVE_GROUP = 256


def _deinterleave_kernel(w_ref, perm_ref, o_ref):
    g = DEINTERLEAVE_GROUP
    n = w_ref.shape[2]
    half = n // 2
    for c in range(n // g):
        res = _dot(w_ref[0, :, c * g:(c + 1) * g].astype(BF16), perm_ref[...])
        o_ref[0, :, c * g // 2:(c + 1) * g // 2] = res[:, :g // 2].astype(BF16)
        o_ref[0, :, half + c * g // 2:half + (c + 1) * g // 2] = res[:, g // 2:].astype(BF16)


def _deinterleave(w1, tr=256):
    e, d, n = w1.shape
    g = DEINTERLEAVE_GROUP
    src = jnp.arange(g)
    dst = jnp.where(src % 2 == 0, src // 2, g // 2 + src // 2)
    perm = (dst[:, None] == jnp.arange(g)[None, :]).astype(BF16)
    return pl.pallas_call(
        _deinterleave_kernel,
        grid=(e, d // tr),
        in_specs=[pl.BlockSpec((1, tr, n), lambda ei, ri: (ei, ri, 0)),
                  pl.BlockSpec((g, g), lambda ei, ri: (0, 0))],
        out_specs=pl.BlockSpec((1, tr, n), lambda ei, ri: (ei, ri, 0)),
        out_shape=jax.ShapeDtypeStruct((e, d, n), BF16),
        compiler_params=pltpu.CompilerParams(dimension_semantics=("parallel", "parallel"),
                                             vmem_limit_bytes=VMEM_LIMIT),
    )(w1, perm)

def _row_copy(src_hbm, dst_buf, sem, src_row, slot, dst_row):
    return pltpu.make_async_copy(src_hbm.at[pl.ds(src_row, 1), :],
                                 dst_buf.at[slot, pl.ds(dst_row, 1), :],
                                 sem.at[slot])


GATHER_UNROLL = 16


def _gather_rows(src_hbm, dst_buf, sem, idx_ref, slot, n_rows):
    def issue(i, _):
        for j in range(2):
            r = 2 * i + j
            _row_copy(src_hbm, dst_buf, sem, idx_ref[0, 0, r], slot, r).start(priority=j)
        return 0
    lax.fori_loop(0, n_rows // 2, issue, 0, unroll=GATHER_UNROLL // 2)


def _wait_rows(src_hbm, dst_buf, sem, slot, n_rows):
    pltpu.make_async_copy(src_hbm.at[pl.ds(0, n_rows), :], dst_buf.at[slot], sem.at[slot]).wait()


ROUTE_ROWS = 512
DISPATCH_ROWS = 256


def _route_kernel(idx_ref, dest_ref, cnt_ref, run_ref, start_ref, *, tm):
    phase = pl.program_id(0)
    i = pl.program_id(1)
    idx = idx_ref[...]
    lane = lax.broadcasted_iota(jnp.int32, idx.shape, 1)
    hot = [jnp.where(lane == idx[:, k:k + 1], 1.0, 0.0) for k in range(TOP_K)]
    cnt = hot[0] + hot[1] + hot[2] + hot[3]
    tile_total = jnp.sum(cnt, axis=0, keepdims=True)

    @pl.when((phase == 0) & (i == 0))
    def _():
        run_ref[...] = jnp.zeros_like(run_ref)

    @pl.when(phase == 0)
    def _():
        run_ref[...] += tile_total
        dest_ref[...] = jnp.zeros_like(dest_ref)

    @pl.when((phase == 1) & (i == 0))
    def _():
        counts = run_ref[...]
        padded = jnp.floor((counts + (EXPERT_ROWS - 1)) * (1.0 / EXPERT_ROWS)) * EXPERT_ROWS
        ri = lax.broadcasted_iota(jnp.int32, (ROUTER_PAD, ROUTER_PAD), 0)
        ci = lax.broadcasted_iota(jnp.int32, (ROUTER_PAD, ROUTER_PAD), 1)
        before = jnp.where(ri < ci, 1.0, 0.0).astype(BF16)
        start = _mm(_parts(jnp.broadcast_to(padded, (8, ROUTER_PAD)), 3), [before])
        start_ref[...] = start[0:1]
        cnt_ref[...] = counts
        run_ref[...] = jnp.zeros_like(run_ref)

    @pl.when(phase == 1)
    def _():
        rt = lax.broadcasted_iota(jnp.int32, (tm, tm), 0)
        ct = lax.broadcasted_iota(jnp.int32, (tm, tm), 1)
        earlier = jnp.where(ct < rt, 1.0, 0.0).astype(BF16)
        pos = start_ref[...] + run_ref[...] + _dot(earlier, cnt.astype(BF16))
        dest = jnp.zeros(idx.shape, F32)
        for k in range(TOP_K):
            d_k = jnp.sum(hot[k] * pos, axis=1, keepdims=True)
            dest = jnp.where(lane == k, d_k, dest)
            pos = pos + hot[k]
        dest_ref[...] = dest.astype(jnp.int32)
        run_ref[...] += tile_total


def _route(idx_pad):
    t = idx_pad.shape[0]
    tm = ROUTE_ROWS
    return pl.pallas_call(
        functools.partial(_route_kernel, tm=tm),
        grid=(2, t // tm),
        in_specs=[pl.BlockSpec((tm, ROUTER_PAD), lambda ph, i: (i, 0))],
        out_specs=[pl.BlockSpec((tm, ROUTER_PAD), lambda ph, i: (ph * i, 0)),
                   pl.BlockSpec((1, ROUTER_PAD), lambda ph, i: (0, 0))],
        out_shape=[jax.ShapeDtypeStruct((t, ROUTER_PAD), jnp.int32),
                   jax.ShapeDtypeStruct((1, ROUTER_PAD), F32)],
        scratch_shapes=[pltpu.VMEM((1, ROUTER_PAD), F32), pltpu.VMEM((1, ROUTER_PAD), F32)],
        compiler_params=pltpu.CompilerParams(dimension_semantics=("arbitrary", "arbitrary"),
                                             vmem_limit_bytes=VMEM_LIMIT),
    )(idx_pad)


def _dispatch_kernel(pad_end_ref, dest_ref, h_ref, xs_hbm, zero_ref, sem, *, tm):
    @pl.when(pl.program_id(0) == 0)
    def _():
        zero_ref[...] = jnp.zeros_like(zero_ref)

        def last_block(e):
            end = pad_end_ref[e]
            start = pl.multiple_of(jnp.maximum(end - EXPERT_ROWS, 0), EXPERT_ROWS)
            return pltpu.make_async_copy(zero_ref, xs_hbm.at[pl.ds(start, EXPERT_ROWS), :], sem.at[0])

        for e in range(N_EXPERTS):
            last_block(e).start()
        for e in range(N_EXPERTS):
            last_block(e).wait()

        def unused_block(b):
            start = pl.multiple_of(b * EXPERT_ROWS, EXPERT_ROWS)
            return pltpu.make_async_copy(zero_ref, xs_hbm.at[pl.ds(start, EXPERT_ROWS), :], sem.at[0])

        first_unused = lax.shift_right_logical(pad_end_ref[N_EXPERTS - 1], EXPERT_ROWS.bit_length() - 1)
        n_blocks = xs_hbm.shape[0] // EXPERT_ROWS

        def start_one(b, _):
            unused_block(b).start()
            return 0

        def wait_one(b, _):
            unused_block(b).wait()
            return 0

        lax.fori_loop(first_unused, n_blocks, start_one, 0)
        lax.fori_loop(first_unused, n_blocks, wait_one, 0)

    def issue(r, _):
        for k in range(TOP_K):
            pltpu.make_async_copy(h_ref.at[pl.ds(r, 1), :],
                                  xs_hbm.at[pl.ds(dest_ref[0, 0, TOP_K * r + k], 1), :],
                                  sem.at[0]).start(priority=k % 2)
        return 0

    lax.fori_loop(0, tm, issue, 0, unroll=4)
    for _ in range(TOP_K):
        pltpu.make_async_copy(h_ref, xs_hbm.at[pl.ds(0, tm), :], sem.at[0]).wait()


def _dispatch(pad_end, dest, h, n_rows):
    t, d = h.shape
    tm = DISPATCH_ROWS
    n = TOP_K * tm
    grid_spec = pltpu.PrefetchScalarGridSpec(
        num_scalar_prefetch=1,
        grid=(t // tm,),
        in_specs=[pl.BlockSpec((1, 1, n), lambda i, pe: (i, 0, 0), memory_space=pltpu.SMEM),
                  pl.BlockSpec((tm, d), lambda i, pe: (i, 0))],
        out_specs=pl.BlockSpec(memory_space=pl.ANY),
        scratch_shapes=[pltpu.VMEM((EXPERT_ROWS, d), F32), pltpu.SemaphoreType.DMA((1,))],
    )
    return pl.pallas_call(
        functools.partial(_dispatch_kernel, tm=tm),
        grid_spec=grid_spec,
        out_shape=jax.ShapeDtypeStruct((n_rows, d), F32),
        compiler_params=pltpu.CompilerParams(dimension_semantics=("arbitrary",),
                                             vmem_limit_bytes=VMEM_LIMIT),
    )(pad_end, dest.reshape(t // tm, 1, n), h)


def _expert_kernel(be_ref, nact_ref, x_ref, w1_ref, b1_ref, w2_ref, b2_ref, y_ref):
    del be_ref

    @pl.when(pl.program_id(0) < nact_ref[0])
    def _():
        f = w2_ref.shape[1]
        hid = _dot(x_ref[...].astype(BF16), w1_ref[0]) + b1_ref[0]
        x_glu = jnp.minimum(hid[:, :f], SWIGLU_LIMIT)
        x_lin = jnp.clip(hid[:, f:], -SWIGLU_LIMIT, SWIGLU_LIMIT)
        act = x_glu * _sigmoid(SWIGLU_ALPHA * x_glu) * (x_lin + 1.0)
        y_ref[...] = _dot(act.astype(BF16), w2_ref[0]) + b2_ref[0]

    @pl.when(pl.program_id(0) >= nact_ref[0])
    def _():
        y_ref[...] = jnp.zeros_like(y_ref)


def _experts(block_expert, n_active, x_sorted, w1p, b1p, w2b, b2):
    n_rows, d = x_sorted.shape
    rows = EXPERT_ROWS
    nblk = n_rows // rows
    f = w2b.shape[1]
    grid_spec = pltpu.PrefetchScalarGridSpec(
        num_scalar_prefetch=2,
        grid=(nblk,),
        in_specs=[
            pl.BlockSpec((rows, d), lambda b, be, na: (jnp.minimum(b, jnp.maximum(na[0] - 1, 0)), 0)),
            pl.BlockSpec((1, d, 2 * f), lambda b, be, na: (be[b], 0, 0)),
            pl.BlockSpec((1, 1, 2 * f), lambda b, be, na: (be[b], 0, 0)),
            pl.BlockSpec((1, f, d), lambda b, be, na: (be[b], 0, 0)),
            pl.BlockSpec((1, 1, d), lambda b, be, na: (be[b], 0, 0)),
        ],
        out_specs=pl.BlockSpec((rows, d), lambda b, be, na: (b, 0)),
    )
    return pl.pallas_call(
        _expert_kernel,
        grid_spec=grid_spec,
        out_shape=jax.ShapeDtypeStruct((n_rows, d), F32),
        compiler_params=pltpu.CompilerParams(dimension_semantics=("arbitrary",),
                                             vmem_limit_bytes=VMEM_LIMIT),
    )(block_expert, n_active, x_sorted, w1p, b1p, w2b, b2)


def _combine_kernel(dest_ref, dest_next_ref, h_ref, gate_ref, g2_ref, b2_ref, y_hbm, o_ref, ybuf, sem,
                    *, alpha, nsteps):
    tm = COMBINE_ROWS
    n = TOP_K * tm
    s = pl.program_id(0)
    slot = s % 2

    @pl.when(s == 0)
    def _():
        _gather_rows(y_hbm, ybuf, sem, dest_ref, 0, n)

    @pl.when(s + 1 < nsteps)
    def _():
        _gather_rows(y_hbm, ybuf, sem, dest_next_ref, 1 - slot, n)

    _wait_rows(y_hbm, ybuf, sem, slot, n)
    gates = gate_ref[...]
    moe = gates[:, 0:1] * ybuf[slot, 0:tm, :]
    for kk in range(1, TOP_K):
        moe = moe + gates[:, kk:kk + 1] * ybuf[slot, kk * tm:(kk + 1) * tm, :]
    o_ref[...] = _layer_norm(alpha * h_ref[...] + moe, g2_ref[...], b2_ref[...])


def _combine(dest_km, h, gate_pad, g2, b2, y_sorted, alpha):
    t, d = h.shape
    tm = COMBINE_ROWS
    nsteps = t // tm
    n = TOP_K * tm
    return pl.pallas_call(
        functools.partial(_combine_kernel, alpha=alpha, nsteps=nsteps),
        grid=(nsteps,),
        in_specs=[
            pl.BlockSpec((1, 1, n), lambda s: (s, 0, 0), memory_space=pltpu.SMEM),
            pl.BlockSpec((1, 1, n), lambda s: (jnp.minimum(s + 1, nsteps - 1), 0, 0),
                         memory_space=pltpu.SMEM),
            pl.BlockSpec((tm, d), lambda s: (s, 0)),
            pl.BlockSpec((tm, ROUTER_PAD), lambda s: (s, 0)),
            pl.BlockSpec((1, d), lambda s: (0, 0)),
            pl.BlockSpec((1, d), lambda s: (0, 0)),
            pl.BlockSpec(memory_space=pl.ANY),
        ],
        out_specs=pl.BlockSpec((tm, d), lambda s: (s, 0)),
        out_shape=jax.ShapeDtypeStruct((t, d), F32),
        scratch_shapes=[pltpu.VMEM((2, n, d), F32), pltpu.SemaphoreType.DMA((2,))],
        compiler_params=pltpu.CompilerParams(dimension_semantics=("arbitrary",),
                                             vmem_limit_bytes=VMEM_LIMIT),
    )(dest_km, dest_km, h, gate_pad, g2, b2, y_sorted)


def _block_tables(counts, nblk):
    rows = EXPERT_ROWS
    counts = counts.astype(jnp.int32)
    padded = (counts + rows - 1) // rows * rows
    pad_ends = jnp.cumsum(padded)
    block_start = jnp.arange(nblk, dtype=jnp.int32) * rows
    block_expert = jnp.sum((pad_ends[None, :] <= block_start[:, None]).astype(jnp.int32), axis=1)
    block_expert = jnp.minimum(block_expert, N_EXPERTS - 1).astype(jnp.int32)
    n_active = (pad_ends[-1] // rows).astype(jnp.int32).reshape(1)
    return block_expert, n_active, pad_ends.astype(jnp.int32)


def _layer(x, w_in, mu_shift, w0, w_decay_up, a0, w_aaa_up, w_gate_up, k_k, k_a, r_k, lnx_g, lnx_b,
           w_attn_br, w_rwkv_br, w_out, ln1_g, ln1_b, w_router, b_router, w1, b1, w2, b2, ln2_g, ln2_b,
           alpha):
    b, s, d = x.shape
    t = b * s
    x2 = x.reshape(t, d)
    xb = x2.astype(BF16)
    row = lambda vec: vec.reshape(1, -1)

    off_q = RW_COLS
    off_gate = off_q + 3 * WIDTH
    pad_cols = RW_COLS_PAD - RW_COLS
    w_rw = jnp.pad(w_in[:, :RW_COLS], ((0, 0), (0, pad_cols))).astype(BF16)
    mu = jnp.pad(mu_shift, (0, pad_cols)).reshape(1, -1)
    w_qkv = w_in[:, off_q:off_gate].astype(BF16)
    w_gate = w_in[:, off_gate:].astype(BF16)

    half = HEAD_DIM // 2
    inv_freq = ROPE_THETA ** (-jnp.arange(half, dtype=F32) / half)
    ang = jnp.arange(s, dtype=F32)[:, None] * inv_freq[None, :]
    cos, sin = jnp.cos(ang), jnp.sin(ang)
    cos_t = jnp.concatenate([cos, cos, cos, cos], axis=1)
    sin_t = jnp.concatenate([-sin, sin, -sin, sin], axis=1)

    proj_rw = _project(xb, w_rw).reshape(b, s, RW_COLS_PAD)
    qt, kaug, vaug, kmean = _project_qkv(xb, w_qkv, cos_t, sin_t, b, s)
    kmean = kmean.reshape(b, s // MOBA_BLOCK, N_HEADS, HEAD_DIM).transpose(0, 2, 1, 3)

    zeros = jnp.zeros((DECAY_LORA, WIDTH), F32)
    w_lora = jnp.concatenate([jnp.concatenate([w_decay_up, zeros], axis=1),
                              jnp.concatenate([zeros, w_aaa_up], axis=1)], axis=0).astype(BF16)
    w_g = jnp.pad(w_gate_up, ((0, GATE_LORA_PAD - GATE_LORA), (0, 0))).astype(BF16)
    y_rwkv = _rwkv(proj_rw, mu, row(w0), row(a0), w_lora, w_g, row(k_k), row(k_a), row(r_k),
                   row(lnx_g), row(lnx_b))
    y_attn = _moba(qt, kaug, vaug, kmean).transpose(0, 2, 1, 3)

    w_r = jnp.pad(w_router, ((0, 0), (0, ROUTER_PAD - N_EXPERTS)))
    b_r = jnp.pad(b_router, (0, ROUTER_PAD - N_EXPERTS), constant_values=NEG_INF).reshape(1, -1)
    h, idx_pad, gate_pad = _merge(x2, y_attn.reshape(t, WIDTH), y_rwkv.reshape(t, WIDTH), w_gate,
                                  w_attn_br.astype(BF16), w_rwkv_br.astype(BF16), w_out.astype(BF16),
                                  row(ln1_g), row(ln1_b), w_r, b_r, alpha)

    dest_pad, counts = _route(idx_pad)
    dest = dest_pad[:, :TOP_K]
    nblk = t * TOP_K // EXPERT_ROWS + N_EXPERTS
    block_expert, n_active, pad_end = _block_tables(counts[0, :N_EXPERTS], nblk)
    x_sorted = _dispatch(pad_end, dest, h, nblk * EXPERT_ROWS)
    w1p = _deinterleave(w1)
    b1p = jnp.concatenate([b1[:, 0::2], b1[:, 1::2]], axis=1)[:, None, :]
    y_sorted = _experts(block_expert, n_active, x_sorted, w1p, b1p, w2.astype(BF16), b2[:, None, :])

    tm = COMBINE_ROWS
    dest_km = dest.reshape(t // tm, tm, TOP_K).transpose(0, 2, 1).reshape(t // tm, 1, TOP_K * tm)
    out = _combine(dest_km, h, gate_pad, row(ln2_g), row(ln2_b), y_sorted, alpha)
    return out.reshape(b, s, d)


def kernel(x, w_in, mu_shift, w0, w_decay_up, a0, w_aaa_up, w_gate_up, k_k, k_a, r_k, lnx_g, lnx_b,
           w_attn_br, w_rwkv_br, w_out, ln1_g, ln1_b, w_router, b_router, w1, b1, w2, b2, ln2_g, ln2_b):
    depth = w_in.shape[0]
    alpha = (2 * depth) ** 0.25
    for l in range(depth):
        x = _layer(x, w_in[l], mu_shift[l], w0[l], w_decay_up[l], a0[l], w_aaa_up[l], w_gate_up[l],
                   k_k[l], k_a[l], r_k[l].reshape(-1), lnx_g[l], lnx_b[l], w_attn_br[l], w_rwkv_br[l],
                   w_out[l], ln1_g[l], ln1_b[l], w_router[l], b_router[l], w1[l], b1[l], w2[l], b2[l],
                   ln2_g[l], ln2_b[l], alpha)
    return x
```

```python
import functools

import jax
import jax.numpy as jnp
from jax import lax
from jax.experimental import pallas as pl
from jax.experimental.pallas import tpu as pltpu

F32 = jnp.float32
BF16 = jnp.bfloat16
HI = lax.Precision.HIGHEST

HEAD_DIM = 64
N_HEADS = 8
WIDTH = N_HEADS * HEAD_DIM
PAIR = 2 * HEAD_DIM
N_PAIRS = N_HEADS // 2
MOBA_BLOCK = 256
MOBA_TOPK = 3
KEY_GROUP = 8
VALUE_ROWS = 2 * HEAD_DIM
ROPE_THETA = 10000.0
DECAY_LORA = 64
AAA_LORA = 64
GATE_LORA = 160
GATE_LORA_PAD = 256
RW_COLS = 3 * WIDTH + DECAY_LORA + AAA_LORA + GATE_LORA
RW_COLS_PAD = 3 * WIDTH + DECAY_LORA + AAA_LORA + GATE_LORA_PAD
GN_EPS = 64e-5
LN_EPS = 1e-5
N_EXPERTS = 32
TOP_K = 4
ROUTER_PAD = 128
SWIGLU_ALPHA = 1.702
SWIGLU_LIMIT = 7.0
RWKV_CHUNK = 64
EXPERT_ROWS = 256
COMBINE_ROWS = 128
VMEM_LIMIT = 48 * 1024 * 1024

NEG_INF = float("-inf")
LOG2_E = 1.4426950408889634
MASK_BIAS = -1e30


def _nt(a, b, precision=None):
    return lax.dot_general(a, b, (((1,), (1,)), ((), ())), precision=precision,
                           preferred_element_type=F32)


def _dot(a, b, precision=None):
    return jnp.dot(a, b, precision=precision, preferred_element_type=F32)


def _matmul_kernel(x_ref, w_ref, o_ref):
    o_ref[...] = _dot(x_ref[...], w_ref[...])


def _project(xb, w, tm=512):
    t, d = xb.shape
    n = w.shape[1]
    return pl.pallas_call(
        _matmul_kernel,
        grid=(t // tm,),
        in_specs=[pl.BlockSpec((tm, d), lambda i: (i, 0)),
                  pl.BlockSpec((d, n), lambda i: (0, 0))],
        out_specs=pl.BlockSpec((tm, n), lambda i: (i, 0)),
        out_shape=jax.ShapeDtypeStruct((t, n), F32),
        compiler_params=pltpu.CompilerParams(dimension_semantics=("parallel",),
                                             vmem_limit_bytes=VMEM_LIMIT),
    )(xb, w)


def _qkv_kernel(x_ref, w_ref, cos_ref, sin_ref, qt_ref, kaug_ref, vaug_ref, km_ref, *, tm, steps_per_seq):
    acc = _dot(x_ref[...], w_ref[...])
    cos = jnp.concatenate([cos_ref[...]] * (WIDTH // PAIR), axis=1)
    sin = jnp.concatenate([sin_ref[...]] * (WIDTH // PAIR), axis=1)
    lane = lax.broadcasted_iota(jnp.int32, (tm, WIDTH), 1)
    first_half = (lane & (HEAD_DIM // 2)) == 0

    def rope(t):
        partner = jnp.where(first_half, pltpu.roll(t, WIDTH - HEAD_DIM // 2, 1),
                            pltpu.roll(t, HEAD_DIM // 2, 1))
        return t * cos + partner * sin

    q = rope(acc[:, :WIDTH]) * (HEAD_DIM ** -0.5 * LOG2_E)
    k = rope(acc[:, WIDTH:2 * WIDTH])
    v = acc[:, 2 * WIDTH:]
    for j in range(tm // MOBA_BLOCK):
        km_ref[0, j:j + 1, :] = jnp.mean(k[j * MOBA_BLOCK:(j + 1) * MOBA_BLOCK], axis=0, keepdims=True)

    lane_p = lax.broadcasted_iota(jnp.int32, (tm, PAIR), 1)
    row_p = lax.broadcasted_iota(jnp.int32, (tm, PAIR), 0)
    first_block = (pl.program_id(0) % steps_per_seq) * (tm // MOBA_BLOCK)
    row_block = lax.shift_right_logical(row_p, MOBA_BLOCK.bit_length() - 1)
    block_tag = jnp.where(lane_p - HEAD_DIM == first_block + row_block, 1.0, 0.0)
    ones = jnp.ones((VALUE_ROWS - HEAD_DIM, tm), F32)
    for pp in range(N_PAIRS):
        sl = slice(pp * PAIR, (pp + 1) * PAIR)
        q_t = jnp.transpose(q[:, sl])
        v_t = jnp.transpose(v[:, sl])
        k_p = k[:, sl]
        k_sw = pltpu.roll(k_p, HEAD_DIM, 1)
        for h, k_h in ((0, k_p), (1, k_sw)):
            rows = slice(h * HEAD_DIM, (h + 1) * HEAD_DIM)
            qt_ref[0, 2 * pp + h] = q_t[rows].astype(BF16)
            kaug_ref[0, 2 * pp + h] = jnp.where(lane_p < HEAD_DIM, k_h, block_tag).astype(BF16)
            vaug_ref[0, 2 * pp + h] = jnp.concatenate([v_t[rows], ones], axis=0).astype(BF16)


def _project_qkv(xb, w, cos_t, sin_t, batch, seq, tm=512):
    t, d = xb.shape
    steps_per_seq = seq // tm
    assert seq // MOBA_BLOCK <= HEAD_DIM, "block one-hot tags must fit the spare lanes of a head"
    tab_spec = pl.BlockSpec((tm, PAIR), lambda i: (i % steps_per_seq, 0))
    return pl.pallas_call(
        functools.partial(_qkv_kernel, tm=tm, steps_per_seq=steps_per_seq),
        grid=(t // tm,),
        in_specs=[pl.BlockSpec((tm, d), lambda i: (i, 0)),
                  pl.BlockSpec((d, 3 * WIDTH), lambda i: (0, 0)),
                  tab_spec, tab_spec],
        out_specs=[
            pl.BlockSpec((1, N_HEADS, HEAD_DIM, tm), lambda i: (i // steps_per_seq, 0, 0, i % steps_per_seq)),
            pl.BlockSpec((1, N_HEADS, tm, PAIR), lambda i: (i // steps_per_seq, 0, i % steps_per_seq, 0)),
            pl.BlockSpec((1, N_HEADS, VALUE_ROWS, tm),
                         lambda i: (i // steps_per_seq, 0, 0, i % steps_per_seq)),
            pl.BlockSpec((1, tm // MOBA_BLOCK, WIDTH), lambda i: (i, 0, 0))],
        out_shape=[jax.ShapeDtypeStruct((batch, N_HEADS, HEAD_DIM, seq), BF16),
                   jax.ShapeDtypeStruct((batch, N_HEADS, seq, PAIR), BF16),
                   jax.ShapeDtypeStruct((batch, N_HEADS, VALUE_ROWS, seq), BF16),
                   jax.ShapeDtypeStruct((t // tm, tm // MOBA_BLOCK, WIDTH), F32)],
        compiler_params=pltpu.CompilerParams(dimension_semantics=("parallel",),
                                             vmem_limit_bytes=VMEM_LIMIT),
    )(xb, w, cos_t, sin_t)


SPLIT_PARTS = 1


def _parts(x, n=SPLIT_PARTS):
    out = []
    for _ in range(n):
        piece = x.astype(BF16)
        out.append(piece)
        x = x - piece.astype(F32)
    return out


def _mm(a_parts, b_parts, f=None):
    f = f or _dot
    order = max(len(a_parts), len(b_parts))
    acc = None
    for i, a in enumerate(a_parts):
        for j, b in enumerate(b_parts):
            if i + j < order:
                term = f(a, b)
                acc = term if acc is None else acc + term
    return acc


def _softplus(z):
    return jnp.maximum(z, 0.0) + jnp.log(1.0 + jnp.exp(-jnp.abs(z)))


def _sigmoid(z):
    return 1.0 / (1.0 + jnp.exp(-z))


def _rwkv_kernel(p_ref, mu_ref, w0_ref, a0_ref, wlora_ref, wg_ref, kk_ref, ka_ref, rk_ref,
                 lng_ref, lnb_ref, y_ref, carry_ref, state_ref):
    c = RWKV_CHUNK
    nbatch = p_ref.shape[0]

    @pl.when(pl.program_id(0) == 0)
    def _():
        carry_ref[...] = jnp.zeros_like(carry_ref)
        state_ref[...] = jnp.zeros_like(state_ref)

    ri = lax.broadcasted_iota(jnp.int32, (PAIR, PAIR), 0)
    ci = lax.broadcasted_iota(jnp.int32, (PAIR, PAIR), 1)
    head_sum = jnp.where((ri // HEAD_DIM) == (ci // HEAD_DIM), 1.0, 0.0).astype(F32)
    eye = jnp.where(ri == ci, 1.0, 0.0).astype(F32)
    strict_lower = ri > ci
    lower = ri >= ci
    rc = lax.broadcasted_iota(jnp.int32, (c, c), 0)
    cc = lax.broadcasted_iota(jnp.int32, (c, c), 1)
    cumsum_mat = jnp.where(rc >= cc, 1.0, 0.0).astype(F32)
    lane_p = lax.broadcasted_iota(jnp.int32, (c, PAIR), 1)
    head0 = lane_p < HEAD_DIM

    def stack(t):
        return jnp.concatenate([jnp.where(head0, t, 0.0), jnp.where(head0, 0.0, t)], axis=0)

    head_sum_b = [head_sum.astype(BF16)]
    cumsum_b = [cumsum_mat.astype(BF16)]

    chains = []
    for bi in range(nbatch):
        p = p_ref[bi]
        row = lax.broadcasted_iota(jnp.int32, p.shape, 0)
        prev = jnp.where(row == 0, carry_ref[bi, 0:1, :], pltpu.roll(p, 1, 0))
        carry_ref[bi] = jnp.broadcast_to(p[c - 1:c, :], carry_ref.shape[1:])
        sh = p + (prev - p) * mu_ref[...]
        r = sh[:, 0:WIDTH]
        k = sh[:, WIDTH:2 * WIDTH]
        v = sh[:, 2 * WIDTH:3 * WIDTH]
        lora = sh[:, 3 * WIDTH:3 * WIDTH + PAIR]
        hg = sh[:, 3 * WIDTH + PAIR:]
        lane_l = lax.broadcasted_iota(jnp.int32, lora.shape, 1)
        lora_act = jnp.where(lane_l < DECAY_LORA, jnp.tanh(lora), lora)
        wa = _dot(lora_act.astype(BF16), wlora_ref[...])
        w_log = -_softplus(-(w0_ref[...] + wa[:, :WIDTH])) - 0.5
        logw = -jnp.exp(w_log)
        a = _sigmoid(a0_ref[...] + wa[:, WIDTH:])
        g = _dot(_sigmoid(hg).astype(BF16), wg_ref[...])
        kkn = k * kk_ref[...]
        k2 = k * (1.0 + (a - 1.0) * ka_ref[...])
        cum_all = _mm(cumsum_b, _parts(logw, 3))
        for pp in range(N_PAIRS):
            sl = slice(pp * PAIR, (pp + 1) * PAIR)
            chains.append(dict(bi=bi, pp=pp, sl=sl, r=r[:, sl], k=k2[:, sl], v=v[:, sl], a=a[:, sl],
                               kk=kkn[:, sl], lw=logw[:, sl], cum=cum_all[:, sl], g=g[:, sl]))

    for ch in chains:
        ch['ss'] = _mm(_parts(ch['kk'] * ch['kk'], 2), head_sum_b)
    for ch in chains:
        kap = ch['kk'] / jnp.maximum(jnp.sqrt(ch['ss']), 1e-12)
        cum = ch['cum']
        ch['pc'] = jnp.exp(cum[c - 1:c, :])
        inv = jnp.exp(-cum)
        rm = stack(ch['r'] * jnp.exp(cum))
        bm = stack(kap * jnp.exp(cum - ch['lw']))
        am = stack(-(kap * ch['a']) * inv)
        km = stack(ch['k'] * inv)
        ch.update(rm=rm, bm=bm, am=am, km=km, vm=stack(ch['v']))
    for ch in chains:
        ch['sb'] = _mm(_parts(jnp.concatenate([ch['bm'], ch['rm']], axis=0)),
                       _parts(jnp.concatenate([ch['am'], ch['km']], axis=0)), _nt)
    for ch in chains:
        sb = ch['sb']
        ch['la'] = jnp.where(strict_lower, sb[:2 * c, :2 * c], 0.0)
        ch['lk'] = jnp.where(strict_lower, sb[:2 * c, 2 * c:], 0.0)
        ch['ma'] = jnp.where(lower, sb[2 * c:, :2 * c], 0.0)
        ch['mk'] = jnp.where(lower, sb[2 * c:, 2 * c:], 0.0)
        ch['tinv'] = eye + ch['la']
        ch['lpow'] = ch['la']
    n = 2
    while n < c:
        for ch in chains:
            lp = _parts(ch['lpow'])
            ch['lpow'] = _mm(lp, lp)
        for ch in chains:
            ch['tinv'] = ch['tinv'] + _mm(_parts(ch['tinv']), _parts(ch['lpow']))
        n *= 2

    for ch in chains:
        ch['h0'] = state_ref[ch['bi'], ch['pp']]
        ch['rhs'] = _mm(_parts(jnp.concatenate([ch['bm'], ch['lk']], axis=1)),
                        _parts(jnp.concatenate([ch['h0'], ch['vm']], axis=0)))
    for ch in chains:
        ch['u'] = _mm(_parts(ch['tinv']), _parts(ch['rhs']))
    for ch in chains:
        yst = _mm(_parts(jnp.concatenate([ch['rm'], ch['ma'], ch['mk']], axis=1)),
                  _parts(jnp.concatenate([ch['h0'], ch['u'], ch['vm']], axis=0)))
        ch['y'] = yst[:c] + yst[c:]
    for ch in chains:
        pc = ch['pc']
        pc_col = jnp.transpose(jnp.broadcast_to(pc, (PAIR, PAIR)))
        upd = _mm(_parts(jnp.concatenate([jnp.transpose(ch['am'] * pc), jnp.transpose(ch['km'] * pc)],
                                         axis=1)),
                  _parts(jnp.concatenate([ch['u'], ch['vm']], axis=0)))
        state_ref[ch['bi'], ch['pp']] = ch['h0'] * pc_col + upd
    for ch in chains:
        ch['mean'] = _mm(_parts(ch['y'], 2), head_sum_b) * (1.0 / HEAD_DIM)
        ch['bonus'] = _mm(_parts(ch['r'] * ch['k'] * rk_ref[:, ch['sl']], 2), head_sum_b) * ch['v']
    for ch in chains:
        yc = ch['y'] - ch['mean']
        ch['yc'] = yc
        ch['var'] = _mm(_parts(yc * yc, 2), head_sum_b) * (1.0 / HEAD_DIM)
    for ch in chains:
        sl = ch['sl']
        yn = ch['yc'] * lax.rsqrt(ch['var'] + GN_EPS) * lng_ref[:, sl] + lnb_ref[:, sl]
        y_ref[ch['bi'], :, sl] = ((yn + ch['bonus']) * ch['g']).astype(y_ref.dtype)


def _rwkv(proj_rw, mu, w0, a0, wlora, wg, k_k, k_a, r_k, lnx_g, lnx_b):
    b, s, n = proj_rw.shape
    c = RWKV_CHUNK
    vec = lambda width: pl.BlockSpec((1, width), lambda ci: (0, 0))
    return pl.pallas_call(
        _rwkv_kernel,
        grid=(s // c,),
        in_specs=[pl.BlockSpec((b, c, n), lambda ci: (0, ci, 0)),
                  vec(n), vec(WIDTH), vec(WIDTH),
                  pl.BlockSpec(wlora.shape, lambda ci: (0, 0)),
                  pl.BlockSpec(wg.shape, lambda ci: (0, 0)),
                  vec(WIDTH), vec(WIDTH), vec(WIDTH), vec(WIDTH), vec(WIDTH)],
        out_specs=pl.BlockSpec((b, c, WIDTH), lambda ci: (0, ci, 0)),
        out_shape=jax.ShapeDtypeStruct((b, s, WIDTH), BF16),
        scratch_shapes=[pltpu.VMEM((b, 8, n), F32),
                        pltpu.VMEM((b, N_PAIRS, PAIR, PAIR), F32)],
        compiler_params=pltpu.CompilerParams(dimension_semantics=("arbitrary",),
                                             vmem_limit_bytes=VMEM_LIMIT),
    )(proj_rw, mu, w0, a0, wlora, wg, k_k, k_a, r_k, lnx_g, lnx_b)


def _key_group(nb):
    return max(1, min(KEY_GROUP, nb // 2))


def _moba_kernel(qt_ref, k_ref, vt_ref, km_ref, o_ref, sa_ref, sb_ref, mxa_ref, mxb_ref, *, nb):
    bs = MOBA_BLOCK
    i = pl.program_id(2)
    qt = qt_ref[0, 0]
    km = km_ref[0, 0].astype(BF16)
    blk = lax.broadcasted_iota(jnp.int32, (nb, bs), 0).astype(F32)
    gate = jnp.where(blk < i.astype(F32), _dot(km, qt), NEG_INF)
    sel = jnp.zeros((nb, bs), F32)
    for _ in range(MOBA_TOPK):
        mx = jnp.max(gate, axis=0, keepdims=True)
        hit = (gate == mx) & (mx > NEG_INF)
        idx = jnp.min(jnp.where(hit, blk, float(nb)), axis=0, keepdims=True)
        pick = blk == idx
        sel = jnp.where(pick, 1.0, sel)
        gate = jnp.where(pick, NEG_INF, gate)
    bias = jnp.where(sel > 0.0, 0.0, MASK_BIAS)
    if nb < HEAD_DIM:
        bias = jnp.concatenate([bias, jnp.zeros((HEAD_DIM - nb, bs), F32)], axis=0)
    q_sel = jnp.concatenate([qt, bias.astype(BF16)], axis=0)
    q_own = jnp.concatenate([qt, jnp.zeros((HEAD_DIM, bs), BF16)], axis=0)

    start = pl.multiple_of(i * bs, bs)
    key_i = lax.broadcasted_iota(jnp.int32, (bs, bs), 0)
    qry_i = lax.broadcasted_iota(jnp.int32, (bs, bs), 1)
    s = jnp.where(key_i <= qry_i, _dot(k_ref[0, 0, pl.ds(start, bs), :], q_own), NEG_INF)
    m = jnp.max(s, axis=0, keepdims=True)
    p = jnp.exp2(s - m)
    acc = _dot(vt_ref[0, 0, :, pl.ds(start, bs)], p.astype(BF16))

    span = _key_group(nb) * bs

    n_groups = lax.shift_right_logical(i * bs + span - 1, span.bit_length() - 1)

    def scores(g):
        off = pl.multiple_of(g * span, span)
        return _dot(k_ref[0, 0, pl.ds(off, span), :], q_sel)

    def col_max(sc):
        return jnp.max(jnp.max(sc.reshape(span // bs, bs, bs), axis=0), axis=0, keepdims=True)

    def put(s_buf, mx_buf, g):
        s_buf[...] = scores(g)
        mx_buf[...] = col_max(s_buf[...])

    def absorb(s_buf, mx_buf, g, m, acc):
        m_new = jnp.maximum(m, mx_buf[...])
        off = pl.multiple_of(g * span, span)
        p = jnp.exp2(s_buf[...] - m_new)
        acc = jnp.exp2(m - m_new) * acc + _dot(vt_ref[0, 0, :, pl.ds(off, span)], p.astype(BF16))
        return m_new, acc

    last_group = nb * bs // span - 1
    put(sa_ref, mxa_ref, 0)

    def pair(t, carry):
        put(sb_ref, mxb_ref, 2 * t + 1)
        m, acc = absorb(sa_ref, mxa_ref, 2 * t, *carry)
        put(sa_ref, mxa_ref, jnp.minimum(2 * t + 2, last_group))
        return absorb(sb_ref, mxb_ref, 2 * t + 1, m, acc)

    m, acc = lax.fori_loop(0, lax.shift_right_logical(n_groups + 1, 1), pair, (m, acc))
    out_t = acc[:HEAD_DIM] / acc[HEAD_DIM:HEAD_DIM + 1]
    o_ref[0, 0] = jnp.transpose(out_t).astype(o_ref.dtype)


def _moba(qt, kaug, vaug, kmean):
    b, nh, _, s = qt.shape
    nb = s // MOBA_BLOCK
    group = _key_group(nb)
    assert nb % (2 * group) == 0, "the attention loop needs an even number of key groups"
    return pl.pallas_call(
        functools.partial(_moba_kernel, nb=nb),
        grid=(b, nh, nb),
        in_specs=[pl.BlockSpec((1, 1, HEAD_DIM, MOBA_BLOCK), lambda bi, hi, qi: (bi, hi, 0, qi)),
                  pl.BlockSpec((1, 1, s, PAIR), lambda bi, hi, qi: (bi, hi, 0, 0)),
                  pl.BlockSpec((1, 1, VALUE_ROWS, s), lambda bi, hi, qi: (bi, hi, 0, 0)),
                  pl.BlockSpec((1, 1, nb, HEAD_DIM), lambda bi, hi, qi: (bi, hi, 0, 0))],
        out_specs=pl.BlockSpec((1, 1, MOBA_BLOCK, HEAD_DIM), lambda bi, hi, qi: (bi, hi, qi, 0)),
        out_shape=jax.ShapeDtypeStruct((b, nh, s, HEAD_DIM), BF16),
        scratch_shapes=[pltpu.VMEM((group * MOBA_BLOCK, MOBA_BLOCK), F32),
                        pltpu.VMEM((group * MOBA_BLOCK, MOBA_BLOCK), F32),
                        pltpu.VMEM((1, MOBA_BLOCK), F32),
                        pltpu.VMEM((1, MOBA_BLOCK), F32)],
        compiler_params=pltpu.CompilerParams(
            dimension_semantics=("parallel", "parallel", "arbitrary"),
            vmem_limit_bytes=VMEM_LIMIT),
    )(qt, kaug, vaug, kmean)


def _layer_norm(z, g, b):
    mu = jnp.mean(z, axis=1, keepdims=True)
    zc = z - mu
    var = jnp.mean(zc * zc, axis=1, keepdims=True)
    return zc * lax.rsqrt(var + LN_EPS) * g + b


def _merge_kernel(x_ref, ya_ref, yr_ref, wgate_ref, wab_ref, wrb_ref, wout_ref, g1_ref, b1_ref,
                  wr_hi_ref, wr_lo_ref, br_ref, h_ref, idx_ref, gate_ref, *, alpha):
    x = x_ref[...]
    d = x.shape[1]
    gates = _sigmoid(_dot(x.astype(BF16), wgate_ref[...]))
    mixed = (gates[:, :d] * _dot(ya_ref[...], wab_ref[...])
             + gates[:, d:] * _dot(yr_ref[...], wrb_ref[...]))
    h = _layer_norm(alpha * x + _dot(mixed.astype(BF16), wout_ref[...]), g1_ref[...], b1_ref[...])
    h_ref[...] = h

    logits = _mm(_parts(h, 2), [wr_hi_ref[...], wr_lo_ref[...]]) + br_ref[...]
    col = lax.broadcasted_iota(jnp.int32, logits.shape, 1).astype(F32)
    idx_out = jnp.zeros(logits.shape, F32)
    val_out = jnp.zeros(logits.shape, F32)
    top = None
    denom = None
    for t in range(TOP_K):
        mx = jnp.max(logits, axis=1, keepdims=True)
        idx = jnp.min(jnp.where(logits == mx, col, float(ROUTER_PAD)), axis=1, keepdims=True)
        if t == 0:
            top = mx
        e = jnp.exp(mx - top)
        denom = e if t == 0 else denom + e
        idx_out = jnp.where(col == float(t), idx, idx_out)
        val_out = jnp.where(col == float(t), e, val_out)
        logits = jnp.where(col == idx, NEG_INF, logits)
    idx_ref[...] = idx_out.astype(jnp.int32)
    gate_ref[...] = val_out / denom


def _merge(x2, ya, yr, wgate, wab, wrb, wout, g1, b1, wr, br, alpha, tm=256):
    wr_hi, wr_lo = _parts(wr, 2)
    t, d = x2.shape
    row = lambda width: pl.BlockSpec((tm, width), lambda i: (i, 0))
    full = lambda arr: pl.BlockSpec(arr.shape, lambda i: (0, 0))
    return pl.pallas_call(
        functools.partial(_merge_kernel, alpha=alpha),
        grid=(t // tm,),
        in_specs=[row(d), row(WIDTH), row(WIDTH), full(wgate), full(wab), full(wrb), full(wout),
                  full(g1), full(b1), full(wr_hi), full(wr_lo), full(br)],
        out_specs=[row(d), row(ROUTER_PAD), row(ROUTER_PAD)],
        out_shape=[jax.ShapeDtypeStruct((t, d), F32),
                   jax.ShapeDtypeStruct((t, ROUTER_PAD), jnp.int32),
                   jax.ShapeDtypeStruct((t, ROUTER_PAD), F32)],
        compiler_params=pltpu.CompilerParams(dimension_semantics=("parallel",),
                                             vmem_limit_bytes=VMEM_LIMIT),
    )(x2, ya, yr, wgate, wab, wrb, wout, g1, b1, wr_hi, wr_lo, br)


DEINTERLEAVE_GROUP = 256


def _deinterleave_kernel(w_ref, perm_ref, o_ref):
    g = DEINTERLEAVE_GROUP
    n = w_ref.shape[2]
    half = n // 2
    for c in range(n // g):
        res = _dot(w_ref[0, :, c * g:(c + 1) * g].astype(BF16), perm_ref[...])
        o_ref[0, :, c * g // 2:(c + 1) * g // 2] = res[:, :g // 2].astype(BF16)
        o_ref[0, :, half + c * g // 2:half + (c + 1) * g // 2] = res[:, g // 2:].astype(BF16)


def _deinterleave(w1, tr=256):
    e, d, n = w1.shape
    g = DEINTERLEAVE_GROUP
    src = jnp.arange(g)
    dst = jnp.where(src % 2 == 0, src // 2, g // 2 + src // 2)
    perm = (dst[:, None] == jnp.arange(g)[None, :]).astype(BF16)
    return pl.pallas_call(
        _deinterleave_kernel,
        grid=(e, d // tr),
        in_specs=[pl.BlockSpec((1, tr, n), lambda ei, ri: (ei, ri, 0)),
                  pl.BlockSpec((g, g), lambda ei, ri: (0, 0))],
        out_specs=pl.BlockSpec((1, tr, n), lambda ei, ri: (ei, ri, 0)),
        out_shape=jax.ShapeDtypeStruct((e, d, n), BF16),
        compiler_params=pltpu.CompilerParams(dimension_semantics=("parallel", "parallel"),
                                             vmem_limit_bytes=VMEM_LIMIT),
    )(w1, perm)

def _row_copy(src_hbm, dst_buf, sem, src_row, slot, dst_row):
    return pltpu.make_async_copy(src_hbm.at[pl.ds(src_row, 1), :],
                                 dst_buf.at[slot, pl.ds(dst_row, 1), :],
                                 sem.at[slot])


GATHER_UNROLL = 16


def _gather_rows(src_hbm, dst_buf, sem, idx_ref, slot, n_rows):
    def issue(i, _):
        for j in range(2):
            r = 2 * i + j
            _row_copy(src_hbm, dst_buf, sem, idx_ref[0, 0, r], slot, r).start(priority=j)
        return 0
    lax.fori_loop(0, n_rows // 2, issue, 0, unroll=GATHER_UNROLL // 2)


def _wait_rows(src_hbm, dst_buf, sem, slot, n_rows):
    pltpu.make_async_copy(src_hbm.at[pl.ds(0, n_rows), :], dst_buf.at[slot], sem.at[slot]).wait()


ROUTE_ROWS = 512
DISPATCH_ROWS = 256


def _route_kernel(idx_ref, dest_ref, cnt_ref, run_ref, start_ref, *, tm):
    phase = pl.program_id(0)
    i = pl.program_id(1)
    idx = idx_ref[...]
    lane = lax.broadcasted_iota(jnp.int32, idx.shape, 1)
    hot = [jnp.where(lane == idx[:, k:k + 1], 1.0, 0.0) for k in range(TOP_K)]
    cnt = hot[0] + hot[1] + hot[2] + hot[3]
    tile_total = jnp.sum(cnt, axis=0, keepdims=True)

    @pl.when((phase == 0) & (i == 0))
    def _():
        run_ref[...] = jnp.zeros_like(run_ref)

    @pl.when(phase == 0)
    def _():
        run_ref[...] += tile_total
        dest_ref[...] = jnp.zeros_like(dest_ref)

    @pl.when((phase == 1) & (i == 0))
    def _():
        counts = run_ref[...]
        padded = jnp.floor((counts + (EXPERT_ROWS - 1)) * (1.0 / EXPERT_ROWS)) * EXPERT_ROWS
        ri = lax.broadcasted_iota(jnp.int32, (ROUTER_PAD, ROUTER_PAD), 0)
        ci = lax.broadcasted_iota(jnp.int32, (ROUTER_PAD, ROUTER_PAD), 1)
        before = jnp.where(ri < ci, 1.0, 0.0).astype(BF16)
        start = _mm(_parts(jnp.broadcast_to(padded, (8, ROUTER_PAD)), 3), [before])
        start_ref[...] = start[0:1]
        cnt_ref[...] = counts
        run_ref[...] = jnp.zeros_like(run_ref)

    @pl.when(phase == 1)
    def _():
        rt = lax.broadcasted_iota(jnp.int32, (tm, tm), 0)
        ct = lax.broadcasted_iota(jnp.int32, (tm, tm), 1)
        earlier = jnp.where(ct < rt, 1.0, 0.0).astype(BF16)
        pos = start_ref[...] + run_ref[...] + _dot(earlier, cnt.astype(BF16))
        dest = jnp.zeros(idx.shape, F32)
        for k in range(TOP_K):
            d_k = jnp.sum(hot[k] * pos, axis=1, keepdims=True)
            dest = jnp.where(lane == k, d_k, dest)
            pos = pos + hot[k]
        dest_ref[...] = dest.astype(jnp.int32)
        run_ref[...] += tile_total


def _route(idx_pad):
    t = idx_pad.shape[0]
    tm = ROUTE_ROWS
    return pl.pallas_call(
        functools.partial(_route_kernel, tm=tm),
        grid=(2, t // tm),
        in_specs=[pl.BlockSpec((tm, ROUTER_PAD), lambda ph, i: (i, 0))],
        out_specs=[pl.BlockSpec((tm, ROUTER_PAD), lambda ph, i: (ph * i, 0)),
                   pl.BlockSpec((1, ROUTER_PAD), lambda ph, i: (0, 0))],
        out_shape=[jax.ShapeDtypeStruct((t, ROUTER_PAD), jnp.int32),
                   jax.ShapeDtypeStruct((1, ROUTER_PAD), F32)],
        scratch_shapes=[pltpu.VMEM((1, ROUTER_PAD), F32), pltpu.VMEM((1, ROUTER_PAD), F32)],
        compiler_params=pltpu.CompilerParams(dimension_semantics=("arbitrary", "arbitrary"),
                                             vmem_limit_bytes=VMEM_LIMIT),
    )(idx_pad)


def _dispatch_kernel(pad_end_ref, dest_ref, h_ref, xs_hbm, zero_ref, sem, *, tm):
    @pl.when(pl.program_id(0) == 0)
    def _():
        zero_ref[...] = jnp.zeros_like(zero_ref)

        def last_block(e):
            end = pad_end_ref[e]
            start = pl.multiple_of(jnp.maximum(end - EXPERT_ROWS, 0), EXPERT_ROWS)
            return pltpu.make_async_copy(zero_ref, xs_hbm.at[pl.ds(start, EXPERT_ROWS), :], sem.at[0])

        for e in range(N_EXPERTS):
            last_block(e).start()
        for e in range(N_EXPERTS):
            last_block(e).wait()

        def unused_block(b):
            start = pl.multiple_of(b * EXPERT_ROWS, EXPERT_ROWS)
            return pltpu.make_async_copy(zero_ref, xs_hbm.at[pl.ds(start, EXPERT_ROWS), :], sem.at[0])

        first_unused = lax.shift_right_logical(pad_end_ref[N_EXPERTS - 1], EXPERT_ROWS.bit_length() - 1)
        n_blocks = xs_hbm.shape[0] // EXPERT_ROWS

        def start_one(b, _):
            unused_block(b).start()
            return 0

        def wait_one(b, _):
            unused_block(b).wait()
            return 0

        lax.fori_loop(first_unused, n_blocks, start_one, 0)
        lax.fori_loop(first_unused, n_blocks, wait_one, 0)

    def issue(r, _):
        for k in range(TOP_K):
            pltpu.make_async_copy(h_ref.at[pl.ds(r, 1), :],
                                  xs_hbm.at[pl.ds(dest_ref[0, 0, TOP_K * r + k], 1), :],
                                  sem.at[0]).start(priority=k % 2)
        return 0

    lax.fori_loop(0, tm, issue, 0, unroll=4)
    for _ in range(TOP_K):
        pltpu.make_async_copy(h_ref, xs_hbm.at[pl.ds(0, tm), :], sem.at[0]).wait()


def _dispatch(pad_end, dest, h, n_rows):
    t, d = h.shape
    tm = DISPATCH_ROWS
    n = TOP_K * tm
    grid_spec = pltpu.PrefetchScalarGridSpec(
        num_scalar_prefetch=1,
        grid=(t // tm,),
        in_specs=[pl.BlockSpec((1, 1, n), lambda i, pe: (i, 0, 0), memory_space=pltpu.SMEM),
                  pl.BlockSpec((tm, d), lambda i, pe: (i, 0))],
        out_specs=pl.BlockSpec(memory_space=pl.ANY),
        scratch_shapes=[pltpu.VMEM((EXPERT_ROWS, d), F32), pltpu.SemaphoreType.DMA((1,))],
    )
    return pl.pallas_call(
        functools.partial(_dispatch_kernel, tm=tm),
        grid_spec=grid_spec,
        out_shape=jax.ShapeDtypeStruct((n_rows, d), F32),
        compiler_params=pltpu.CompilerParams(dimension_semantics=("arbitrary",),
                                             vmem_limit_bytes=VMEM_LIMIT),
    )(pad_end, dest.reshape(t // tm, 1, n), h)


def _expert_kernel(be_ref, nact_ref, x_ref, w1_ref, b1_ref, w2_ref, b2_ref, y_ref):
    del be_ref

    @pl.when(pl.program_id(0) < nact_ref[0])
    def _():
        f = w2_ref.shape[1]
        hid = _dot(x_ref[...].astype(BF16), w1_ref[0]) + b1_ref[0]
        x_glu = jnp.minimum(hid[:, :f], SWIGLU_LIMIT)
        x_lin = jnp.clip(hid[:, f:], -SWIGLU_LIMIT, SWIGLU_LIMIT)
        act = x_glu * _sigmoid(SWIGLU_ALPHA * x_glu) * (x_lin + 1.0)
        y_ref[...] = _dot(act.astype(BF16), w2_ref[0]) + b2_ref[0]

    @pl.when(pl.program_id(0) >= nact_ref[0])
    def _():
        y_ref[...] = jnp.zeros_like(y_ref)


def _experts(block_expert, n_active, x_sorted, w1p, b1p, w2b, b2):
    n_rows, d = x_sorted.shape
    rows = EXPERT_ROWS
    nblk = n_rows // rows
    f = w2b.shape[1]
    grid_spec = pltpu.PrefetchScalarGridSpec(
        num_scalar_prefetch=2,
        grid=(nblk,),
        in_specs=[
            pl.BlockSpec((rows, d), lambda b, be, na: (jnp.minimum(b, jnp.maximum(na[0] - 1, 0)), 0)),
            pl.BlockSpec((1, d, 2 * f), lambda b, be, na: (be[b], 0, 0)),
            pl.BlockSpec((1, 1, 2 * f), lambda b, be, na: (be[b], 0, 0)),
            pl.BlockSpec((1, f, d), lambda b, be, na: (be[b], 0, 0)),
            pl.BlockSpec((1, 1, d), lambda b, be, na: (be[b], 0, 0)),
        ],
        out_specs=pl.BlockSpec((rows, d), lambda b, be, na: (b, 0)),
    )
    return pl.pallas_call(
        _expert_kernel,
        grid_spec=grid_spec,
        out_shape=jax.ShapeDtypeStruct((n_rows, d), F32),
        compiler_params=pltpu.CompilerParams(dimension_semantics=("arbitrary",),
                                             vmem_limit_bytes=VMEM_LIMIT),
    )(block_expert, n_active, x_sorted, w1p, b1p, w2b, b2)


def _combine_kernel(dest_ref, dest_next_ref, h_ref, gate_ref, g2_ref, b2_ref, y_hbm, o_ref, ybuf, sem,
                    *, alpha, nsteps):
    tm = COMBINE_ROWS
    n = TOP_K * tm
    s = pl.program_id(0)
    slot = s % 2

    @pl.when(s == 0)
    def _():
        _gather_rows(y_hbm, ybuf, sem, dest_ref, 0, n)

    @pl.when(s + 1 < nsteps)
    def _():
        _gather_rows(y_hbm, ybuf, sem, dest_next_ref, 1 - slot, n)

    _wait_rows(y_hbm, ybuf, sem, slot, n)
    gates = gate_ref[...]
    moe = gates[:, 0:1] * ybuf[slot, 0:tm, :]
    for kk in range(1, TOP_K):
        moe = moe + gates[:, kk:kk + 1] * ybuf[slot, kk * tm:(kk + 1) * tm, :]
    o_ref[...] = _layer_norm(alpha * h_ref[...] + moe, g2_ref[...], b2_ref[...])


def _combine(dest_km, h, gate_pad, g2, b2, y_sorted, alpha):
    t, d = h.shape
    tm = COMBINE_ROWS
    nsteps = t // tm
    n = TOP_K * tm
    return pl.pallas_call(
        functools.partial(_combine_kernel, alpha=alpha, nsteps=nsteps),
        grid=(nsteps,),
        in_specs=[
            pl.BlockSpec((1, 1, n), lambda s: (s, 0, 0), memory_space=pltpu.SMEM),
            pl.BlockSpec((1, 1, n), lambda s: (jnp.minimum(s + 1, nsteps - 1), 0, 0),
                         memory_space=pltpu.SMEM),
            pl.BlockSpec((tm, d), lambda s: (s, 0)),
            pl.BlockSpec((tm, ROUTER_PAD), lambda s: (s, 0)),
            pl.BlockSpec((1, d), lambda s: (0, 0)),
            pl.BlockSpec((1, d), lambda s: (0, 0)),
            pl.BlockSpec(memory_space=pl.ANY),
        ],
        out_specs=pl.BlockSpec((tm, d), lambda s: (s, 0)),
        out_shape=jax.ShapeDtypeStruct((t, d), F32),
        scratch_shapes=[pltpu.VMEM((2, n, d), F32), pltpu.SemaphoreType.DMA((2,))],
        compiler_params=pltpu.CompilerParams(dimension_semantics=("arbitrary",),
                                             vmem_limit_bytes=VMEM_LIMIT),
    )(dest_km, dest_km, h, gate_pad, g2, b2, y_sorted)


def _block_tables(counts, nblk):
    rows = EXPERT_ROWS
    counts = counts.astype(jnp.int32)
    padded = (counts + rows - 1) // rows * rows
    pad_ends = jnp.cumsum(padded)
    block_start = jnp.arange(nblk, dtype=jnp.int32) * rows
    block_expert = jnp.sum((pad_ends[None, :] <= block_start[:, None]).astype(jnp.int32), axis=1)
    block_expert = jnp.minimum(block_expert, N_EXPERTS - 1).astype(jnp.int32)
    n_active = (pad_ends[-1] // rows).astype(jnp.int32).reshape(1)
    return block_expert, n_active, pad_ends.astype(jnp.int32)


def _layer(x, w_in, mu_shift, w0, w_decay_up, a0, w_aaa_up, w_gate_up, k_k, k_a, r_k, lnx_g, lnx_b,
           w_attn_br, w_rwkv_br, w_out, ln1_g, ln1_b, w_router, b_router, w1, b1, w2, b2, ln2_g, ln2_b,
           alpha):
    b, s, d = x.shape
    t = b * s
    x2 = x.reshape(t, d)
    xb = x2.astype(BF16)
    row = lambda vec: vec.reshape(1, -1)

    off_q = RW_COLS
    off_gate = off_q + 3 * WIDTH
    pad_cols = RW_COLS_PAD - RW_COLS
    w_rw = jnp.pad(w_in[:, :RW_COLS], ((0, 0), (0, pad_cols))).astype(BF16)
    mu = jnp.pad(mu_shift, (0, pad_cols)).reshape(1, -1)
    w_qkv = w_in[:, off_q:off_gate].astype(BF16)
    w_gate = w_in[:, off_gate:].astype(BF16)

    half = HEAD_DIM // 2
    inv_freq = ROPE_THETA ** (-jnp.arange(half, dtype=F32) / half)
    ang = jnp.arange(s, dtype=F32)[:, None] * inv_freq[None, :]
    cos, sin = jnp.cos(ang), jnp.sin(ang)
    cos_t = jnp.concatenate([cos, cos, cos, cos], axis=1)
    sin_t = jnp.concatenate([-sin, sin, -sin, sin], axis=1)

    proj_rw = _project(xb, w_rw).reshape(b, s, RW_COLS_PAD)
    qt, kaug, vaug, kmean = _project_qkv(xb, w_qkv, cos_t, sin_t, b, s)
    kmean = kmean.reshape(b, s // MOBA_BLOCK, N_HEADS, HEAD_DIM).transpose(0, 2, 1, 3)

    zeros = jnp.zeros((DECAY_LORA, WIDTH), F32)
    w_lora = jnp.concatenate([jnp.concatenate([w_decay_up, zeros], axis=1),
                              jnp.concatenate([zeros, w_aaa_up], axis=1)], axis=0).astype(BF16)
    w_g = jnp.pad(w_gate_up, ((0, GATE_LORA_PAD - GATE_LORA), (0, 0))).astype(BF16)
    y_rwkv = _rwkv(proj_rw, mu, row(w0), row(a0), w_lora, w_g, row(k_k), row(k_a), row(r_k),
                   row(lnx_g), row(lnx_b))
    y_attn = _moba(qt, kaug, vaug, kmean).transpose(0, 2, 1, 3)

    w_r = jnp.pad(w_router, ((0, 0), (0, ROUTER_PAD - N_EXPERTS)))
    b_r = jnp.pad(b_router, (0, ROUTER_PAD - N_EXPERTS), constant_values=NEG_INF).reshape(1, -1)
    h, idx_pad, gate_pad = _merge(x2, y_attn.reshape(t, WIDTH), y_rwkv.reshape(t, WIDTH), w_gate,
                                  w_attn_br.astype(BF16), w_rwkv_br.astype(BF16), w_out.astype(BF16),
                                  row(ln1_g), row(ln1_b), w_r, b_r, alpha)

    dest_pad, counts = _route(idx_pad)
    dest = dest_pad[:, :TOP_K]
    nblk = t * TOP_K // EXPERT_ROWS + N_EXPERTS
    block_expert, n_active, pad_end = _block_tables(counts[0, :N_EXPERTS], nblk)
    x_sorted = _dispatch(pad_end, dest, h, nblk * EXPERT_ROWS)
    w1p = _deinterleave(w1)
    b1p = jnp.concatenate([b1[:, 0::2], b1[:, 1::2]], axis=1)[:, None, :]
    y_sorted = _experts(block_expert, n_active, x_sorted, w1p, b1p, w2.astype(BF16), b2[:, None, :])

    tm = COMBINE_ROWS
    dest_km = dest.reshape(t // tm, tm, TOP_K).transpose(0, 2, 1).reshape(t // tm, 1, TOP_K * tm)
    out = _combine(dest_km, h, gate_pad, row(ln2_g), row(ln2_b), y_sorted, alpha)
    return out.reshape(b, s, d)


def kernel(x, w_in, mu_shift, w0, w_decay_up, a0, w_aaa_up, w_gate_up, k_k, k_a, r_k, lnx_g, lnx_b,
           w_attn_br, w_rwkv_br, w_out, ln1_g, ln1_b, w_router, b_router, w1, b1, w2, b2, ln2_g, ln2_b):
    depth = w_in.shape[0]
    alpha = (2 * depth) ** 0.25
    for l in range(depth):
        x = _layer(x, w_in[l], mu_shift[l], w0[l], w_decay_up[l], a0[l], w_aaa_up[l], w_gate_up[l],
                   k_k[l], k_a[l], r_k[l].reshape(-1), lnx_g[l], lnx_b[l], w_attn_br[l], w_rwkv_br[l],
                   w_out[l], ln1_g[l], ln1_b[l], w_router[l], b_router[l], w1[l], b1[l], w2[l], b2[l],
                   ln2_g[l], ln2_b[l], alpha)
    return x
```

```python
import functools

import jax
import jax.numpy as jnp
from jax import lax
from jax.experimental import pallas as pl
from jax.experimental.pallas import tpu as pltpu

F32 = jnp.float32
BF16 = jnp.bfloat16
HI = lax.Precision.HIGHEST

HEAD_DIM = 64
N_HEADS = 8
WIDTH = N_HEADS * HEAD_DIM
PAIR = 2 * HEAD_DIM
N_PAIRS = N_HEADS // 2
MOBA_BLOCK = 256
MOBA_TOPK = 3
KEY_GROUP = 8
VALUE_ROWS = 2 * HEAD_DIM
ROPE_THETA = 10000.0
DECAY_LORA = 64
AAA_LORA = 64
GATE_LORA = 160
GATE_LORA_PAD = 256
RW_COLS = 3 * WIDTH + DECAY_LORA + AAA_LORA + GATE_LORA
RW_COLS_PAD = 3 * WIDTH + DECAY_LORA + AAA_LORA + GATE_LORA_PAD
GN_EPS = 64e-5
LN_EPS = 1e-5
N_EXPERTS = 32
TOP_K = 4
ROUTER_PAD = 128
SWIGLU_ALPHA = 1.702
SWIGLU_LIMIT = 7.0
RWKV_CHUNK = 64
EXPERT_ROWS = 256
COMBINE_ROWS = 128
VMEM_LIMIT = 48 * 1024 * 1024

NEG_INF = float("-inf")
LOG2_E = 1.4426950408889634
MASK_BIAS = -1e30


def _nt(a, b, precision=None):
    return lax.dot_general(a, b, (((1,), (1,)), ((), ())), precision=precision,
                           preferred_element_type=F32)


def _dot(a, b, precision=None):
    return jnp.dot(a, b, precision=precision, preferred_element_type=F32)


def _matmul_kernel(x_ref, w_ref, o_ref):
    o_ref[...] = _dot(x_ref[...], w_ref[...])


def _project(xb, w, tm=512):
    t, d = xb.shape
    n = w.shape[1]
    return pl.pallas_call(
        _matmul_kernel,
        grid=(t // tm,),
        in_specs=[pl.BlockSpec((tm, d), lambda i: (i, 0)),
                  pl.BlockSpec((d, n), lambda i: (0, 0))],
        out_specs=pl.BlockSpec((tm, n), lambda i: (i, 0)),
        out_shape=jax.ShapeDtypeStruct((t, n), F32),
        compiler_params=pltpu.CompilerParams(dimension_semantics=("parallel",),
                                             vmem_limit_bytes=VMEM_LIMIT),
    )(xb, w)


def _qkv_kernel(x_ref, w_ref, cos_ref, sin_ref, qt_ref, kaug_ref, vaug_ref, km_ref, *, tm, steps_per_seq):
    acc = _dot(x_ref[...], w_ref[...])
    cos = jnp.concatenate([cos_ref[...]] * (WIDTH // PAIR), axis=1)
    sin = jnp.concatenate([sin_ref[...]] * (WIDTH // PAIR), axis=1)
    lane = lax.broadcasted_iota(jnp.int32, (tm, WIDTH), 1)
    first_half = (lane & (HEAD_DIM // 2)) == 0

    def rope(t):
        partner = jnp.where(first_half, pltpu.roll(t, WIDTH - HEAD_DIM // 2, 1),
                            pltpu.roll(t, HEAD_DIM // 2, 1))
        return t * cos + partner * sin

    q = rope(acc[:, :WIDTH]) * (HEAD_DIM ** -0.5 * LOG2_E)
    k = rope(acc[:, WIDTH:2 * WIDTH])
    v = acc[:, 2 * WIDTH:]
    for j in range(tm // MOBA_BLOCK):
        km_ref[0, j:j + 1, :] = jnp.mean(k[j * MOBA_BLOCK:(j + 1) * MOBA_BLOCK], axis=0, keepdims=True)

    lane_p = lax.broadcasted_iota(jnp.int32, (tm, PAIR), 1)
    row_p = lax.broadcasted_iota(jnp.int32, (tm, PAIR), 0)
    first_block = (pl.program_id(0) % steps_per_seq) * (tm // MOBA_BLOCK)
    row_block = lax.shift_right_logical(row_p, MOBA_BLOCK.bit_length() - 1)
    block_tag = jnp.where(lane_p - HEAD_DIM == first_block + row_block, 1.0, 0.0)
    ones = jnp.ones((VALUE_ROWS - HEAD_DIM, tm), F32)
    for pp in range(N_PAIRS):
        sl = slice(pp * PAIR, (pp + 1) * PAIR)
        q_t = jnp.transpose(q[:, sl])
        v_t = jnp.transpose(v[:, sl])
        k_p = k[:, sl]
        k_sw = pltpu.roll(k_p, HEAD_DIM, 1)
        for h, k_h in ((0, k_p), (1, k_sw)):
            rows = slice(h * HEAD_DIM, (h + 1) * HEAD_DIM)
            qt_ref[0, 2 * pp + h] = q_t[rows].astype(BF16)
            kaug_ref[0, 2 * pp + h] = jnp.where(lane_p < HEAD_DIM, k_h, block_tag).astype(BF16)
            vaug_ref[0, 2 * pp + h] = jnp.concatenate([v_t[rows], ones], axis=0).astype(BF16)


def _project_qkv(xb, w, cos_t, sin_t, batch, seq, tm=512):
    t, d = xb.shape
    steps_per_seq = seq // tm
    assert seq // MOBA_BLOCK <= HEAD_DIM, "block one-hot tags must fit the spare lanes of a head"
    tab_spec = pl.BlockSpec((tm, PAIR), lambda i: (i % steps_per_seq, 0))
    return pl.pallas_call(
        functools.partial(_qkv_kernel, tm=tm, steps_per_seq=steps_per_seq),
        grid=(t // tm,),
        in_specs=[pl.BlockSpec((tm, d), lambda i: (i, 0)),
                  pl.BlockSpec((d, 3 * WIDTH), lambda i: (0, 0)),
                  tab_spec, tab_spec],
        out_specs=[
            pl.BlockSpec((1, N_HEADS, HEAD_DIM, tm), lambda i: (i // steps_per_seq, 0, 0, i % steps_per_seq)),
            pl.BlockSpec((1, N_HEADS, tm, PAIR), lambda i: (i // steps_per_seq, 0, i % steps_per_seq, 0)),
            pl.BlockSpec((1, N_HEADS, VALUE_ROWS, tm),
                         lambda i: (i // steps_per_seq, 0, 0, i % steps_per_seq)),
            pl.BlockSpec((1, tm // MOBA_BLOCK, WIDTH), lambda i: (i, 0, 0))],
        out_shape=[jax.ShapeDtypeStruct((batch, N_HEADS, HEAD_DIM, seq), BF16),
                   jax.ShapeDtypeStruct((batch, N_HEADS, seq, PAIR), BF16),
                   jax.ShapeDtypeStruct((batch, N_HEADS, VALUE_ROWS, seq), BF16),
                   jax.ShapeDtypeStruct((t // tm, tm // MOBA_BLOCK, WIDTH), F32)],
        compiler_params=pltpu.CompilerParams(dimension_semantics=("parallel",),
                                             vmem_limit_bytes=VMEM_LIMIT),
    )(xb, w, cos_t, sin_t)


SPLIT_PARTS = 1


def _parts(x, n=SPLIT_PARTS):
    out = []
    for _ in range(n):
        piece = x.astype(BF16)
        out.append(piece)
        x = x - piece.astype(F32)
    return out


def _mm(a_parts, b_parts, f=None):
    f = f or _dot
    order = max(len(a_parts), len(b_parts))
    acc = None
    for i, a in enumerate(a_parts):
        for j, b in enumerate(b_parts):
            if i + j < order:
                term = f(a, b)
                acc = term if acc is None else acc + term
    return acc


def _softplus(z):
    return jnp.maximum(z, 0.0) + jnp.log(1.0 + jnp.exp(-jnp.abs(z)))


def _sigmoid(z):
    return 1.0 / (1.0 + jnp.exp(-z))


def _rwkv_kernel(p_ref, mu_ref, w0_ref, a0_ref, wlora_ref, wg_ref, kk_ref, ka_ref, rk_ref,
                 lng_ref, lnb_ref, y_ref, carry_ref, state_ref):
    c = RWKV_CHUNK
    nbatch = p_ref.shape[0]

    @pl.when(pl.program_id(0) == 0)
    def _():
        carry_ref[...] = jnp.zeros_like(carry_ref)
        state_ref[...] = jnp.zeros_like(state_ref)

    ri = lax.broadcasted_iota(jnp.int32, (PAIR, PAIR), 0)
    ci = lax.broadcasted_iota(jnp.int32, (PAIR, PAIR), 1)
    head_sum = jnp.where((ri // HEAD_DIM) == (ci // HEAD_DIM), 1.0, 0.0).astype(F32)
    eye = jnp.where(ri == ci, 1.0, 0.0).astype(F32)
    strict_lower = ri > ci
    lower = ri >= ci
    rc = lax.broadcasted_iota(jnp.int32, (c, c), 0)
    cc = lax.broadcasted_iota(jnp.int32, (c, c), 1)
    cumsum_mat = jnp.where(rc >= cc, 1.0, 0.0).astype(F32)
    lane_p = lax.broadcasted_iota(jnp.int32, (c, PAIR), 1)
    head0 = lane_p < HEAD_DIM

    def stack(t):
        return jnp.concatenate([jnp.where(head0, t, 0.0), jnp.where(head0, 0.0, t)], axis=0)

    head_sum_b = [head_sum.astype(BF16)]
    cumsum_b = [cumsum_mat.astype(BF16)]

    chains = []
    for bi in range(nbatch):
        p = p_ref[bi]
        row = lax.broadcasted_iota(jnp.int32, p.shape, 0)
        prev = jnp.where(row == 0, carry_ref[bi, 0:1, :], pltpu.roll(p, 1, 0))
        carry_ref[bi] = jnp.broadcast_to(p[c - 1:c, :], carry_ref.shape[1:])
        sh = p + (prev - p) * mu_ref[...]
        r = sh[:, 0:WIDTH]
        k = sh[:, WIDTH:2 * WIDTH]
        v = sh[:, 2 * WIDTH:3 * WIDTH]
        lora = sh[:, 3 * WIDTH:3 * WIDTH + PAIR]
        hg = sh[:, 3 * WIDTH + PAIR:]
        lane_l = lax.broadcasted_iota(jnp.int32, lora.shape, 1)
        lora_act = jnp.where(lane_l < DECAY_LORA, jnp.tanh(lora), lora)
        wa = _dot(lora_act.astype(BF16), wlora_ref[...])
        w_log = -_softplus(-(w0_ref[...] + wa[:, :WIDTH])) - 0.5
        logw = -jnp.exp(w_log)
        a = _sigmoid(a0_ref[...] + wa[:, WIDTH:])
        g = _dot(_sigmoid(hg).astype(BF16), wg_ref[...])
        kkn = k * kk_ref[...]
        k2 = k * (1.0 + (a - 1.0) * ka_ref[...])
        cum_all = _mm(cumsum_b, _parts(logw, 3))
        for pp in range(N_PAIRS):
            sl = slice(pp * PAIR, (pp + 1) * PAIR)
            chains.append(dict(bi=bi, pp=pp, sl=sl, r=r[:, sl], k=k2[:, sl], v=v[:, sl], a=a[:, sl],
                               kk=kkn[:, sl], lw=logw[:, sl], cum=cum_all[:, sl], g=g[:, sl]))

    for ch in chains:
        ch['ss'] = _mm(_parts(ch['kk'] * ch['kk'], 2), head_sum_b)
    for ch in chains:
        kap = ch['kk'] / jnp.maximum(jnp.sqrt(ch['ss']), 1e-12)
        cum = ch['cum']
        ch['pc'] = jnp.exp(cum[c - 1:c, :])
        inv = jnp.exp(-cum)
        rm = stack(ch['r'] * jnp.exp(cum))
        bm = stack(kap * jnp.exp(cum - ch['lw']))
        am = stack(-(kap * ch['a']) * inv)
        km = stack(ch['k'] * inv)
        ch.update(rm=rm, bm=bm, am=am, km=km, vm=stack(ch['v']))
    for ch in chains:
        ch['sb'] = _mm(_parts(jnp.concatenate([ch['bm'], ch['rm']], axis=0)),
                       _parts(jnp.concatenate([ch['am'], ch['km']], axis=0)), _nt)
    for ch in chains:
        sb = ch['sb']
        ch['la'] = jnp.where(strict_lower, sb[:2 * c, :2 * c], 0.0)
        ch['lk'] = jnp.where(strict_lower, sb[:2 * c, 2 * c:], 0.0)
        ch['ma'] = jnp.where(lower, sb[2 * c:, :2 * c], 0.0)
        ch['mk'] = jnp.where(lower, sb[2 * c:, 2 * c:], 0.0)
        ch['tinv'] = eye + ch['la']
        ch['lpow'] = ch['la']
    n = 2
    while n < c:
        for ch in chains:
            lp = _parts(ch['lpow'])
            ch['lpow'] = _mm(lp, lp)
        for ch in chains:
            ch['tinv'] = ch['tinv'] + _mm(_parts(ch['tinv']), _parts(ch['lpow']))
        n *= 2

    for ch in chains:
        ch['h0'] = state_ref[ch['bi'], ch['pp']]
        ch['rhs'] = _mm(_parts(jnp.concatenate([ch['bm'], ch['lk']], axis=1)),
                        _parts(jnp.concatenate([ch['h0'], ch['vm']], axis=0)))
    for ch in chains:
        ch['u'] = _mm(_parts(ch['tinv']), _parts(ch['rhs']))
    for ch in chains:
        yst = _mm(_parts(jnp.concatenate([ch['rm'], ch['ma'], ch['mk']], axis=1)),
                  _parts(jnp.concatenate([ch['h0'], ch['u'], ch['vm']], axis=0)))
        ch['y'] = yst[:c] + yst[c:]
    for ch in chains:
        pc = ch['pc']
        pc_col = jnp.transpose(jnp.broadcast_to(pc, (PAIR, PAIR)))
        upd = _mm(_parts(jnp.concatenate([jnp.transpose(ch['am'] * pc), jnp.transpose(ch['km'] * pc)],
                                         axis=1)),
                  _parts(jnp.concatenate([ch['u'], ch['vm']], axis=0)))
        state_ref[ch['bi'], ch['pp']] = ch['h0'] * pc_col + upd
    for ch in chains:
        ch['mean'] = _mm(_parts(ch['y'], 2), head_sum_b) * (1.0 / HEAD_DIM)
        ch['bonus'] = _mm(_parts(ch['r'] * ch['k'] * rk_ref[:, ch['sl']], 2), head_sum_b) * ch['v']
    for ch in chains:
        yc = ch['y'] - ch['mean']
        ch['yc'] = yc
        ch['var'] = _mm(_parts(yc * yc, 2), head_sum_b) * (1.0 / HEAD_DIM)
    for ch in chains:
        sl = ch['sl']
        yn = ch['yc'] * lax.rsqrt(ch['var'] + GN_EPS) * lng_ref[:, sl] + lnb_ref[:, sl]
        y_ref[ch['bi'], :, sl] = ((yn + ch['bonus']) * ch['g']).astype(y_ref.dtype)


def _rwkv(proj_rw, mu, w0, a0, wlora, wg, k_k, k_a, r_k, lnx_g, lnx_b):
    b, s, n = proj_rw.shape
    c = RWKV_CHUNK
    vec = lambda width: pl.BlockSpec((1, width), lambda ci: (0, 0))
    return pl.pallas_call(
        _rwkv_kernel,
        grid=(s // c,),
        in_specs=[pl.BlockSpec((b, c, n), lambda ci: (0, ci, 0)),
                  vec(n), vec(WIDTH), vec(WIDTH),
                  pl.BlockSpec(wlora.shape, lambda ci: (0, 0)),
                  pl.BlockSpec(wg.shape, lambda ci: (0, 0)),
                  vec(WIDTH), vec(WIDTH), vec(WIDTH), vec(WIDTH), vec(WIDTH)],
        out_specs=pl.BlockSpec((b, c, WIDTH), lambda ci: (0, ci, 0)),
        out_shape=jax.ShapeDtypeStruct((b, s, WIDTH), BF16),
        scratch_shapes=[pltpu.VMEM((b, 8, n), F32),
                        pltpu.VMEM((b, N_PAIRS, PAIR, PAIR), F32)],
        compiler_params=pltpu.CompilerParams(dimension_semantics=("arbitrary",),
                                             vmem_limit_bytes=VMEM_LIMIT),
    )(proj_rw, mu, w0, a0, wlora, wg, k_k, k_a, r_k, lnx_g, lnx_b)


def _key_group(nb):
    return max(1, min(KEY_GROUP, nb // 2))


def _moba_kernel(qt_ref, k_ref, vt_ref, km_ref, o_ref, sa_ref, sb_ref, mxa_ref, mxb_ref, *, nb):
    bs = MOBA_BLOCK
    i = pl.program_id(2)
    qt = qt_ref[0, 0]
    km = km_ref[0, 0].astype(BF16)
    blk = lax.broadcasted_iota(jnp.int32, (nb, bs), 0).astype(F32)
    gate = jnp.where(blk < i.astype(F32), _dot(km, qt), NEG_INF)
    sel = jnp.zeros((nb, bs), F32)
    for _ in range(MOBA_TOPK):
        mx = jnp.max(gate, axis=0, keepdims=True)
        hit = (gate == mx) & (mx > NEG_INF)
        idx = jnp.min(jnp.where(hit, blk, float(nb)), axis=0, keepdims=True)
        pick = blk == idx
        sel = jnp.where(pick, 1.0, sel)
        gate = jnp.where(pick, NEG_INF, gate)
    bias = jnp.where(sel > 0.0, 0.0, MASK_BIAS)
    if nb < HEAD_DIM:
        bias = jnp.concatenate([bias, jnp.zeros((HEAD_DIM - nb, bs), F32)], axis=0)
    q_sel = jnp.concatenate([qt, bias.astype(BF16)], axis=0)
    q_own = jnp.concatenate([qt, jnp.zeros((HEAD_DIM, bs), BF16)], axis=0)

    span = _key_group(nb) * bs

    n_groups = lax.shift_right_logical(i * bs + span - 1, span.bit_length() - 1)

    def scores(g):
        off = pl.multiple_of(g * span, span)
        return _dot(k_ref[0, 0, pl.ds(off, span), :], q_sel)

    def col_max(sc):
        return jnp.max(jnp.max(sc.reshape(span // bs, bs, bs), axis=0), axis=0, keepdims=True)

    def put(s_buf, mx_buf, g):
        s_new = scores(g)
        s_buf[...] = s_new
        mx_buf[...] = col_max(s_new)

    def absorb(s_buf, mx_buf, g, m, acc):
        m_new = jnp.maximum(m, mx_buf[...])
        off = pl.multiple_of(g * span, span)
        p = jnp.exp2(s_buf[...] - m_new)
        acc = jnp.exp2(m - m_new) * acc + _dot(vt_ref[0, 0, :, pl.ds(off, span)], p.astype(BF16))
        return m_new, acc

    start = pl.multiple_of(i * bs, bs)
    s_own = _dot(k_ref[0, 0, pl.ds(start, bs), :], q_own)
    put(sa_ref, mxa_ref, 0)

    key_i = lax.broadcasted_iota(jnp.int32, (bs, bs), 0)
    qry_i = lax.broadcasted_iota(jnp.int32, (bs, bs), 1)
    s = jnp.where(key_i <= qry_i, s_own, NEG_INF)
    m = jnp.max(s, axis=0, keepdims=True)
    p = jnp.exp2(s - m)
    acc = _dot(vt_ref[0, 0, :, pl.ds(start, bs)], p.astype(BF16))

    def pair(t, carry):
        put(sb_ref, mxb_ref, 2 * t + 1)
        m, acc = absorb(sa_ref, mxa_ref, 2 * t, *carry)
        put(sa_ref, mxa_ref, 2 * t + 2)
        return absorb(sb_ref, mxb_ref, 2 * t + 1, m, acc)

    n_pairs = lax.shift_right_logical(jnp.maximum(n_groups - 1, 0), 1)
    m, acc = lax.fori_loop(0, n_pairs, pair, (m, acc))
    last_even = 2 * n_pairs

    def tail_two(m, acc):
        put(sb_ref, mxb_ref, last_even + 1)
        m, acc = absorb(sa_ref, mxa_ref, last_even, m, acc)
        return absorb(sb_ref, mxb_ref, last_even + 1, m, acc)

    def tail_one(m, acc):
        return absorb(sa_ref, mxa_ref, last_even, m, acc)

    def tail(m, acc):
        return lax.cond(n_groups - last_even == 2, tail_two, tail_one, m, acc)

    m, acc = lax.cond(n_groups > 0, tail, lambda m, acc: (m, acc), m, acc)
    out_t = acc[:HEAD_DIM] / acc[HEAD_DIM:HEAD_DIM + 1]
    o_ref[0, 0] = jnp.transpose(out_t).astype(o_ref.dtype)


def _moba(qt, kaug, vaug, kmean):
    b, nh, _, s = qt.shape
    nb = s // MOBA_BLOCK
    group = _key_group(nb)
    assert nb % (2 * group) == 0, "the attention loop needs an even number of key groups"
    return pl.pallas_call(
        functools.partial(_moba_kernel, nb=nb),
        grid=(b, nh, nb),
        in_specs=[pl.BlockSpec((1, 1, HEAD_DIM, MOBA_BLOCK), lambda bi, hi, qi: (bi, hi, 0, qi)),
                  pl.BlockSpec((1, 1, s, PAIR), lambda bi, hi, qi: (bi, hi, 0, 0)),
                  pl.BlockSpec((1, 1, VALUE_ROWS, s), lambda bi, hi, qi: (bi, hi, 0, 0)),
                  pl.BlockSpec((1, 1, nb, HEAD_DIM), lambda bi, hi, qi: (bi, hi, 0, 0))],
        out_specs=pl.BlockSpec((1, 1, MOBA_BLOCK, HEAD_DIM), lambda bi, hi, qi: (bi, hi, qi, 0)),
        out_shape=jax.ShapeDtypeStruct((b, nh, s, HEAD_DIM), BF16),
        scratch_shapes=[pltpu.VMEM((group * MOBA_BLOCK, MOBA_BLOCK), F32),
                        pltpu.VMEM((group * MOBA_BLOCK, MOBA_BLOCK), F32),
                        pltpu.VMEM((1, MOBA_BLOCK), F32),
                        pltpu.VMEM((1, MOBA_BLOCK), F32)],
        compiler_params=pltpu.CompilerParams(
            dimension_semantics=("parallel", "parallel", "arbitrary"),
            vmem_limit_bytes=VMEM_LIMIT),
    )(qt, kaug, vaug, kmean)


def _layer_norm(z, g, b):
    mu = jnp.mean(z, axis=1, keepdims=True)
    zc = z - mu
    var = jnp.mean(zc * zc, axis=1, keepdims=True)
    return zc * lax.rsqrt(var + LN_EPS) * g + b


def _merge_kernel(x_ref, ya_ref, yr_ref, wgate_ref, wab_ref, wrb_ref, wout_ref, g1_ref, b1_ref,
                  wr_hi_ref, wr_lo_ref, br_ref, h_ref, idx_ref, gate_ref, *, alpha):
    x = x_ref[...]
    d = x.shape[1]
    gates = _sigmoid(_dot(x.astype(BF16), wgate_ref[...]))
    mixed = (gates[:, :d] * _dot(ya_ref[...], wab_ref[...])
             + gates[:, d:] * _dot(yr_ref[...], wrb_ref[...]))
    h = _layer_norm(alpha * x + _dot(mixed.astype(BF16), wout_ref[...]), g1_ref[...], b1_ref[...])
    h_ref[...] = h

    logits = _mm(_parts(h, 2), [wr_hi_ref[...], wr_lo_ref[...]]) + br_ref[...]
    col = lax.broadcasted_iota(jnp.int32, logits.shape, 1).astype(F32)
    idx_out = jnp.zeros(logits.shape, F32)
    val_out = jnp.zeros(logits.shape, F32)
    top = None
    denom = None
    for t in range(TOP_K):
        mx = jnp.max(logits, axis=1, keepdims=True)
        idx = jnp.min(jnp.where(logits == mx, col, float(ROUTER_PAD)), axis=1, keepdims=True)
        if t == 0:
            top = mx
        e = jnp.exp(mx - top)
        denom = e if t == 0 else denom + e
        idx_out = jnp.where(col == float(t), idx, idx_out)
        val_out = jnp.where(col == float(t), e, val_out)
        logits = jnp.where(col == idx, NEG_INF, logits)
    idx_ref[...] = idx_out.astype(jnp.int32)
    gate_ref[...] = val_out / denom


def _merge(x2, ya, yr, wgate, wab, wrb, wout, g1, b1, wr, br, alpha, tm=512):
    wr_hi, wr_lo = _parts(wr, 2)
    t, d = x2.shape
    row = lambda width: pl.BlockSpec((tm, width), lambda i: (i, 0))
    full = lambda arr: pl.BlockSpec(arr.shape, lambda i: (0, 0))
    return pl.pallas_call(
        functools.partial(_merge_kernel, alpha=alpha),
        grid=(t // tm,),
        in_specs=[row(d), row(WIDTH), row(WIDTH), full(wgate), full(wab), full(wrb), full(wout),
                  full(g1), full(b1), full(wr_hi), full(wr_lo), full(br)],
        out_specs=[row(d), row(ROUTER_PAD), row(ROUTER_PAD)],
        out_shape=[jax.ShapeDtypeStruct((t, d), F32),
                   jax.ShapeDtypeStruct((t, ROUTER_PAD), jnp.int32),
                   jax.ShapeDtypeStruct((t, ROUTER_PAD), F32)],
        compiler_params=pltpu.CompilerParams(dimension_semantics=("parallel",),
                                             vmem_limit_bytes=VMEM_LIMIT),
    )(x2, ya, yr, wgate, wab, wrb, wout, g1, b1, wr_hi, wr_lo, br)


DEINTERLEAVE_GROUP = 256


def _deinterleave_kernel(w_ref, perm_ref, o_ref):
    g = DEINTERLEAVE_GROUP
    n = w_ref.shape[2]
    half = n // 2
    for c in range(n // g):
        res = _dot(w_ref[0, :, c * g:(c + 1) * g].astype(BF16), perm_ref[...])
        o_ref[0, :, c * g // 2:(c + 1) * g // 2] = res[:, :g // 2].astype(BF16)
        o_ref[0, :, half + c * g // 2:half + (c + 1) * g // 2] = res[:, g // 2:].astype(BF16)


def _deinterleave(w1, tr=256):
    e, d, n = w1.shape
    g = DEINTERLEAVE_GROUP
    src = jnp.arange(g)
    dst = jnp.where(src % 2 == 0, src // 2, g // 2 + src // 2)
    perm = (dst[:, None] == jnp.arange(g)[None, :]).astype(BF16)
    return pl.pallas_call(
        _deinterleave_kernel,
        grid=(e, d // tr),
        in_specs=[pl.BlockSpec((1, tr, n), lambda ei, ri: (ei, ri, 0)),
                  pl.BlockSpec((g, g), lambda ei, ri: (0, 0))],
        out_specs=pl.BlockSpec((1, tr, n), lambda ei, ri: (ei, ri, 0)),
        out_shape=jax.ShapeDtypeStruct((e, d, n), BF16),
        compiler_params=pltpu.CompilerParams(dimension_semantics=("parallel", "parallel"),
                                             vmem_limit_bytes=VMEM_LIMIT),
    )(w1, perm)

def _row_copy(src_hbm, dst_buf, sem, src_row, slot, dst_row):
    return pltpu.make_async_copy(src_hbm.at[pl.ds(src_row, 1), :],
                                 dst_buf.at[slot, pl.ds(dst_row, 1), :],
                                 sem.at[slot])


GATHER_UNROLL = 16


def _gather_rows(src_hbm, dst_buf, sem, idx_ref, slot, n_rows):
    def issue(i, _):
        for j in range(2):
            r = 2 * i + j
            _row_copy(src_hbm, dst_buf, sem, idx_ref[0, 0, r], slot, r).start(priority=j)
        return 0
    lax.fori_loop(0, n_rows // 2, issue, 0, unroll=GATHER_UNROLL // 2)


def _wait_rows(src_hbm, dst_buf, sem, slot, n_rows):
    pltpu.make_async_copy(src_hbm.at[pl.ds(0, n_rows), :], dst_buf.at[slot], sem.at[slot]).wait()


ROUTE_ROWS = 512
DISPATCH_ROWS = 256


def _route_kernel(idx_ref, dest_ref, cnt_ref, run_ref, start_ref, *, tm):
    phase = pl.program_id(0)
    i = pl.program_id(1)
    idx = idx_ref[...]
    lane = lax.broadcasted_iota(jnp.int32, idx.shape, 1)
    hot = [jnp.where(lane == idx[:, k:k + 1], 1.0, 0.0) for k in range(TOP_K)]
    cnt = hot[0] + hot[1] + hot[2] + hot[3]
    tile_total = jnp.sum(cnt, axis=0, keepdims=True)

    @pl.when((phase == 0) & (i == 0))
    def _():
        run_ref[...] = jnp.zeros_like(run_ref)

    @pl.when(phase == 0)
    def _():
        run_ref[...] += tile_total
        dest_ref[...] = jnp.zeros_like(dest_ref)

    @pl.when((phase == 1) & (i == 0))
    def _():
        counts = run_ref[...]
        padded = jnp.floor((counts + (EXPERT_ROWS - 1)) * (1.0 / EXPERT_ROWS)) * EXPERT_ROWS
        ri = lax.broadcasted_iota(jnp.int32, (ROUTER_PAD, ROUTER_PAD), 0)
        ci = lax.broadcasted_iota(jnp.int32, (ROUTER_PAD, ROUTER_PAD), 1)
        before = jnp.where(ri < ci, 1.0, 0.0).astype(BF16)
        start = _mm(_parts(jnp.broadcast_to(padded, (8, ROUTER_PAD)), 3), [before])
        start_ref[...] = start[0:1]
        cnt_ref[...] = counts
        run_ref[...] = jnp.zeros_like(run_ref)

    @pl.when(phase == 1)
    def _():
        rt = lax.broadcasted_iota(jnp.int32, (tm, tm), 0)
        ct = lax.broadcasted_iota(jnp.int32, (tm, tm), 1)
        earlier = jnp.where(ct < rt, 1.0, 0.0).astype(BF16)
        pos = start_ref[...] + run_ref[...] + _dot(earlier, cnt.astype(BF16))
        dest = jnp.zeros(idx.shape, F32)
        for k in range(TOP_K):
            d_k = jnp.sum(hot[k] * pos, axis=1, keepdims=True)
            dest = jnp.where(lane == k, d_k, dest)
            pos = pos + hot[k]
        dest_ref[...] = dest.astype(jnp.int32)
        run_ref[...] += tile_total


def _route(idx_pad):
    t = idx_pad.shape[0]
    tm = ROUTE_ROWS
    return pl.pallas_call(
        functools.partial(_route_kernel, tm=tm),
        grid=(2, t // tm),
        in_specs=[pl.BlockSpec((tm, ROUTER_PAD), lambda ph, i: (i, 0))],
        out_specs=[pl.BlockSpec((tm, ROUTER_PAD), lambda ph, i: (ph * i, 0)),
                   pl.BlockSpec((1, ROUTER_PAD), lambda ph, i: (0, 0))],
        out_shape=[jax.ShapeDtypeStruct((t, ROUTER_PAD), jnp.int32),
                   jax.ShapeDtypeStruct((1, ROUTER_PAD), F32)],
        scratch_shapes=[pltpu.VMEM((1, ROUTER_PAD), F32), pltpu.VMEM((1, ROUTER_PAD), F32)],
        compiler_params=pltpu.CompilerParams(dimension_semantics=("arbitrary", "arbitrary"),
                                             vmem_limit_bytes=VMEM_LIMIT),
    )(idx_pad)


def _dispatch_kernel(pad_end_ref, dest_ref, h_ref, xs_hbm, zero_ref, sem, *, tm):
    @pl.when(pl.program_id(0) == 0)
    def _():
        zero_ref[...] = jnp.zeros_like(zero_ref)

        def last_block(e):
            end = pad_end_ref[e]
            start = pl.multiple_of(jnp.maximum(end - EXPERT_ROWS, 0), EXPERT_ROWS)
            return pltpu.make_async_copy(zero_ref, xs_hbm.at[pl.ds(start, EXPERT_ROWS), :], sem.at[0])

        for e in range(N_EXPERTS):
            last_block(e).start()
        for e in range(N_EXPERTS):
            last_block(e).wait()

        def unused_block(b):
            start = pl.multiple_of(b * EXPERT_ROWS, EXPERT_ROWS)
            return pltpu.make_async_copy(zero_ref, xs_hbm.at[pl.ds(start, EXPERT_ROWS), :], sem.at[0])

        first_unused = lax.shift_right_logical(pad_end_ref[N_EXPERTS - 1], EXPERT_ROWS.bit_length() - 1)
        n_blocks = xs_hbm.shape[0] // EXPERT_ROWS

        def start_one(b, _):
            unused_block(b).start()
            return 0

        def wait_one(b, _):
            unused_block(b).wait()
            return 0

        lax.fori_loop(first_unused, n_blocks, start_one, 0)
        lax.fori_loop(first_unused, n_blocks, wait_one, 0)

    def issue(r, _):
        for k in range(TOP_K):
            pltpu.make_async_copy(h_ref.at[pl.ds(r, 1), :],
                                  xs_hbm.at[pl.ds(dest_ref[0, 0, TOP_K * r + k], 1), :],
                                  sem.at[0]).start(priority=k % 2)
        return 0

    lax.fori_loop(0, tm, issue, 0, unroll=4)
    for _ in range(TOP_K):
        pltpu.make_async_copy(h_ref, xs_hbm.at[pl.ds(0, tm), :], sem.at[0]).wait()


def _dispatch(pad_end, dest, h, n_rows):
    t, d = h.shape
    tm = DISPATCH_ROWS
    n = TOP_K * tm
    grid_spec = pltpu.PrefetchScalarGridSpec(
        num_scalar_prefetch=1,
        grid=(t // tm,),
        in_specs=[pl.BlockSpec((1, 1, n), lambda i, pe: (i, 0, 0), memory_space=pltpu.SMEM),
                  pl.BlockSpec((tm, d), lambda i, pe: (i, 0))],
        out_specs=pl.BlockSpec(memory_space=pl.ANY),
        scratch_shapes=[pltpu.VMEM((EXPERT_ROWS, d), F32), pltpu.SemaphoreType.DMA((1,))],
    )
    return pl.pallas_call(
        functools.partial(_dispatch_kernel, tm=tm),
        grid_spec=grid_spec,
        out_shape=jax.ShapeDtypeStruct((n_rows, d), F32),
        compiler_params=pltpu.CompilerParams(dimension_semantics=("arbitrary",),
                                             vmem_limit_bytes=VMEM_LIMIT),
    )(pad_end, dest.reshape(t // tm, 1, n), h)


def _expert_kernel(be_ref, nact_ref, x_ref, w1_ref, b1_ref, w2_ref, b2_ref, y_ref):
    del be_ref

    @pl.when(pl.program_id(0) < nact_ref[0])
    def _():
        f = w2_ref.shape[1]
        hid = _dot(x_ref[...].astype(BF16), w1_ref[0]) + b1_ref[0]
        x_glu = jnp.minimum(hid[:, :f], SWIGLU_LIMIT)
        x_lin = jnp.clip(hid[:, f:], -SWIGLU_LIMIT, SWIGLU_LIMIT)
        act = x_glu * _sigmoid(SWIGLU_ALPHA * x_glu) * (x_lin + 1.0)
        y_ref[...] = _dot(act.astype(BF16), w2_ref[0]) + b2_ref[0]

    @pl.when(pl.program_id(0) >= nact_ref[0])
    def _():
        y_ref[...] = jnp.zeros_like(y_ref)


def _experts(block_expert, n_active, x_sorted, w1p, b1p, w2b, b2):
    n_rows, d = x_sorted.shape
    rows = EXPERT_ROWS
    nblk = n_rows // rows
    f = w2b.shape[1]
    grid_spec = pltpu.PrefetchScalarGridSpec(
        num_scalar_prefetch=2,
        grid=(nblk,),
        in_specs=[
            pl.BlockSpec((rows, d), lambda b, be, na: (jnp.minimum(b, jnp.maximum(na[0] - 1, 0)), 0)),
            pl.BlockSpec((1, d, 2 * f), lambda b, be, na: (be[b], 0, 0)),
            pl.BlockSpec((1, 1, 2 * f), lambda b, be, na: (be[b], 0, 0)),
            pl.BlockSpec((1, f, d), lambda b, be, na: (be[b], 0, 0)),
            pl.BlockSpec((1, 1, d), lambda b, be, na: (be[b], 0, 0)),
        ],
        out_specs=pl.BlockSpec((rows, d), lambda b, be, na: (b, 0)),
    )
    return pl.pallas_call(
        _expert_kernel,
        grid_spec=grid_spec,
        out_shape=jax.ShapeDtypeStruct((n_rows, d), F32),
        compiler_params=pltpu.CompilerParams(dimension_semantics=("arbitrary",),
                                             vmem_limit_bytes=VMEM_LIMIT),
    )(block_expert, n_active, x_sorted, w1p, b1p, w2b, b2)


def _combine_kernel(dest_ref, dest_next_ref, h_ref, gate_ref, g2_ref, b2_ref, y_hbm, o_ref, ybuf, sem,
                    *, alpha, nsteps):
    tm = COMBINE_ROWS
    n = TOP_K * tm
    s = pl.program_id(0)
    slot = s % 2

    @pl.when(s == 0)
    def _():
        _gather_rows(y_hbm, ybuf, sem, dest_ref, 0, n)

    @pl.when(s + 1 < nsteps)
    def _():
        _gather_rows(y_hbm, ybuf, sem, dest_next_ref, 1 - slot, n)

    _wait_rows(y_hbm, ybuf, sem, slot, n)
    gates = gate_ref[...]
    moe = gates[:, 0:1] * ybuf[slot, 0:tm, :]
    for kk in range(1, TOP_K):
        moe = moe + gates[:, kk:kk + 1] * ybuf[slot, kk * tm:(kk + 1) * tm, :]
    o_ref[...] = _layer_norm(alpha * h_ref[...] + moe, g2_ref[...], b2_ref[...])


def _combine(dest_km, h, gate_pad, g2, b2, y_sorted, alpha):
    t, d = h.shape
    tm = COMBINE_ROWS
    nsteps = t // tm
    n = TOP_K * tm
    return pl.pallas_call(
        functools.partial(_combine_kernel, alpha=alpha, nsteps=nsteps),
        grid=(nsteps,),
        in_specs=[
            pl.BlockSpec((1, 1, n), lambda s: (s, 0, 0), memory_space=pltpu.SMEM),
            pl.BlockSpec((1, 1, n), lambda s: (jnp.minimum(s + 1, nsteps - 1), 0, 0),
                         memory_space=pltpu.SMEM),
            pl.BlockSpec((tm, d), lambda s: (s, 0)),
            pl.BlockSpec((tm, ROUTER_PAD), lambda s: (s, 0)),
            pl.BlockSpec((1, d), lambda s: (0, 0)),
            pl.BlockSpec((1, d), lambda s: (0, 0)),
            pl.BlockSpec(memory_space=pl.ANY),
        ],
        out_specs=pl.BlockSpec((tm, d), lambda s: (s, 0)),
        out_shape=jax.ShapeDtypeStruct((t, d), F32),
        scratch_shapes=[pltpu.VMEM((2, n, d), F32), pltpu.SemaphoreType.DMA((2,))],
        compiler_params=pltpu.CompilerParams(dimension_semantics=("arbitrary",),
                                             vmem_limit_bytes=VMEM_LIMIT),
    )(dest_km, dest_km, h, gate_pad, g2, b2, y_sorted)


def _block_tables(counts, nblk):
    rows = EXPERT_ROWS
    counts = counts.astype(jnp.int32)
    padded = (counts + rows - 1) // rows * rows
    pad_ends = jnp.cumsum(padded)
    block_start = jnp.arange(nblk, dtype=jnp.int32) * rows
    block_expert = jnp.sum((pad_ends[None, :] <= block_start[:, None]).astype(jnp.int32), axis=1)
    block_expert = jnp.minimum(block_expert, N_EXPERTS - 1).astype(jnp.int32)
    n_active = (pad_ends[-1] // rows).astype(jnp.int32).reshape(1)
    return block_expert, n_active, pad_ends.astype(jnp.int32)


def _layer(x, w_in, mu_shift, w0, w_decay_up, a0, w_aaa_up, w_gate_up, k_k, k_a, r_k, lnx_g, lnx_b,
           w_attn_br, w_rwkv_br, w_out, ln1_g, ln1_b, w_router, b_router, w1, b1, w2, b2, ln2_g, ln2_b,
           alpha):
    b, s, d = x.shape
    t = b * s
    x2 = x.reshape(t, d)
    xb = x2.astype(BF16)
    row = lambda vec: vec.reshape(1, -1)

    off_q = RW_COLS
    off_gate = off_q + 3 * WIDTH
    pad_cols = RW_COLS_PAD - RW_COLS
    w_rw = jnp.pad(w_in[:, :RW_COLS], ((0, 0), (0, pad_cols))).astype(BF16)
    mu = jnp.pad(mu_shift, (0, pad_cols)).reshape(1, -1)
    w_qkv = w_in[:, off_q:off_gate].astype(BF16)
    w_gate = w_in[:, off_gate:].astype(BF16)

    half = HEAD_DIM // 2
    inv_freq = ROPE_THETA ** (-jnp.arange(half, dtype=F32) / half)
    ang = jnp.arange(s, dtype=F32)[:, None] * inv_freq[None, :]
    cos, sin = jnp.cos(ang), jnp.sin(ang)
    cos_t = jnp.concatenate([cos, cos, cos, cos], axis=1)
    sin_t = jnp.concatenate([-sin, sin, -sin, sin], axis=1)

    proj_rw = _project(xb, w_rw).reshape(b, s, RW_COLS_PAD)
    qt, kaug, vaug, kmean = _project_qkv(xb, w_qkv, cos_t, sin_t, b, s)
    kmean = kmean.reshape(b, s // MOBA_BLOCK, N_HEADS, HEAD_DIM).transpose(0, 2, 1, 3)

    zeros = jnp.zeros((DECAY_LORA, WIDTH), F32)
    w_lora = jnp.concatenate([jnp.concatenate([w_decay_up, zeros], axis=1),
                              jnp.concatenate([zeros, w_aaa_up], axis=1)], axis=0).astype(BF16)
    w_g = jnp.pad(w_gate_up, ((0, GATE_LORA_PAD - GATE_LORA), (0, 0))).astype(BF16)
    y_rwkv = _rwkv(proj_rw, mu, row(w0), row(a0), w_lora, w_g, row(k_k), row(k_a), row(r_k),
                   row(lnx_g), row(lnx_b))
    y_attn = _moba(qt, kaug, vaug, kmean).transpose(0, 2, 1, 3)

    w_r = jnp.pad(w_router, ((0, 0), (0, ROUTER_PAD - N_EXPERTS)))
    b_r = jnp.pad(b_router, (0, ROUTER_PAD - N_EXPERTS), constant_values=NEG_INF).reshape(1, -1)
    h, idx_pad, gate_pad = _merge(x2, y_attn.reshape(t, WIDTH), y_rwkv.reshape(t, WIDTH), w_gate,
                                  w_attn_br.astype(BF16), w_rwkv_br.astype(BF16), w_out.astype(BF16),
                                  row(ln1_g), row(ln1_b), w_r, b_r, alpha)

    dest_pad, counts = _route(idx_pad)
    dest = dest_pad[:, :TOP_K]
    nblk = t * TOP_K // EXPERT_ROWS + N_EXPERTS
    block_expert, n_active, pad_end = _block_tables(counts[0, :N_EXPERTS], nblk)
    x_sorted = _dispatch(pad_end, dest, h, nblk * EXPERT_ROWS)
    w1p = _deinterleave(w1)
    b1p = jnp.concatenate([b1[:, 0::2], b1[:, 1::2]], axis=1)[:, None, :]
    y_sorted = _experts(block_expert, n_active, x_sorted, w1p, b1p, w2.astype(BF16), b2[:, None, :])

    tm = COMBINE_ROWS
    dest_km = dest.reshape(t // tm, tm, TOP_K).transpose(0, 2, 1).reshape(t // tm, 1, TOP_K * tm)
    out = _combine(dest_km, h, gate_pad, row(ln2_g), row(ln2_b), y_sorted, alpha)
    return out.reshape(b, s, d)


def kernel(x, w_in, mu_shift, w0, w_decay_up, a0, w_aaa_up, w_gate_up, k_k, k_a, r_k, lnx_g, lnx_b,
           w_attn_br, w_rwkv_br, w_out, ln1_g, ln1_b, w_router, b_router, w1, b1, w2, b2, ln2_g, ln2_b):
    depth = w_in.shape[0]
    alpha = (2 * depth) ** 0.25
    for l in range(depth):
        x = _layer(x, w_in[l], mu_shift[l], w0[l], w_decay_up[l], a0[l], w_aaa_up[l], w_gate_up[l],
                   k_k[l], k_a[l], r_k[l].reshape(-1), lnx_g[l], lnx_b[l], w_attn_br[l], w_rwkv_br[l],
                   w_out[l], ln1_g[l], ln1_b[l], w_router[l], b_router[l], w1[l], b1[l], w2[l], b2[l],
                   ln2_g[l], ln2_b[l], alpha)
    return x
```

```python
import functools

import jax
import jax.numpy as jnp
from jax import lax
from jax.experimental import pallas as pl
from jax.experimental.pallas import tpu as pltpu

F32 = jnp.float32
BF16 = jnp.bfloat16
HI = lax.Precision.HIGHEST

HEAD_DIM = 64
N_HEADS = 8
WIDTH = N_HEADS * HEAD_DIM
PAIR = 2 * HEAD_DIM
N_PAIRS = N_HEADS // 2
MOBA_BLOCK = 256
MOBA_TOPK = 3
KEY_GROUP = 8
VALUE_ROWS = 2 * HEAD_DIM
ROPE_THETA = 10000.0
DECAY_LORA = 64
AAA_LORA = 64
GATE_LORA = 160
GATE_LORA_PAD = 256
RW_COLS = 3 * WIDTH + DECAY_LORA + AAA_LORA + GATE_LORA
RW_COLS_PAD = 3 * WIDTH + DECAY_LORA + AAA_LORA + GATE_LORA_PAD
GN_EPS = 64e-5
LN_EPS = 1e-5
N_EXPERTS = 32
TOP_K = 4
ROUTER_PAD = 128
SWIGLU_ALPHA = 1.702
SWIGLU_LIMIT = 7.0
RWKV_CHUNK = 64
EXPERT_ROWS = 256
COMBINE_ROWS = 128
VMEM_LIMIT = 48 * 1024 * 1024

NEG_INF = float("-inf")
LOG2_E = 1.4426950408889634
MASK_BIAS = -1e30


def _nt(a, b, precision=None):
    return lax.dot_general(a, b, (((1,), (1,)), ((), ())), precision=precision,
                           preferred_element_type=F32)


def _dot(a, b, precision=None):
    return jnp.dot(a, b, precision=precision, preferred_element_type=F32)


def _matmul_kernel(x_ref, w_ref, o_ref):
    o_ref[...] = _dot(x_ref[...], w_ref[...])


def _project(xb, w, tm=512):
    t, d = xb.shape
    n = w.shape[1]
    return pl.pallas_call(
        _matmul_kernel,
        grid=(t // tm,),
        in_specs=[pl.BlockSpec((tm, d), lambda i: (i, 0)),
                  pl.BlockSpec((d, n), lambda i: (0, 0))],
        out_specs=pl.BlockSpec((tm, n), lambda i: (i, 0)),
        out_shape=jax.ShapeDtypeStruct((t, n), F32),
        compiler_params=pltpu.CompilerParams(dimension_semantics=("parallel",),
                                             vmem_limit_bytes=VMEM_LIMIT),
    )(xb, w)


def _qkv_kernel(x_ref, w_ref, cos_ref, sin_ref, qt_ref, kaug_ref, vaug_ref, km_ref, *, tm, steps_per_seq):
    acc = _dot(x_ref[...], w_ref[...])
    cos = jnp.concatenate([cos_ref[...]] * (WIDTH // PAIR), axis=1)
    sin = jnp.concatenate([sin_ref[...]] * (WIDTH // PAIR), axis=1)
    lane = lax.broadcasted_iota(jnp.int32, (tm, WIDTH), 1)
    first_half = (lane & (HEAD_DIM // 2)) == 0

    def rope(t):
        partner = jnp.where(first_half, pltpu.roll(t, WIDTH - HEAD_DIM // 2, 1),
                            pltpu.roll(t, HEAD_DIM // 2, 1))
        return t * cos + partner * sin

    q = rope(acc[:, :WIDTH]) * (HEAD_DIM ** -0.5 * LOG2_E)
    k = rope(acc[:, WIDTH:2 * WIDTH])
    v = acc[:, 2 * WIDTH:]
    for j in range(tm // MOBA_BLOCK):
        km_ref[0, j:j + 1, :] = jnp.mean(k[j * MOBA_BLOCK:(j + 1) * MOBA_BLOCK], axis=0, keepdims=True)

    lane_p = lax.broadcasted_iota(jnp.int32, (tm, PAIR), 1)
    row_p = lax.broadcasted_iota(jnp.int32, (tm, PAIR), 0)
    first_block = (pl.program_id(0) % steps_per_seq) * (tm // MOBA_BLOCK)
    row_block = lax.shift_right_logical(row_p, MOBA_BLOCK.bit_length() - 1)
    block_tag = jnp.where(lane_p - HEAD_DIM == first_block + row_block, 1.0, 0.0)
    ones = jnp.ones((VALUE_ROWS - HEAD_DIM, tm), F32)
    for pp in range(N_PAIRS):
        sl = slice(pp * PAIR, (pp + 1) * PAIR)
        q_t = jnp.transpose(q[:, sl])
        v_t = jnp.transpose(v[:, sl])
        k_p = k[:, sl]
        k_sw = pltpu.roll(k_p, HEAD_DIM, 1)
        for h, k_h in ((0, k_p), (1, k_sw)):
            rows = slice(h * HEAD_DIM, (h + 1) * HEAD_DIM)
            qt_ref[0, 2 * pp + h] = q_t[rows].astype(BF16)
            kaug_ref[0, 2 * pp + h] = jnp.where(lane_p < HEAD_DIM, k_h, block_tag).astype(BF16)
            vaug_ref[0, 2 * pp + h] = jnp.concatenate([v_t[rows], ones], axis=0).astype(BF16)


def _project_qkv(xb, w, cos_t, sin_t, batch, seq, tm=512):
    t, d = xb.shape
    steps_per_seq = seq // tm
    assert seq // MOBA_BLOCK <= HEAD_DIM, "block one-hot tags must fit the spare lanes of a head"
    tab_spec = pl.BlockSpec((tm, PAIR), lambda i: (i % steps_per_seq, 0))
    return pl.pallas_call(
        functools.partial(_qkv_kernel, tm=tm, steps_per_seq=steps_per_seq),
        grid=(t // tm,),
        in_specs=[pl.BlockSpec((tm, d), lambda i: (i, 0)),
                  pl.BlockSpec((d, 3 * WIDTH), lambda i: (0, 0)),
                  tab_spec, tab_spec],
        out_specs=[
            pl.BlockSpec((1, N_HEADS, HEAD_DIM, tm), lambda i: (i // steps_per_seq, 0, 0, i % steps_per_seq)),
            pl.BlockSpec((1, N_HEADS, tm, PAIR), lambda i: (i // steps_per_seq, 0, i % steps_per_seq, 0)),
            pl.BlockSpec((1, N_HEADS, VALUE_ROWS, tm),
                         lambda i: (i // steps_per_seq, 0, 0, i % steps_per_seq)),
            pl.BlockSpec((1, tm // MOBA_BLOCK, WIDTH), lambda i: (i, 0, 0))],
        out_shape=[jax.ShapeDtypeStruct((batch, N_HEADS, HEAD_DIM, seq), BF16),
                   jax.ShapeDtypeStruct((batch, N_HEADS, seq, PAIR), BF16),
                   jax.ShapeDtypeStruct((batch, N_HEADS, VALUE_ROWS, seq), BF16),
                   jax.ShapeDtypeStruct((t // tm, tm // MOBA_BLOCK, WIDTH), F32)],
        compiler_params=pltpu.CompilerParams(dimension_semantics=("parallel",),
                                             vmem_limit_bytes=VMEM_LIMIT),
    )(xb, w, cos_t, sin_t)


SPLIT_PARTS = 1


def _parts(x, n=SPLIT_PARTS):
    out = []
    for _ in range(n):
        piece = x.astype(BF16)
        out.append(piece)
        x = x - piece.astype(F32)
    return out


def _mm(a_parts, b_parts, f=None):
    f = f or _dot
    order = max(len(a_parts), len(b_parts))
    acc = None
    for i, a in enumerate(a_parts):
        for j, b in enumerate(b_parts):
            if i + j < order:
                term = f(a, b)
                acc = term if acc is None else acc + term
    return acc


def _softplus(z):
    return jnp.maximum(z, 0.0) + jnp.log(1.0 + jnp.exp(-jnp.abs(z)))


def _sigmoid(z):
    return 1.0 / (1.0 + jnp.exp(-z))


def _rwkv_kernel(p_ref, mu_ref, w0_ref, a0_ref, wlora_ref, wg_ref, kk_ref, ka_ref, rk_ref,
                 lng_ref, lnb_ref, y_ref, carry_ref, state_ref):
    c = RWKV_CHUNK
    nbatch = p_ref.shape[0]

    @pl.when(pl.program_id(0) == 0)
    def _():
        carry_ref[...] = jnp.zeros_like(carry_ref)
        state_ref[...] = jnp.zeros_like(state_ref)

    ri = lax.broadcasted_iota(jnp.int32, (PAIR, PAIR), 0)
    ci = lax.broadcasted_iota(jnp.int32, (PAIR, PAIR), 1)
    head_sum = jnp.where((ri // HEAD_DIM) == (ci // HEAD_DIM), 1.0, 0.0).astype(F32)
    eye = jnp.where(ri == ci, 1.0, 0.0).astype(F32)
    strict_lower = ri > ci
    lower = ri >= ci
    rc = lax.broadcasted_iota(jnp.int32, (c, c), 0)
    cc = lax.broadcasted_iota(jnp.int32, (c, c), 1)
    cumsum_mat = jnp.where(rc >= cc, 1.0, 0.0).astype(F32)
    lane_p = lax.broadcasted_iota(jnp.int32, (c, PAIR), 1)
    head0 = lane_p < HEAD_DIM

    def stack(t):
        return jnp.concatenate([jnp.where(head0, t, 0.0), jnp.where(head0, 0.0, t)], axis=0)

    head_sum_b = [head_sum.astype(BF16)]
    cumsum_b = [cumsum_mat.astype(BF16)]

    chains = []
    for bi in range(nbatch):
        p = p_ref[bi]
        row = lax.broadcasted_iota(jnp.int32, p.shape, 0)
        prev = jnp.where(row == 0, carry_ref[bi, 0:1, :], pltpu.roll(p, 1, 0))
        carry_ref[bi] = jnp.broadcast_to(p[c - 1:c, :], carry_ref.shape[1:])
        sh = p + (prev - p) * mu_ref[...]
        r = sh[:, 0:WIDTH]
        k = sh[:, WIDTH:2 * WIDTH]
        v = sh[:, 2 * WIDTH:3 * WIDTH]
        lora = sh[:, 3 * WIDTH:3 * WIDTH + PAIR]
        hg = sh[:, 3 * WIDTH + PAIR:]
        lane_l = lax.broadcasted_iota(jnp.int32, lora.shape, 1)
        lora_act = jnp.where(lane_l < DECAY_LORA, jnp.tanh(lora), lora)
        wa = _dot(lora_act.astype(BF16), wlora_ref[...])
        w_log = -_softplus(-(w0_ref[...] + wa[:, :WIDTH])) - 0.5
        logw = -jnp.exp(w_log)
        a = _sigmoid(a0_ref[...] + wa[:, WIDTH:])
        g = _dot(_sigmoid(hg).astype(BF16), wg_ref[...])
        kkn = k * kk_ref[...]
        k2 = k * (1.0 + (a - 1.0) * ka_ref[...])
        cum_all = _mm(cumsum_b, _parts(logw, 3))
        for pp in range(N_PAIRS):
            sl = slice(pp * PAIR, (pp + 1) * PAIR)
            chains.append(dict(bi=bi, pp=pp, sl=sl, r=r[:, sl], k=k2[:, sl], v=v[:, sl], a=a[:, sl],
                               kk=kkn[:, sl], lw=logw[:, sl], cum=cum_all[:, sl], g=g[:, sl]))

    for ch in chains:
        ch['ss'] = _mm(_parts(ch['kk'] * ch['kk'], 2), head_sum_b)
    for ch in chains:
        kap = ch['kk'] / jnp.maximum(jnp.sqrt(ch['ss']), 1e-12)
        cum = ch['cum']
        ch['pc'] = jnp.exp(cum[c - 1:c, :])
        inv = jnp.exp(-cum)
        rm = stack(ch['r'] * jnp.exp(cum))
        bm = stack(kap * jnp.exp(cum - ch['lw']))
        am = stack(-(kap * ch['a']) * inv)
        km = stack(ch['k'] * inv)
        ch.update(rm=rm, bm=bm, am=am, km=km, vm=stack(ch['v']))
    for ch in chains:
        ch['sb'] = _mm(_parts(jnp.concatenate([ch['bm'], ch['rm']], axis=0)),
                       _parts(jnp.concatenate([ch['am'], ch['km']], axis=0)), _nt)
    for ch in chains:
        sb = ch['sb']
        ch['la'] = jnp.where(strict_lower, sb[:2 * c, :2 * c], 0.0)
        ch['lk'] = jnp.where(strict_lower, sb[:2 * c, 2 * c:], 0.0)
        ch['ma'] = jnp.where(lower, sb[2 * c:, :2 * c], 0.0)
        ch['mk'] = jnp.where(lower, sb[2 * c:, 2 * c:], 0.0)
        ch['tinv'] = eye + ch['la']
        ch['lpow'] = ch['la']
    n = 2
    while n < c:
        for ch in chains:
            lp = _parts(ch['lpow'])
            ch['lpow'] = _mm(lp, lp)
        for ch in chains:
            ch['tinv'] = ch['tinv'] + _mm(_parts(ch['tinv']), _parts(ch['lpow']))
        n *= 2

    for ch in chains:
        ch['h0'] = state_ref[ch['bi'], ch['pp']]
        ch['rhs'] = _mm(_parts(jnp.concatenate([ch['bm'], ch['lk']], axis=1)),
                        _parts(jnp.concatenate([ch['h0'], ch['vm']], axis=0)))
    for ch in chains:
        ch['u'] = _mm(_parts(ch['tinv']), _parts(ch['rhs']))
    for ch in chains:
        yst = _mm(_parts(jnp.concatenate([ch['rm'], ch['ma'], ch['mk']], axis=1)),
                  _parts(jnp.concatenate([ch['h0'], ch['u'], ch['vm']], axis=0)))
        ch['y'] = yst[:c] + yst[c:]
    for ch in chains:
        pc = ch['pc']
        pc_col = jnp.transpose(jnp.broadcast_to(pc, (PAIR, PAIR)))
        upd = _mm(_parts(jnp.concatenate([jnp.transpose(ch['am'] * pc), jnp.transpose(ch['km'] * pc)],
                                         axis=1)),
                  _parts(jnp.concatenate([ch['u'], ch['vm']], axis=0)))
        state_ref[ch['bi'], ch['pp']] = ch['h0'] * pc_col + upd
    for ch in chains:
        ch['mean'] = _mm(_parts(ch['y'], 2), head_sum_b) * (1.0 / HEAD_DIM)
        ch['bonus'] = _mm(_parts(ch['r'] * ch['k'] * rk_ref[:, ch['sl']], 2), head_sum_b) * ch['v']
    for ch in chains:
        yc = ch['y'] - ch['mean']
        ch['yc'] = yc
        ch['var'] = _mm(_parts(yc * yc, 2), head_sum_b) * (1.0 / HEAD_DIM)
    for ch in chains:
        sl = ch['sl']
        yn = ch['yc'] * lax.rsqrt(ch['var'] + GN_EPS) * lng_ref[:, sl] + lnb_ref[:, sl]
        y_ref[ch['bi'], :, sl] = ((yn + ch['bonus']) * ch['g']).astype(y_ref.dtype)


def _rwkv(proj_rw, mu, w0, a0, wlora, wg, k_k, k_a, r_k, lnx_g, lnx_b):
    b, s, n = proj_rw.shape
    c = RWKV_CHUNK
    vec = lambda width: pl.BlockSpec((1, width), lambda ci: (0, 0))
    return pl.pallas_call(
        _rwkv_kernel,
        grid=(s // c,),
        in_specs=[pl.BlockSpec((b, c, n), lambda ci: (0, ci, 0)),
                  vec(n), vec(WIDTH), vec(WIDTH),
                  pl.BlockSpec(wlora.shape, lambda ci: (0, 0)),
                  pl.BlockSpec(wg.shape, lambda ci: (0, 0)),
                  vec(WIDTH), vec(WIDTH), vec(WIDTH), vec(WIDTH), vec(WIDTH)],
        out_specs=pl.BlockSpec((b, c, WIDTH), lambda ci: (0, ci, 0)),
        out_shape=jax.ShapeDtypeStruct((b, s, WIDTH), BF16),
        scratch_shapes=[pltpu.VMEM((b, 8, n), F32),
                        pltpu.VMEM((b, N_PAIRS, PAIR, PAIR), F32)],
        compiler_params=pltpu.CompilerParams(dimension_semantics=("arbitrary",),
                                             vmem_limit_bytes=VMEM_LIMIT),
    )(proj_rw, mu, w0, a0, wlora, wg, k_k, k_a, r_k, lnx_g, lnx_b)


MOBA_HEADS_PER_STEP = 2
MOBA_VMEM_LIMIT = 56 * 1024 * 1024


def _key_group(nb):
    return min(KEY_GROUP, nb)


def _moba_kernel(qt_ref, k_ref, vt_ref, km_ref, o_ref, sa_ref, sb_ref, mxa_ref, mxb_ref, *, nb):
    bs = MOBA_BLOCK
    heads = range(qt_ref.shape[1])
    i = pl.program_id(2)
    blk = lax.broadcasted_iota(jnp.int32, (nb, bs), 0).astype(F32)

    q_sel, q_own = [], []
    for h in heads:
        qt = qt_ref[0, h]
        km = km_ref[0, h].astype(BF16)
        gate = jnp.where(blk < i.astype(F32), _dot(km, qt), NEG_INF)
        sel = jnp.zeros((nb, bs), F32)
        for _ in range(MOBA_TOPK):
            mx = jnp.max(gate, axis=0, keepdims=True)
            hit = (gate == mx) & (mx > NEG_INF)
            idx = jnp.min(jnp.where(hit, blk, float(nb)), axis=0, keepdims=True)
            pick = blk == idx
            sel = jnp.where(pick, 1.0, sel)
            gate = jnp.where(pick, NEG_INF, gate)
        bias = jnp.where(sel > 0.0, 0.0, MASK_BIAS)
        if nb < HEAD_DIM:
            bias = jnp.concatenate([bias, jnp.zeros((HEAD_DIM - nb, bs), F32)], axis=0)
        q_sel.append(jnp.concatenate([qt, bias.astype(BF16)], axis=0))
        q_own.append(jnp.concatenate([qt, jnp.zeros((HEAD_DIM, bs), BF16)], axis=0))

    span = _key_group(nb) * bs
    n_groups = lax.shift_right_logical(i * bs + span - 1, span.bit_length() - 1)

    def col_max(sc):
        return jnp.max(jnp.max(sc.reshape(span // bs, bs, bs), axis=0), axis=0, keepdims=True)

    even, odd = (sa_ref, mxa_ref), (sb_ref, mxb_ref)

    def put(bufs, g):
        off = pl.multiple_of(g * span, span)
        for h in heads:
            s_new = _dot(k_ref[0, h, pl.ds(off, span), :], q_sel[h])
            bufs[0][h] = s_new
            bufs[1][h] = col_max(s_new)

    def absorb(bufs, g, carry):
        off = pl.multiple_of(g * span, span)
        out = []
        for h in heads:
            m, acc = carry[2 * h], carry[2 * h + 1]
            m_new = jnp.maximum(m, bufs[1][h])
            p = jnp.exp2(bufs[0][h] - m_new)
            acc = jnp.exp2(m - m_new) * acc + _dot(vt_ref[0, h, :, pl.ds(off, span)], p.astype(BF16))
            out += [m_new, acc]
        return tuple(out)

    start = pl.multiple_of(i * bs, bs)
    s_own = [_dot(k_ref[0, h, pl.ds(start, bs), :], q_own[h]) for h in heads]
    put(even, 0)

    key_i = lax.broadcasted_iota(jnp.int32, (bs, bs), 0)
    qry_i = lax.broadcasted_iota(jnp.int32, (bs, bs), 1)
    carry = []
    for h in heads:
        s = jnp.where(key_i <= qry_i, s_own[h], NEG_INF)
        m = jnp.max(s, axis=0, keepdims=True)
        p = jnp.exp2(s - m)
        carry += [m, _dot(vt_ref[0, h, :, pl.ds(start, bs)], p.astype(BF16))]
    carry = tuple(carry)

    def pair(t, carry):
        put(odd, 2 * t + 1)
        carry = absorb(even, 2 * t, carry)
        put(even, 2 * t + 2)
        return absorb(odd, 2 * t + 1, carry)

    n_pairs = lax.shift_right_logical(jnp.maximum(n_groups - 1, 0), 1)
    carry = lax.fori_loop(0, n_pairs, pair, carry)
    last_even = 2 * n_pairs

    def tail_two(*carry):
        put(odd, last_even + 1)
        return absorb(odd, last_even + 1, absorb(even, last_even, carry))

    def tail_one(*carry):
        return absorb(even, last_even, carry)

    def tail(*carry):
        return lax.cond(n_groups - last_even == 2, tail_two, tail_one, *carry)

    carry = lax.cond(n_groups > 0, tail, lambda *carry: carry, *carry)
    for h in heads:
        acc = carry[2 * h + 1]
        out_t = acc[:HEAD_DIM] / acc[HEAD_DIM:HEAD_DIM + 1]
        o_ref[0, h] = jnp.transpose(out_t).astype(o_ref.dtype)


def _moba(qt, kaug, vaug, kmean):
    b, nh, _, s = qt.shape
    nb = s // MOBA_BLOCK
    group = _key_group(nb)
    hp = MOBA_HEADS_PER_STEP
    assert nb % group == 0 and nh % hp == 0
    return pl.pallas_call(
        functools.partial(_moba_kernel, nb=nb),
        grid=(b, nh // hp, nb),
        in_specs=[pl.BlockSpec((1, hp, HEAD_DIM, MOBA_BLOCK), lambda bi, hi, qi: (bi, hi, 0, qi)),
                  pl.BlockSpec((1, hp, s, PAIR), lambda bi, hi, qi: (bi, hi, 0, 0)),
                  pl.BlockSpec((1, hp, VALUE_ROWS, s), lambda bi, hi, qi: (bi, hi, 0, 0)),
                  pl.BlockSpec((1, hp, nb, HEAD_DIM), lambda bi, hi, qi: (bi, hi, 0, 0))],
        out_specs=pl.BlockSpec((1, hp, MOBA_BLOCK, HEAD_DIM), lambda bi, hi, qi: (bi, hi, qi, 0)),
        out_shape=jax.ShapeDtypeStruct((b, nh, s, HEAD_DIM), BF16),
        scratch_shapes=[pltpu.VMEM((hp, group * MOBA_BLOCK, MOBA_BLOCK), F32),
                        pltpu.VMEM((hp, group * MOBA_BLOCK, MOBA_BLOCK), F32),
                        pltpu.VMEM((hp, 1, MOBA_BLOCK), F32),
                        pltpu.VMEM((hp, 1, MOBA_BLOCK), F32)],
        compiler_params=pltpu.CompilerParams(
            dimension_semantics=("parallel", "parallel", "arbitrary"),
            vmem_limit_bytes=MOBA_VMEM_LIMIT),
    )(qt, kaug, vaug, kmean)


def _layer_norm(z, g, b):
    mu = jnp.mean(z, axis=1, keepdims=True)
    zc = z - mu
    var = jnp.mean(zc * zc, axis=1, keepdims=True)
    return zc * lax.rsqrt(var + LN_EPS) * g + b


def _merge_kernel(x_ref, ya_ref, yr_ref, wgate_ref, wab_ref, wrb_ref, wout_ref, g1_ref, b1_ref,
                  wr_hi_ref, wr_lo_ref, br_ref, h_ref, idx_ref, gate_ref, *, alpha):
    x = x_ref[...]
    d = x.shape[1]
    gates = _sigmoid(_dot(x.astype(BF16), wgate_ref[...]))
    mixed = (gates[:, :d] * _dot(ya_ref[...], wab_ref[...])
             + gates[:, d:] * _dot(yr_ref[...], wrb_ref[...]))
    h = _layer_norm(alpha * x + _dot(mixed.astype(BF16), wout_ref[...]), g1_ref[...], b1_ref[...])
    h_ref[...] = h

    logits = _mm(_parts(h, 2), [wr_hi_ref[...], wr_lo_ref[...]]) + br_ref[...]
    col = lax.broadcasted_iota(jnp.int32, logits.shape, 1).astype(F32)
    idx_out = jnp.zeros(logits.shape, F32)
    val_out = jnp.zeros(logits.shape, F32)
    top = None
    denom = None
    for t in range(TOP_K):
        mx = jnp.max(logits, axis=1, keepdims=True)
        idx = jnp.min(jnp.where(logits == mx, col, float(ROUTER_PAD)), axis=1, keepdims=True)
        if t == 0:
            top = mx
        e = jnp.exp(mx - top)
        denom = e if t == 0 else denom + e
        idx_out = jnp.where(col == float(t), idx, idx_out)
        val_out = jnp.where(col == float(t), e, val_out)
        logits = jnp.where(col == idx, NEG_INF, logits)
    idx_ref[...] = idx_out.astype(jnp.int32)
    gate_ref[...] = val_out / denom


def _merge(x2, ya, yr, wgate, wab, wrb, wout, g1, b1, wr, br, alpha, tm=512):
    wr_hi, wr_lo = _parts(wr, 2)
    t, d = x2.shape
    row = lambda width: pl.BlockSpec((tm, width), lambda i: (i, 0))
    full = lambda arr: pl.BlockSpec(arr.shape, lambda i: (0, 0))
    return pl.pallas_call(
        functools.partial(_merge_kernel, alpha=alpha),
        grid=(t // tm,),
        in_specs=[row(d), row(WIDTH), row(WIDTH), full(wgate), full(wab), full(wrb), full(wout),
                  full(g1), full(b1), full(wr_hi), full(wr_lo), full(br)],
        out_specs=[row(d), row(ROUTER_PAD), row(ROUTER_PAD)],
        out_shape=[jax.ShapeDtypeStruct((t, d), F32),
                   jax.ShapeDtypeStruct((t, ROUTER_PAD), jnp.int32),
                   jax.ShapeDtypeStruct((t, ROUTER_PAD), F32)],
        compiler_params=pltpu.CompilerParams(dimension_semantics=("parallel",),
                                             vmem_limit_bytes=VMEM_LIMIT),
    )(x2, ya, yr, wgate, wab, wrb, wout, g1, b1, wr_hi, wr_lo, br)


DEINTERLEAVE_GROUP = 256


def _deinterleave_kernel(w_ref, perm_ref, o_ref):
    g = DEINTERLEAVE_GROUP
    n = w_ref.shape[2]
    half = n // 2
    for c in range(n // g):
        res = _dot(w_ref[0, :, c * g:(c + 1) * g].astype(BF16), perm_ref[...])
        o_ref[0, :, c * g // 2:(c + 1) * g // 2] = res[:, :g // 2].astype(BF16)
        o_ref[0, :, half + c * g // 2:half + (c + 1) * g // 2] = res[:, g // 2:].astype(BF16)


def _deinterleave(w1, tr=256):
    e, d, n = w1.shape
    g = DEINTERLEAVE_GROUP
    src = jnp.arange(g)
    dst = jnp.where(src % 2 == 0, src // 2, g // 2 + src // 2)
    perm = (dst[:, None] == jnp.arange(g)[None, :]).astype(BF16)
    return pl.pallas_call(
        _deinterleave_kernel,
        grid=(e, d // tr),
        in_specs=[pl.BlockSpec((1, tr, n), lambda ei, ri: (ei, ri, 0)),
                  pl.BlockSpec((g, g), lambda ei, ri: (0, 0))],
        out_specs=pl.BlockSpec((1, tr, n), lambda ei, ri: (ei, ri, 0)),
        out_shape=jax.ShapeDtypeStruct((e, d, n), BF16),
        compiler_params=pltpu.CompilerParams(dimension_semantics=("parallel", "parallel"),
                                             vmem_limit_bytes=VMEM_LIMIT),
    )(w1, perm)

def _row_copy(src_hbm, dst_buf, sem, src_row, slot, dst_row):
    return pltpu.make_async_copy(src_hbm.at[pl.ds(src_row, 1), :],
                                 dst_buf.at[slot, pl.ds(dst_row, 1), :],
                                 sem.at[slot])


GATHER_UNROLL = 16


def _gather_rows(src_hbm, dst_buf, sem, idx_ref, slot, n_rows):
    def issue(i, _):
        for j in range(2):
            r = 2 * i + j
            _row_copy(src_hbm, dst_buf, sem, idx_ref[0, 0, r], slot, r).start(priority=j)
        return 0
    lax.fori_loop(0, n_rows // 2, issue, 0, unroll=GATHER_UNROLL // 2)


def _wait_rows(src_hbm, dst_buf, sem, slot, n_rows):
    pltpu.make_async_copy(src_hbm.at[pl.ds(0, n_rows), :], dst_buf.at[slot], sem.at[slot]).wait()


ROUTE_ROWS = 512
DISPATCH_ROWS = 256


def _route_kernel(idx_ref, dest_ref, cnt_ref, run_ref, start_ref, *, tm):
    phase = pl.program_id(0)
    i = pl.program_id(1)
    idx = idx_ref[...]
    lane = lax.broadcasted_iota(jnp.int32, idx.shape, 1)
    hot = [jnp.where(lane == idx[:, k:k + 1], 1.0, 0.0) for k in range(TOP_K)]
    cnt = hot[0] + hot[1] + hot[2] + hot[3]
    tile_total = jnp.sum(cnt, axis=0, keepdims=True)

    @pl.when((phase == 0) & (i == 0))
    def _():
        run_ref[...] = jnp.zeros_like(run_ref)

    @pl.when(phase == 0)
    def _():
        run_ref[...] += tile_total
        dest_ref[...] = jnp.zeros_like(dest_ref)

    @pl.when((phase == 1) & (i == 0))
    def _():
        counts = run_ref[...]
        padded = jnp.floor((counts + (EXPERT_ROWS - 1)) * (1.0 / EXPERT_ROWS)) * EXPERT_ROWS
        ri = lax.broadcasted_iota(jnp.int32, (ROUTER_PAD, ROUTER_PAD), 0)
        ci = lax.broadcasted_iota(jnp.int32, (ROUTER_PAD, ROUTER_PAD), 1)
        before = jnp.where(ri < ci, 1.0, 0.0).astype(BF16)
        start = _mm(_parts(jnp.broadcast_to(padded, (8, ROUTER_PAD)), 3), [before])
        start_ref[...] = start[0:1]
        cnt_ref[...] = counts
        run_ref[...] = jnp.zeros_like(run_ref)

    @pl.when(phase == 1)
    def _():
        rt = lax.broadcasted_iota(jnp.int32, (tm, tm), 0)
        ct = lax.broadcasted_iota(jnp.int32, (tm, tm), 1)
        earlier = jnp.where(ct < rt, 1.0, 0.0).astype(BF16)
        pos = start_ref[...] + run_ref[...] + _dot(earlier, cnt.astype(BF16))
        dest = jnp.zeros(idx.shape, F32)
        for k in range(TOP_K):
            d_k = jnp.sum(hot[k] * pos, axis=1, keepdims=True)
            dest = jnp.where(lane == k, d_k, dest)
            pos = pos + hot[k]
        dest_ref[...] = dest.astype(jnp.int32)
        run_ref[...] += tile_total


def _route(idx_pad):
    t = idx_pad.shape[0]
    tm = ROUTE_ROWS
    return pl.pallas_call(
        functools.partial(_route_kernel, tm=tm),
        grid=(2, t // tm),
        in_specs=[pl.BlockSpec((tm, ROUTER_PAD), lambda ph, i: (i, 0))],
        out_specs=[pl.BlockSpec((tm, ROUTER_PAD), lambda ph, i: (ph * i, 0)),
                   pl.BlockSpec((1, ROUTER_PAD), lambda ph, i: (0, 0))],
        out_shape=[jax.ShapeDtypeStruct((t, ROUTER_PAD), jnp.int32),
                   jax.ShapeDtypeStruct((1, ROUTER_PAD), F32)],
        scratch_shapes=[pltpu.VMEM((1, ROUTER_PAD), F32), pltpu.VMEM((1, ROUTER_PAD), F32)],
        compiler_params=pltpu.CompilerParams(dimension_semantics=("arbitrary", "arbitrary"),
                                             vmem_limit_bytes=VMEM_LIMIT),
    )(idx_pad)


def _dispatch_kernel(pad_end_ref, dest_ref, h_ref, xs_hbm, zero_ref, sem, *, tm):
    @pl.when(pl.program_id(0) == 0)
    def _():
        zero_ref[...] = jnp.zeros_like(zero_ref)

        def last_block(e):
            end = pad_end_ref[e]
            start = pl.multiple_of(jnp.maximum(end - EXPERT_ROWS, 0), EXPERT_ROWS)
            return pltpu.make_async_copy(zero_ref, xs_hbm.at[pl.ds(start, EXPERT_ROWS), :], sem.at[0])

        for e in range(N_EXPERTS):
            last_block(e).start()
        for e in range(N_EXPERTS):
            last_block(e).wait()

        def unused_block(b):
            start = pl.multiple_of(b * EXPERT_ROWS, EXPERT_ROWS)
            return pltpu.make_async_copy(zero_ref, xs_hbm.at[pl.ds(start, EXPERT_ROWS), :], sem.at[0])

        first_unused = lax.shift_right_logical(pad_end_ref[N_EXPERTS - 1], EXPERT_ROWS.bit_length() - 1)
        n_blocks = xs_hbm.shape[0] // EXPERT_ROWS

        def start_one(b, _):
            unused_block(b).start()
            return 0

        def wait_one(b, _):
            unused_block(b).wait()
            return 0

        lax.fori_loop(first_unused, n_blocks, start_one, 0)
        lax.fori_loop(first_unused, n_blocks, wait_one, 0)

    def issue(r, _):
        for k in range(TOP_K):
            pltpu.make_async_copy(h_ref.at[pl.ds(r, 1), :],
                                  xs_hbm.at[pl.ds(dest_ref[0, 0, TOP_K * r + k], 1), :],
                                  sem.at[0]).start(priority=k % 2)
        return 0

    lax.fori_loop(0, tm, issue, 0, unroll=4)
    for _ in range(TOP_K):
        pltpu.make_async_copy(h_ref, xs_hbm.at[pl.ds(0, tm), :], sem.at[0]).wait()


def _dispatch(pad_end, dest, h, n_rows):
    t, d = h.shape
    tm = DISPATCH_ROWS
    n = TOP_K * tm
    grid_spec = pltpu.PrefetchScalarGridSpec(
        num_scalar_prefetch=1,
        grid=(t // tm,),
        in_specs=[pl.BlockSpec((1, 1, n), lambda i, pe: (i, 0, 0), memory_space=pltpu.SMEM),
                  pl.BlockSpec((tm, d), lambda i, pe: (i, 0))],
        out_specs=pl.BlockSpec(memory_space=pl.ANY),
        scratch_shapes=[pltpu.VMEM((EXPERT_ROWS, d), F32), pltpu.SemaphoreType.DMA((1,))],
    )
    return pl.pallas_call(
        functools.partial(_dispatch_kernel, tm=tm),
        grid_spec=grid_spec,
        out_shape=jax.ShapeDtypeStruct((n_rows, d), F32),
        compiler_params=pltpu.CompilerParams(dimension_semantics=("arbitrary",),
                                             vmem_limit_bytes=VMEM_LIMIT),
    )(pad_end, dest.reshape(t // tm, 1, n), h)


def _expert_kernel(be_ref, nact_ref, x_ref, w1_ref, b1_ref, w2_ref, b2_ref, y_ref):
    del be_ref

    @pl.when(pl.program_id(0) < nact_ref[0])
    def _():
        f = w2_ref.shape[1]
        hid = _dot(x_ref[...].astype(BF16), w1_ref[0]) + b1_ref[0]
        x_glu = jnp.minimum(hid[:, :f], SWIGLU_LIMIT)
        x_lin = jnp.clip(hid[:, f:], -SWIGLU_LIMIT, SWIGLU_LIMIT)
        act = x_glu * _sigmoid(SWIGLU_ALPHA * x_glu) * (x_lin + 1.0)
        y_ref[...] = _dot(act.astype(BF16), w2_ref[0]) + b2_ref[0]

    @pl.when(pl.program_id(0) >= nact_ref[0])
    def _():
        y_ref[...] = jnp.zeros_like(y_ref)


def _experts(block_expert, n_active, x_sorted, w1p, b1p, w2b, b2):
    n_rows, d = x_sorted.shape
    rows = EXPERT_ROWS
    nblk = n_rows // rows
    f = w2b.shape[1]
    grid_spec = pltpu.PrefetchScalarGridSpec(
        num_scalar_prefetch=2,
        grid=(nblk,),
        in_specs=[
            pl.BlockSpec((rows, d), lambda b, be, na: (jnp.minimum(b, jnp.maximum(na[0] - 1, 0)), 0)),
            pl.BlockSpec((1, d, 2 * f), lambda b, be, na: (be[b], 0, 0)),
            pl.BlockSpec((1, 1, 2 * f), lambda b, be, na: (be[b], 0, 0)),
            pl.BlockSpec((1, f, d), lambda b, be, na: (be[b], 0, 0)),
            pl.BlockSpec((1, 1, d), lambda b, be, na: (be[b], 0, 0)),
        ],
        out_specs=pl.BlockSpec((rows, d), lambda b, be, na: (b, 0)),
    )
    return pl.pallas_call(
        _expert_kernel,
        grid_spec=grid_spec,
        out_shape=jax.ShapeDtypeStruct((n_rows, d), F32),
        compiler_params=pltpu.CompilerParams(dimension_semantics=("arbitrary",),
                                             vmem_limit_bytes=VMEM_LIMIT),
    )(block_expert, n_active, x_sorted, w1p, b1p, w2b, b2)


def _combine_kernel(dest_ref, dest_next_ref, h_ref, gate_ref, g2_ref, b2_ref, y_hbm, o_ref, ybuf, sem,
                    *, alpha, nsteps):
    tm = COMBINE_ROWS
    n = TOP_K * tm
    s = pl.program_id(0)
    slot = s % 2

    @pl.when(s == 0)
    def _():
        _gather_rows(y_hbm, ybuf, sem, dest_ref, 0, n)

    @pl.when(s + 1 < nsteps)
    def _():
        _gather_rows(y_hbm, ybuf, sem, dest_next_ref, 1 - slot, n)

    _wait_rows(y_hbm, ybuf, sem, slot, n)
    gates = gate_ref[...]
    moe = gates[:, 0:1] * ybuf[slot, 0:tm, :]
    for kk in range(1, TOP_K):
        moe = moe + gates[:, kk:kk + 1] * ybuf[slot, kk * tm:(kk + 1) * tm, :]
    o_ref[...] = _layer_norm(alpha * h_ref[...] + moe, g2_ref[...], b2_ref[...])


def _combine(dest_km, h, gate_pad, g2, b2, y_sorted, alpha):
    t, d = h.shape
    tm = COMBINE_ROWS
    nsteps = t // tm
    n = TOP_K * tm
    return pl.pallas_call(
        functools.partial(_combine_kernel, alpha=alpha, nsteps=nsteps),
        grid=(nsteps,),
        in_specs=[
            pl.BlockSpec((1, 1, n), lambda s: (s, 0, 0), memory_space=pltpu.SMEM),
            pl.BlockSpec((1, 1, n), lambda s: (jnp.minimum(s + 1, nsteps - 1), 0, 0),
                         memory_space=pltpu.SMEM),
            pl.BlockSpec((tm, d), lambda s: (s, 0)),
            pl.BlockSpec((tm, ROUTER_PAD), lambda s: (s, 0)),
            pl.BlockSpec((1, d), lambda s: (0, 0)),
            pl.BlockSpec((1, d), lambda s: (0, 0)),
            pl.BlockSpec(memory_space=pl.ANY),
        ],
        out_specs=pl.BlockSpec((tm, d), lambda s: (s, 0)),
        out_shape=jax.ShapeDtypeStruct((t, d), F32),
        scratch_shapes=[pltpu.VMEM((2, n, d), F32), pltpu.SemaphoreType.DMA((2,))],
        compiler_params=pltpu.CompilerParams(dimension_semantics=("arbitrary",),
                                             vmem_limit_bytes=VMEM_LIMIT),
    )(dest_km, dest_km, h, gate_pad, g2, b2, y_sorted)


def _block_tables(counts, nblk):
    rows = EXPERT_ROWS
    counts = counts.astype(jnp.int32)
    padded = (counts + rows - 1) // rows * rows
    pad_ends = jnp.cumsum(padded)
    block_start = jnp.arange(nblk, dtype=jnp.int32) * rows
    block_expert = jnp.sum((pad_ends[None, :] <= block_start[:, None]).astype(jnp.int32), axis=1)
    block_expert = jnp.minimum(block_expert, N_EXPERTS - 1).astype(jnp.int32)
    n_active = (pad_ends[-1] // rows).astype(jnp.int32).reshape(1)
    return block_expert, n_active, pad_ends.astype(jnp.int32)


def _layer(x, w_in, mu_shift, w0, w_decay_up, a0, w_aaa_up, w_gate_up, k_k, k_a, r_k, lnx_g, lnx_b,
           w_attn_br, w_rwkv_br, w_out, ln1_g, ln1_b, w_router, b_router, w1, b1, w2, b2, ln2_g, ln2_b,
           alpha):
    b, s, d = x.shape
    t = b * s
    x2 = x.reshape(t, d)
    xb = x2.astype(BF16)
    row = lambda vec: vec.reshape(1, -1)

    off_q = RW_COLS
    off_gate = off_q + 3 * WIDTH
    pad_cols = RW_COLS_PAD - RW_COLS
    w_rw = jnp.pad(w_in[:, :RW_COLS], ((0, 0), (0, pad_cols))).astype(BF16)
    mu = jnp.pad(mu_shift, (0, pad_cols)).reshape(1, -1)
    w_qkv = w_in[:, off_q:off_gate].astype(BF16)
    w_gate = w_in[:, off_gate:].astype(BF16)

    half = HEAD_DIM // 2
    inv_freq = ROPE_THETA ** (-jnp.arange(half, dtype=F32) / half)
    ang = jnp.arange(s, dtype=F32)[:, None] * inv_freq[None, :]
    cos, sin = jnp.cos(ang), jnp.sin(ang)
    cos_t = jnp.concatenate([cos, cos, cos, cos], axis=1)
    sin_t = jnp.concatenate([-sin, sin, -sin, sin], axis=1)

    proj_rw = _project(xb, w_rw).reshape(b, s, RW_COLS_PAD)
    qt, kaug, vaug, kmean = _project_qkv(xb, w_qkv, cos_t, sin_t, b, s)
    kmean = kmean.reshape(b, s // MOBA_BLOCK, N_HEADS, HEAD_DIM).transpose(0, 2, 1, 3)

    zeros = jnp.zeros((DECAY_LORA, WIDTH), F32)
    w_lora = jnp.concatenate([jnp.concatenate([w_decay_up, zeros], axis=1),
                              jnp.concatenate([zeros, w_aaa_up], axis=1)], axis=0).astype(BF16)
    w_g = jnp.pad(w_gate_up, ((0, GATE_LORA_PAD - GATE_LORA), (0, 0))).astype(BF16)
    y_rwkv = _rwkv(proj_rw, mu, row(w0), row(a0), w_lora, w_g, row(k_k), row(k_a), row(r_k),
                   row(lnx_g), row(lnx_b))
    y_attn = _moba(qt, kaug, vaug, kmean).transpose(0, 2, 1, 3)

    w_r = jnp.pad(w_router, ((0, 0), (0, ROUTER_PAD - N_EXPERTS)))
    b_r = jnp.pad(b_router, (0, ROUTER_PAD - N_EXPERTS), constant_values=NEG_INF).reshape(1, -1)
    h, idx_pad, gate_pad = _merge(x2, y_attn.reshape(t, WIDTH), y_rwkv.reshape(t, WIDTH), w_gate,
                                  w_attn_br.astype(BF16), w_rwkv_br.astype(BF16), w_out.astype(BF16),
                                  row(ln1_g), row(ln1_b), w_r, b_r, alpha)

    dest_pad, counts = _route(idx_pad)
    dest = dest_pad[:, :TOP_K]
    nblk = t * TOP_K // EXPERT_ROWS + N_EXPERTS
    block_expert, n_active, pad_end = _block_tables(counts[0, :N_EXPERTS], nblk)
    x_sorted = _dispatch(pad_end, dest, h, nblk * EXPERT_ROWS)
    w1p = _deinterleave(w1)
    b1p = jnp.concatenate([b1[:, 0::2], b1[:, 1::2]], axis=1)[:, None, :]
    y_sorted = _experts(block_expert, n_active, x_sorted, w1p, b1p, w2.astype(BF16), b2[:, None, :])

    tm = COMBINE_ROWS
    dest_km = dest.reshape(t // tm, tm, TOP_K).transpose(0, 2, 1).reshape(t // tm, 1, TOP_K * tm)
    out = _combine(dest_km, h, gate_pad, row(ln2_g), row(ln2_b), y_sorted, alpha)
    return out.reshape(b, s, d)


def kernel(x, w_in, mu_shift, w0, w_decay_up, a0, w_aaa_up, w_gate_up, k_k, k_a, r_k, lnx_g, lnx_b,
           w_attn_br, w_rwkv_br, w_out, ln1_g, ln1_b, w_router, b_router, w1, b1, w2, b2, ln2_g, ln2_b):
    depth = w_in.shape[0]
    alpha = (2 * depth) ** 0.25
    for l in range(depth):
        x = _layer(x, w_in[l], mu_shift[l], w0[l], w_decay_up[l], a0[l], w_aaa_up[l], w_gate_up[l],
                   k_k[l], k_a[l], r_k[l].reshape(-1), lnx_g[l], lnx_b[l], w_attn_br[l], w_rwkv_br[l],
                   w_out[l], ln1_g[l], ln1_b[l], w_router[l], b_router[l], w1[l], b1[l], w2[l], b2[l],
                   ln2_g[l], ln2_b[l], alpha)
    return x
```

```python
import functools

import jax
import jax.numpy as jnp
from jax import lax
from jax.experimental import pallas as pl
from jax.experimental.pallas import tpu as pltpu

F32 = jnp.float32
BF16 = jnp.bfloat16
HI = lax.Precision.HIGHEST

HEAD_DIM = 64
N_HEADS = 8
WIDTH = N_HEADS * HEAD_DIM
PAIR = 2 * HEAD_DIM
N_PAIRS = N_HEADS // 2
MOBA_BLOCK = 256
MOBA_TOPK = 3
KEY_GROUP = 8
VALUE_ROWS = 2 * HEAD_DIM
ROPE_THETA = 10000.0
DECAY_LORA = 64
AAA_LORA = 64
GATE_LORA = 160
GATE_LORA_PAD = 256
RW_COLS = 3 * WIDTH + DECAY_LORA + AAA_LORA + GATE_LORA
RW_COLS_PAD = 3 * WIDTH + DECAY_LORA + AAA_LORA + GATE_LORA_PAD
GN_EPS = 64e-5
LN_EPS = 1e-5
N_EXPERTS = 32
TOP_K = 4
ROUTER_PAD = 128
SWIGLU_ALPHA = 1.702
SWIGLU_LIMIT = 7.0
RWKV_CHUNK = 64
EXPERT_ROWS = 256
COMBINE_ROWS = 128
VMEM_LIMIT = 48 * 1024 * 1024

NEG_INF = float("-inf")
LOG2_E = 1.4426950408889634
MASK_BIAS = -1e30


def _nt(a, b, precision=None):
    return lax.dot_general(a, b, (((1,), (1,)), ((), ())), precision=precision,
                           preferred_element_type=F32)


def _dot(a, b, precision=None):
    return jnp.dot(a, b, precision=precision, preferred_element_type=F32)


def _matmul_kernel(x_ref, w_ref, o_ref):
    o_ref[...] = _dot(x_ref[...], w_ref[...])


def _project(xb, w, tm=512):
    t, d = xb.shape
    n = w.shape[1]
    return pl.pallas_call(
        _matmul_kernel,
        grid=(t // tm,),
        in_specs=[pl.BlockSpec((tm, d), lambda i: (i, 0)),
                  pl.BlockSpec((d, n), lambda i: (0, 0))],
        out_specs=pl.BlockSpec((tm, n), lambda i: (i, 0)),
        out_shape=jax.ShapeDtypeStruct((t, n), F32),
        compiler_params=pltpu.CompilerParams(dimension_semantics=("parallel",),
                                             vmem_limit_bytes=VMEM_LIMIT),
    )(xb, w)


def _qkv_kernel(x_ref, w_ref, cos_ref, sin_ref, qt_ref, kaug_ref, vaug_ref, km_ref, *, tm, steps_per_seq):
    acc = _dot(x_ref[...], w_ref[...])
    cos = jnp.concatenate([cos_ref[...]] * (WIDTH // PAIR), axis=1)
    sin = jnp.concatenate([sin_ref[...]] * (WIDTH // PAIR), axis=1)
    lane = lax.broadcasted_iota(jnp.int32, (tm, WIDTH), 1)
    first_half = (lane & (HEAD_DIM // 2)) == 0

    def rope(t):
        partner = jnp.where(first_half, pltpu.roll(t, WIDTH - HEAD_DIM // 2, 1),
                            pltpu.roll(t, HEAD_DIM // 2, 1))
        return t * cos + partner * sin

    q = rope(acc[:, :WIDTH]) * (HEAD_DIM ** -0.5 * LOG2_E)
    k = rope(acc[:, WIDTH:2 * WIDTH])
    v = acc[:, 2 * WIDTH:]
    for j in range(tm // MOBA_BLOCK):
        km_ref[0, j:j + 1, :] = jnp.mean(k[j * MOBA_BLOCK:(j + 1) * MOBA_BLOCK], axis=0, keepdims=True)

    lane_p = lax.broadcasted_iota(jnp.int32, (tm, PAIR), 1)
    row_p = lax.broadcasted_iota(jnp.int32, (tm, PAIR), 0)
    first_block = (pl.program_id(0) % steps_per_seq) * (tm // MOBA_BLOCK)
    row_block = lax.shift_right_logical(row_p, MOBA_BLOCK.bit_length() - 1)
    block_tag = jnp.where(lane_p - HEAD_DIM == first_block + row_block, 1.0, 0.0)
    ones = jnp.ones((VALUE_ROWS - HEAD_DIM, tm), F32)
    for pp in range(N_PAIRS):
        sl = slice(pp * PAIR, (pp + 1) * PAIR)
        q_t = jnp.transpose(q[:, sl])
        v_t = jnp.transpose(v[:, sl])
        k_p = k[:, sl]
        k_sw = pltpu.roll(k_p, HEAD_DIM, 1)
        for h, k_h in ((0, k_p), (1, k_sw)):
            rows = slice(h * HEAD_DIM, (h + 1) * HEAD_DIM)
            qt_ref[0, 2 * pp + h] = q_t[rows].astype(BF16)
            kaug_ref[0, 2 * pp + h] = jnp.where(lane_p < HEAD_DIM, k_h, block_tag).astype(BF16)
            vaug_ref[0, 2 * pp + h] = jnp.concatenate([v_t[rows], ones], axis=0).astype(BF16)


def _project_qkv(xb, w, cos_t, sin_t, batch, seq, tm=512):
    t, d = xb.shape
    steps_per_seq = seq // tm
    assert seq // MOBA_BLOCK <= HEAD_DIM, "block one-hot tags must fit the spare lanes of a head"
    tab_spec = pl.BlockSpec((tm, PAIR), lambda i: (i % steps_per_seq, 0))
    return pl.pallas_call(
        functools.partial(_qkv_kernel, tm=tm, steps_per_seq=steps_per_seq),
        grid=(t // tm,),
        in_specs=[pl.BlockSpec((tm, d), lambda i: (i, 0)),
                  pl.BlockSpec((d, 3 * WIDTH), lambda i: (0, 0)),
                  tab_spec, tab_spec],
        out_specs=[
            pl.BlockSpec((1, N_HEADS, HEAD_DIM, tm), lambda i: (i // steps_per_seq, 0, 0, i % steps_per_seq)),
            pl.BlockSpec((1, N_HEADS, tm, PAIR), lambda i: (i // steps_per_seq, 0, i % steps_per_seq, 0)),
            pl.BlockSpec((1, N_HEADS, VALUE_ROWS, tm),
                         lambda i: (i // steps_per_seq, 0, 0, i % steps_per_seq)),
            pl.BlockSpec((1, tm // MOBA_BLOCK, WIDTH), lambda i: (i, 0, 0))],
        out_shape=[jax.ShapeDtypeStruct((batch, N_HEADS, HEAD_DIM, seq), BF16),
                   jax.ShapeDtypeStruct((batch, N_HEADS, seq, PAIR), BF16),
                   jax.ShapeDtypeStruct((batch, N_HEADS, VALUE_ROWS, seq), BF16),
                   jax.ShapeDtypeStruct((t // tm, tm // MOBA_BLOCK, WIDTH), F32)],
        compiler_params=pltpu.CompilerParams(dimension_semantics=("parallel",),
                                             vmem_limit_bytes=VMEM_LIMIT),
    )(xb, w, cos_t, sin_t)


SPLIT_PARTS = 1


def _parts(x, n=SPLIT_PARTS):
    out = []
    for _ in range(n):
        piece = x.astype(BF16)
        out.append(piece)
        x = x - piece.astype(F32)
    return out


def _mm(a_parts, b_parts, f=None):
    f = f or _dot
    order = max(len(a_parts), len(b_parts))
    acc = None
    for i, a in enumerate(a_parts):
        for j, b in enumerate(b_parts):
            if i + j < order:
                term = f(a, b)
                acc = term if acc is None else acc + term
    return acc


def _softplus(z):
    return jnp.maximum(z, 0.0) + jnp.log(1.0 + jnp.exp(-jnp.abs(z)))


def _sigmoid(z):
    return 1.0 / (1.0 + jnp.exp(-z))


def _rwkv_kernel(p_ref, mu_ref, w0_ref, a0_ref, wlora_ref, wg_ref, kk_ref, ka_ref, rk_ref,
                 lng_ref, lnb_ref, y_ref, carry_ref, state_ref):
    c = RWKV_CHUNK
    nbatch = p_ref.shape[0]

    @pl.when(pl.program_id(0) == 0)
    def _():
        carry_ref[...] = jnp.zeros_like(carry_ref)
        state_ref[...] = jnp.zeros_like(state_ref)

    ri = lax.broadcasted_iota(jnp.int32, (PAIR, PAIR), 0)
    ci = lax.broadcasted_iota(jnp.int32, (PAIR, PAIR), 1)
    head_sum = jnp.where((ri // HEAD_DIM) == (ci // HEAD_DIM), 1.0, 0.0).astype(F32)
    eye = jnp.where(ri == ci, 1.0, 0.0).astype(F32)
    strict_lower = ri > ci
    lower = ri >= ci
    rc = lax.broadcasted_iota(jnp.int32, (c, c), 0)
    cc = lax.broadcasted_iota(jnp.int32, (c, c), 1)
    cumsum_mat = jnp.where(rc >= cc, 1.0, 0.0).astype(F32)
    lane_p = lax.broadcasted_iota(jnp.int32, (c, PAIR), 1)
    head0 = lane_p < HEAD_DIM

    def stack(t):
        return jnp.concatenate([jnp.where(head0, t, 0.0), jnp.where(head0, 0.0, t)], axis=0)

    head_sum_b = [head_sum.astype(BF16)]
    cumsum_b = [cumsum_mat.astype(BF16)]

    chains = []
    for bi in range(nbatch):
        p = p_ref[bi]
        row = lax.broadcasted_iota(jnp.int32, p.shape, 0)
        prev = jnp.where(row == 0, carry_ref[bi, 0:1, :], pltpu.roll(p, 1, 0))
        carry_ref[bi] = jnp.broadcast_to(p[c - 1:c, :], carry_ref.shape[1:])
        sh = p + (prev - p) * mu_ref[...]
        r = sh[:, 0:WIDTH]
        k = sh[:, WIDTH:2 * WIDTH]
        v = sh[:, 2 * WIDTH:3 * WIDTH]
        lora = sh[:, 3 * WIDTH:3 * WIDTH + PAIR]
        hg = sh[:, 3 * WIDTH + PAIR:]
        lane_l = lax.broadcasted_iota(jnp.int32, lora.shape, 1)
        lora_act = jnp.where(lane_l < DECAY_LORA, jnp.tanh(lora), lora)
        wa = _dot(lora_act.astype(BF16), wlora_ref[...])
        w_log = -_softplus(-(w0_ref[...] + wa[:, :WIDTH])) - 0.5
        logw = -jnp.exp(w_log)
        a = _sigmoid(a0_ref[...] + wa[:, WIDTH:])
        g = _dot(_sigmoid(hg).astype(BF16), wg_ref[...])
        kkn = k * kk_ref[...]
        k2 = k * (1.0 + (a - 1.0) * ka_ref[...])
        cum_all = _mm(cumsum_b, _parts(logw, 3))
        for pp in range(N_PAIRS):
            sl = slice(pp * PAIR, (pp + 1) * PAIR)
            chains.append(dict(bi=bi, pp=pp, sl=sl, r=r[:, sl], k=k2[:, sl], v=v[:, sl], a=a[:, sl],
                               kk=kkn[:, sl], lw=logw[:, sl], cum=cum_all[:, sl], g=g[:, sl]))

    for ch in chains:
        ch['ss'] = _mm(_parts(ch['kk'] * ch['kk'], 2), head_sum_b)
    for ch in chains:
        kap = ch['kk'] / jnp.maximum(jnp.sqrt(ch['ss']), 1e-12)
        cum = ch['cum']
        ch['pc'] = jnp.exp(cum[c - 1:c, :])
        inv = jnp.exp(-cum)
        rm = stack(ch['r'] * jnp.exp(cum))
        bm = stack(kap * jnp.exp(cum - ch['lw']))
        am = stack(-(kap * ch['a']) * inv)
        km = stack(ch['k'] * inv)
        ch.update(rm=rm, bm=bm, am=am, km=km, vm=stack(ch['v']))
    for ch in chains:
        ch['sb'] = _mm(_parts(jnp.concatenate([ch['bm'], ch['rm']], axis=0)),
                       _parts(jnp.concatenate([ch['am'], ch['km']], axis=0)), _nt)
    for ch in chains:
        sb = ch['sb']
        ch['la'] = jnp.where(strict_lower, sb[:2 * c, :2 * c], 0.0)
        ch['lk'] = jnp.where(strict_lower, sb[:2 * c, 2 * c:], 0.0)
        ch['ma'] = jnp.where(lower, sb[2 * c:, :2 * c], 0.0)
        ch['mk'] = jnp.where(lower, sb[2 * c:, 2 * c:], 0.0)
        ch['tinv'] = eye + ch['la']
        ch['lpow'] = ch['la']
    n = 2
    while n < c:
        for ch in chains:
            lp = _parts(ch['lpow'])
            ch['lpow'] = _mm(lp, lp)
        for ch in chains:
            ch['tinv'] = ch['tinv'] + _mm(_parts(ch['tinv']), _parts(ch['lpow']))
        n *= 2

    for ch in chains:
        ch['h0'] = state_ref[ch['bi'], ch['pp']]
        ch['rhs'] = _mm(_parts(jnp.concatenate([ch['bm'], ch['lk']], axis=1)),
                        _parts(jnp.concatenate([ch['h0'], ch['vm']], axis=0)))
    for ch in chains:
        ch['u'] = _mm(_parts(ch['tinv']), _parts(ch['rhs']))
    for ch in chains:
        yst = _mm(_parts(jnp.concatenate([ch['rm'], ch['ma'], ch['mk']], axis=1)),
                  _parts(jnp.concatenate([ch['h0'], ch['u'], ch['vm']], axis=0)))
        ch['y'] = yst[:c] + yst[c:]
    for ch in chains:
        pc = ch['pc']
        pc_col = jnp.transpose(jnp.broadcast_to(pc, (PAIR, PAIR)))
        upd = _mm(_parts(jnp.concatenate([jnp.transpose(ch['am'] * pc), jnp.transpose(ch['km'] * pc)],
                                         axis=1)),
                  _parts(jnp.concatenate([ch['u'], ch['vm']], axis=0)))
        state_ref[ch['bi'], ch['pp']] = ch['h0'] * pc_col + upd
    for ch in chains:
        ch['mean'] = _mm(_parts(ch['y'], 2), head_sum_b) * (1.0 / HEAD_DIM)
        ch['bonus'] = _mm(_parts(ch['r'] * ch['k'] * rk_ref[:, ch['sl']], 2), head_sum_b) * ch['v']
    for ch in chains:
        yc = ch['y'] - ch['mean']
        ch['yc'] = yc
        ch['var'] = _mm(_parts(yc * yc, 2), head_sum_b) * (1.0 / HEAD_DIM)
    for ch in chains:
        sl = ch['sl']
        yn = ch['yc'] * lax.rsqrt(ch['var'] + GN_EPS) * lng_ref[:, sl] + lnb_ref[:, sl]
        y_ref[ch['bi'], :, sl] = ((yn + ch['bonus']) * ch['g']).astype(y_ref.dtype)


def _rwkv(proj_rw, mu, w0, a0, wlora, wg, k_k, k_a, r_k, lnx_g, lnx_b):
    b, s, n = proj_rw.shape
    c = RWKV_CHUNK
    vec = lambda width: pl.BlockSpec((1, width), lambda ci: (0, 0))
    return pl.pallas_call(
        _rwkv_kernel,
        grid=(s // c,),
        in_specs=[pl.BlockSpec((b, c, n), lambda ci: (0, ci, 0)),
                  vec(n), vec(WIDTH), vec(WIDTH),
                  pl.BlockSpec(wlora.shape, lambda ci: (0, 0)),
                  pl.BlockSpec(wg.shape, lambda ci: (0, 0)),
                  vec(WIDTH), vec(WIDTH), vec(WIDTH), vec(WIDTH), vec(WIDTH)],
        out_specs=pl.BlockSpec((b, c, WIDTH), lambda ci: (0, ci, 0)),
        out_shape=jax.ShapeDtypeStruct((b, s, WIDTH), BF16),
        scratch_shapes=[pltpu.VMEM((b, 8, n), F32),
                        pltpu.VMEM((b, N_PAIRS, PAIR, PAIR), F32)],
        compiler_params=pltpu.CompilerParams(dimension_semantics=("arbitrary",),
                                             vmem_limit_bytes=VMEM_LIMIT),
    )(proj_rw, mu, w0, a0, wlora, wg, k_k, k_a, r_k, lnx_g, lnx_b)


MOBA_HEADS_PER_STEP = 4
MOBA_VMEM_LIMIT = 60 * 1024 * 1024


def _key_group(nb):
    return min(KEY_GROUP, nb)


def _moba_kernel(qt_ref, k_ref, vt_ref, km_ref, o_ref, sa_ref, sb_ref, mxa_ref, mxb_ref, *, nb):
    bs = MOBA_BLOCK
    heads = range(qt_ref.shape[1])
    i = pl.program_id(2)
    blk = lax.broadcasted_iota(jnp.int32, (nb, bs), 0).astype(F32)

    q_sel, q_own = [], []
    for h in heads:
        qt = qt_ref[0, h]
        km = km_ref[0, h].astype(BF16)
        gate = jnp.where(blk < i.astype(F32), _dot(km, qt), NEG_INF)
        sel = jnp.zeros((nb, bs), F32)
        for _ in range(MOBA_TOPK):
            mx = jnp.max(gate, axis=0, keepdims=True)
            hit = (gate == mx) & (mx > NEG_INF)
            idx = jnp.min(jnp.where(hit, blk, float(nb)), axis=0, keepdims=True)
            pick = blk == idx
            sel = jnp.where(pick, 1.0, sel)
            gate = jnp.where(pick, NEG_INF, gate)
        bias = jnp.where(sel > 0.0, 0.0, MASK_BIAS)
        if nb < HEAD_DIM:
            bias = jnp.concatenate([bias, jnp.zeros((HEAD_DIM - nb, bs), F32)], axis=0)
        q_sel.append(jnp.concatenate([qt, bias.astype(BF16)], axis=0))
        q_own.append(jnp.concatenate([qt, jnp.zeros((HEAD_DIM, bs), BF16)], axis=0))

    span = _key_group(nb) * bs
    n_groups = lax.shift_right_logical(i * bs + span - 1, span.bit_length() - 1)

    def col_max(sc):
        return jnp.max(jnp.max(sc.reshape(span // bs, bs, bs), axis=0), axis=0, keepdims=True)

    even, odd = (sa_ref, mxa_ref), (sb_ref, mxb_ref)

    def put(bufs, g):
        off = pl.multiple_of(g * span, span)
        for h in heads:
            s_new = _dot(k_ref[0, h, pl.ds(off, span), :], q_sel[h])
            bufs[0][h] = s_new
            bufs[1][h] = col_max(s_new)

    def absorb(bufs, g, carry):
        off = pl.multiple_of(g * span, span)
        out = []
        for h in heads:
            m, acc = carry[2 * h], carry[2 * h + 1]
            m_new = jnp.maximum(m, bufs[1][h])
            p = jnp.exp2(bufs[0][h] - m_new)
            acc = jnp.exp2(m - m_new) * acc + _dot(vt_ref[0, h, :, pl.ds(off, span)], p.astype(BF16))
            out += [m_new, acc]
        return tuple(out)

    start = pl.multiple_of(i * bs, bs)
    s_own = [_dot(k_ref[0, h, pl.ds(start, bs), :], q_own[h]) for h in heads]
    put(even, 0)

    key_i = lax.broadcasted_iota(jnp.int32, (bs, bs), 0)
    qry_i = lax.broadcasted_iota(jnp.int32, (bs, bs), 1)
    carry = []
    for h in heads:
        s = jnp.where(key_i <= qry_i, s_own[h], NEG_INF)
        m = jnp.max(s, axis=0, keepdims=True)
        p = jnp.exp2(s - m)
        carry += [m, _dot(vt_ref[0, h, :, pl.ds(start, bs)], p.astype(BF16))]
    carry = tuple(carry)

    def pair(t, carry):
        put(odd, 2 * t + 1)
        carry = absorb(even, 2 * t, carry)
        put(even, 2 * t + 2)
        return absorb(odd, 2 * t + 1, carry)

    n_pairs = lax.shift_right_logical(jnp.maximum(n_groups - 1, 0), 1)
    carry = lax.fori_loop(0, n_pairs, pair, carry)
    last_even = 2 * n_pairs

    def tail_two(*carry):
        put(odd, last_even + 1)
        return absorb(odd, last_even + 1, absorb(even, last_even, carry))

    def tail_one(*carry):
        return absorb(even, last_even, carry)

    def tail(*carry):
        return lax.cond(n_groups - last_even == 2, tail_two, tail_one, *carry)

    carry = lax.cond(n_groups > 0, tail, lambda *carry: carry, *carry)
    out_t = [carry[2 * h + 1][:HEAD_DIM] / carry[2 * h + 1][HEAD_DIM:HEAD_DIM + 1] for h in heads]
    o_ref[0] = jnp.transpose(jnp.concatenate(out_t, axis=0)).astype(o_ref.dtype)


def _moba(qt, kaug, vaug, kmean):
    b, nh, _, s = qt.shape
    nb = s // MOBA_BLOCK
    group = _key_group(nb)
    hp = MOBA_HEADS_PER_STEP
    assert nb % group == 0 and nh % hp == 0 and (hp * HEAD_DIM) % PAIR == 0
    return pl.pallas_call(
        functools.partial(_moba_kernel, nb=nb),
        grid=(b, nh // hp, nb),
        in_specs=[pl.BlockSpec((1, hp, HEAD_DIM, MOBA_BLOCK), lambda bi, hi, qi: (bi, hi, 0, qi)),
                  pl.BlockSpec((1, hp, s, PAIR), lambda bi, hi, qi: (bi, hi, 0, 0),
                               pipeline_mode=pl.Buffered(1)),
                  pl.BlockSpec((1, hp, VALUE_ROWS, s), lambda bi, hi, qi: (bi, hi, 0, 0),
                               pipeline_mode=pl.Buffered(1)),
                  pl.BlockSpec((1, hp, nb, HEAD_DIM), lambda bi, hi, qi: (bi, hi, 0, 0))],
        out_specs=pl.BlockSpec((1, MOBA_BLOCK, hp * HEAD_DIM), lambda bi, hi, qi: (bi, qi, hi)),
        out_shape=jax.ShapeDtypeStruct((b, s, nh * HEAD_DIM), BF16),
        scratch_shapes=[pltpu.VMEM((hp, group * MOBA_BLOCK, MOBA_BLOCK), F32),
                        pltpu.VMEM((hp, group * MOBA_BLOCK, MOBA_BLOCK), F32),
                        pltpu.VMEM((hp, 1, MOBA_BLOCK), F32),
                        pltpu.VMEM((hp, 1, MOBA_BLOCK), F32)],
        compiler_params=pltpu.CompilerParams(
            dimension_semantics=("parallel", "parallel", "arbitrary"),
            vmem_limit_bytes=MOBA_VMEM_LIMIT),
    )(qt, kaug, vaug, kmean)


def _layer_norm(z, g, b):
    mu = jnp.mean(z, axis=1, keepdims=True)
    zc = z - mu
    var = jnp.mean(zc * zc, axis=1, keepdims=True)
    return zc * lax.rsqrt(var + LN_EPS) * g + b


def _merge_kernel(x_ref, ya_ref, yr_ref, wgate_ref, wab_ref, wrb_ref, wout_ref, g1_ref, b1_ref,
                  wr_hi_ref, wr_lo_ref, br_ref, h_ref, idx_ref, gate_ref, *, alpha):
    x = x_ref[...]
    d = x.shape[1]
    gates = _sigmoid(_dot(x.astype(BF16), wgate_ref[...]))
    mixed = (gates[:, :d] * _dot(ya_ref[...], wab_ref[...])
             + gates[:, d:] * _dot(yr_ref[...], wrb_ref[...]))
    h = _layer_norm(alpha * x + _dot(mixed.astype(BF16), wout_ref[...]), g1_ref[...], b1_ref[...])
    h_ref[...] = h

    logits = _mm(_parts(h, 2), [wr_hi_ref[...], wr_lo_ref[...]]) + br_ref[...]
    col = lax.broadcasted_iota(jnp.int32, logits.shape, 1).astype(F32)
    idx_out = jnp.zeros(logits.shape, F32)
    val_out = jnp.zeros(logits.shape, F32)
    top = None
    denom = None
    for t in range(TOP_K):
        mx = jnp.max(logits, axis=1, keepdims=True)
        idx = jnp.min(jnp.where(logits == mx, col, float(ROUTER_PAD)), axis=1, keepdims=True)
        if t == 0:
            top = mx
        e = jnp.exp(mx - top)
        denom = e if t == 0 else denom + e
        idx_out = jnp.where(col == float(t), idx, idx_out)
        val_out = jnp.where(col == float(t), e, val_out)
        logits = jnp.where(col == idx, NEG_INF, logits)
    idx_ref[...] = idx_out.astype(jnp.int32)
    gate_ref[...] = val_out / denom


def _merge(x2, ya, yr, wgate, wab, wrb, wout, g1, b1, wr, br, alpha, tm=512):
    wr_hi, wr_lo = _parts(wr, 2)
    t, d = x2.shape
    row = lambda width: pl.BlockSpec((tm, width), lambda i: (i, 0))
    full = lambda arr: pl.BlockSpec(arr.shape, lambda i: (0, 0))
    return pl.pallas_call(
        functools.partial(_merge_kernel, alpha=alpha),
        grid=(t // tm,),
        in_specs=[row(d), row(WIDTH), row(WIDTH), full(wgate), full(wab), full(wrb), full(wout),
                  full(g1), full(b1), full(wr_hi), full(wr_lo), full(br)],
        out_specs=[row(d), row(ROUTER_PAD), row(ROUTER_PAD)],
        out_shape=[jax.ShapeDtypeStruct((t, d), F32),
                   jax.ShapeDtypeStruct((t, ROUTER_PAD), jnp.int32),
                   jax.ShapeDtypeStruct((t, ROUTER_PAD), F32)],
        compiler_params=pltpu.CompilerParams(dimension_semantics=("parallel",),
                                             vmem_limit_bytes=VMEM_LIMIT),
    )(x2, ya, yr, wgate, wab, wrb, wout, g1, b1, wr_hi, wr_lo, br)


DEINTERLEAVE_GROUP = 256


def _deinterleave_kernel(w_ref, perm_ref, o_ref):
    g = DEINTERLEAVE_GROUP
    n = w_ref.shape[2]
    half = n // 2
    for c in range(n // g):
        res = _dot(w_ref[0, :, c * g:(c + 1) * g].astype(BF16), perm_ref[...])
        o_ref[0, :, c * g // 2:(c + 1) * g // 2] = res[:, :g // 2].astype(BF16)
        o_ref[0, :, half + c * g // 2:half + (c + 1) * g // 2] = res[:, g // 2:].astype(BF16)


def _deinterleave(w1, tr=256):
    e, d, n = w1.shape
    g = DEINTERLEAVE_GROUP
    src = jnp.arange(g)
    dst = jnp.where(src % 2 == 0, src // 2, g // 2 + src // 2)
    perm = (dst[:, None] == jnp.arange(g)[None, :]).astype(BF16)
    return pl.pallas_call(
        _deinterleave_kernel,
        grid=(e, d // tr),
        in_specs=[pl.BlockSpec((1, tr, n), lambda ei, ri: (ei, ri, 0)),
                  pl.BlockSpec((g, g), lambda ei, ri: (0, 0))],
        out_specs=pl.BlockSpec((1, tr, n), lambda ei, ri: (ei, ri, 0)),
        out_shape=jax.ShapeDtypeStruct((e, d, n), BF16),
        compiler_params=pltpu.CompilerParams(dimension_semantics=("parallel", "parallel"),
                                             vmem_limit_bytes=VMEM_LIMIT),
    )(w1, perm)

def _row_copy(src_hbm, dst_buf, sem, src_row, slot, dst_row):
    return pltpu.make_async_copy(src_hbm.at[pl.ds(src_row, 1), :],
                                 dst_buf.at[slot, pl.ds(dst_row, 1), :],
                                 sem.at[slot])


GATHER_UNROLL = 16


def _gather_rows(src_hbm, dst_buf, sem, idx_ref, slot, n_rows):
    def issue(run, _):
        base = pl.multiple_of(run * GATHER_UNROLL, GATHER_UNROLL)
        for j in range(GATHER_UNROLL):
            _row_copy(src_hbm, dst_buf, sem, idx_ref[0, 0, base + j], slot, base + j).start(priority=j % 2)
        return 0
    lax.fori_loop(0, n_rows // GATHER_UNROLL, issue, 0)


def _wait_rows(src_hbm, dst_buf, sem, slot, n_rows):
    pltpu.make_async_copy(src_hbm.at[pl.ds(0, n_rows), :], dst_buf.at[slot], sem.at[slot]).wait()


ROUTE_ROWS = 512
DISPATCH_ROWS = 256


def _route_kernel(idx_ref, dest_ref, cnt_ref, run_ref, start_ref, *, tm):
    phase = pl.program_id(0)
    i = pl.program_id(1)
    idx = idx_ref[...]
    lane = lax.broadcasted_iota(jnp.int32, idx.shape, 1)
    hot = [jnp.where(lane == idx[:, k:k + 1], 1.0, 0.0) for k in range(TOP_K)]
    cnt = hot[0] + hot[1] + hot[2] + hot[3]
    tile_total = jnp.sum(cnt, axis=0, keepdims=True)

    @pl.when((phase == 0) & (i == 0))
    def _():
        run_ref[...] = jnp.zeros_like(run_ref)

    @pl.when(phase == 0)
    def _():
        run_ref[...] += tile_total
        dest_ref[...] = jnp.zeros_like(dest_ref)

    @pl.when((phase == 1) & (i == 0))
    def _():
        counts = run_ref[...]
        padded = jnp.floor((counts + (EXPERT_ROWS - 1)) * (1.0 / EXPERT_ROWS)) * EXPERT_ROWS
        ri = lax.broadcasted_iota(jnp.int32, (ROUTER_PAD, ROUTER_PAD), 0)
        ci = lax.broadcasted_iota(jnp.int32, (ROUTER_PAD, ROUTER_PAD), 1)
        before = jnp.where(ri < ci, 1.0, 0.0).astype(BF16)
        start = _mm(_parts(jnp.broadcast_to(padded, (8, ROUTER_PAD)), 3), [before])
        start_ref[...] = start[0:1]
        cnt_ref[...] = counts
        run_ref[...] = jnp.zeros_like(run_ref)

    @pl.when(phase == 1)
    def _():
        rt = lax.broadcasted_iota(jnp.int32, (tm, tm), 0)
        ct = lax.broadcasted_iota(jnp.int32, (tm, tm), 1)
        earlier = jnp.where(ct < rt, 1.0, 0.0).astype(BF16)
        pos = start_ref[...] + run_ref[...] + _dot(earlier, cnt.astype(BF16))
        dest = jnp.zeros(idx.shape, F32)
        for k in range(TOP_K):
            d_k = jnp.sum(hot[k] * pos, axis=1, keepdims=True)
            dest = jnp.where(lane == k, d_k, dest)
            pos = pos + hot[k]
        dest_ref[...] = dest.astype(jnp.int32)
        run_ref[...] += tile_total


def _route(idx_pad):
    t = idx_pad.shape[0]
    tm = ROUTE_ROWS
    return pl.pallas_call(
        functools.partial(_route_kernel, tm=tm),
        grid=(2, t // tm),
        in_specs=[pl.BlockSpec((tm, ROUTER_PAD), lambda ph, i: (i, 0))],
        out_specs=[pl.BlockSpec((tm, ROUTER_PAD), lambda ph, i: (ph * i, 0)),
                   pl.BlockSpec((1, ROUTER_PAD), lambda ph, i: (0, 0))],
        out_shape=[jax.ShapeDtypeStruct((t, ROUTER_PAD), jnp.int32),
                   jax.ShapeDtypeStruct((1, ROUTER_PAD), F32)],
        scratch_shapes=[pltpu.VMEM((1, ROUTER_PAD), F32), pltpu.VMEM((1, ROUTER_PAD), F32)],
        compiler_params=pltpu.CompilerParams(dimension_semantics=("arbitrary", "arbitrary"),
                                             vmem_limit_bytes=VMEM_LIMIT),
    )(idx_pad)


def _dispatch_kernel(pad_end_ref, dest_ref, h_ref, xs_hbm, zero_ref, sem, *, tm):
    @pl.when(pl.program_id(0) == 0)
    def _():
        zero_ref[...] = jnp.zeros_like(zero_ref)

        def last_block(e):
            end = pad_end_ref[e]
            start = pl.multiple_of(jnp.maximum(end - EXPERT_ROWS, 0), EXPERT_ROWS)
            return pltpu.make_async_copy(zero_ref, xs_hbm.at[pl.ds(start, EXPERT_ROWS), :], sem.at[0])

        for e in range(N_EXPERTS):
            last_block(e).start()
        for e in range(N_EXPERTS):
            last_block(e).wait()

        def unused_block(b):
            start = pl.multiple_of(b * EXPERT_ROWS, EXPERT_ROWS)
            return pltpu.make_async_copy(zero_ref, xs_hbm.at[pl.ds(start, EXPERT_ROWS), :], sem.at[0])

        first_unused = lax.shift_right_logical(pad_end_ref[N_EXPERTS - 1], EXPERT_ROWS.bit_length() - 1)
        n_blocks = xs_hbm.shape[0] // EXPERT_ROWS

        def start_one(b, _):
            unused_block(b).start()
            return 0

        def wait_one(b, _):
            unused_block(b).wait()
            return 0

        lax.fori_loop(first_unused, n_blocks, start_one, 0)
        lax.fori_loop(first_unused, n_blocks, wait_one, 0)

    run_len = 8

    def issue(run, _):
        base = pl.multiple_of(run * run_len, run_len)
        for j in range(run_len):
            for k in range(TOP_K):
                pltpu.make_async_copy(h_ref.at[pl.ds(base + j, 1), :],
                                      xs_hbm.at[pl.ds(dest_ref[0, 0, TOP_K * (base + j) + k], 1), :],
                                      sem.at[0]).start(priority=k % 2)
        return 0

    lax.fori_loop(0, tm // run_len, issue, 0)
    for _ in range(TOP_K):
        pltpu.make_async_copy(h_ref, xs_hbm.at[pl.ds(0, tm), :], sem.at[0]).wait()


def _dispatch(pad_end, dest, h, n_rows):
    t, d = h.shape
    tm = DISPATCH_ROWS
    n = TOP_K * tm
    grid_spec = pltpu.PrefetchScalarGridSpec(
        num_scalar_prefetch=1,
        grid=(t // tm,),
        in_specs=[pl.BlockSpec((1, 1, n), lambda i, pe: (i, 0, 0), memory_space=pltpu.SMEM),
                  pl.BlockSpec((tm, d), lambda i, pe: (i, 0))],
        out_specs=pl.BlockSpec(memory_space=pl.ANY),
        scratch_shapes=[pltpu.VMEM((EXPERT_ROWS, d), F32), pltpu.SemaphoreType.DMA((1,))],
    )
    return pl.pallas_call(
        functools.partial(_dispatch_kernel, tm=tm),
        grid_spec=grid_spec,
        out_shape=jax.ShapeDtypeStruct((n_rows, d), F32),
        compiler_params=pltpu.CompilerParams(dimension_semantics=("arbitrary",),
                                             vmem_limit_bytes=VMEM_LIMIT),
    )(pad_end, dest.reshape(t // tm, 1, n), h)


def _expert_kernel(be_ref, nact_ref, x_ref, w1_ref, b1_ref, w2_ref, b2_ref, y_ref):
    del be_ref

    @pl.when(pl.program_id(0) < nact_ref[0])
    def _():
        f = w2_ref.shape[1]
        hid = _dot(x_ref[...].astype(BF16), w1_ref[0]) + b1_ref[0]
        x_glu = jnp.minimum(hid[:, :f], SWIGLU_LIMIT)
        x_lin = jnp.clip(hid[:, f:], -SWIGLU_LIMIT, SWIGLU_LIMIT)
        act = x_glu * _sigmoid(SWIGLU_ALPHA * x_glu) * (x_lin + 1.0)
        y_ref[...] = _dot(act.astype(BF16), w2_ref[0]) + b2_ref[0]

    @pl.when(pl.program_id(0) >= nact_ref[0])
    def _():
        y_ref[...] = jnp.zeros_like(y_ref)


def _experts(block_expert, n_active, x_sorted, w1p, b1p, w2b, b2):
    n_rows, d = x_sorted.shape
    rows = EXPERT_ROWS
    nblk = n_rows // rows
    f = w2b.shape[1]
    grid_spec = pltpu.PrefetchScalarGridSpec(
        num_scalar_prefetch=2,
        grid=(nblk,),
        in_specs=[
            pl.BlockSpec((rows, d), lambda b, be, na: (jnp.minimum(b, jnp.maximum(na[0] - 1, 0)), 0)),
            pl.BlockSpec((1, d, 2 * f), lambda b, be, na: (be[b], 0, 0)),
            pl.BlockSpec((1, 1, 2 * f), lambda b, be, na: (be[b], 0, 0)),
            pl.BlockSpec((1, f, d), lambda b, be, na: (be[b], 0, 0)),
            pl.BlockSpec((1, 1, d), lambda b, be, na: (be[b], 0, 0)),
        ],
        out_specs=pl.BlockSpec((rows, d), lambda b, be, na: (b, 0)),
    )
    return pl.pallas_call(
        _expert_kernel,
        grid_spec=grid_spec,
        out_shape=jax.ShapeDtypeStruct((n_rows, d), F32),
        compiler_params=pltpu.CompilerParams(dimension_semantics=("arbitrary",),
                                             vmem_limit_bytes=VMEM_LIMIT),
    )(block_expert, n_active, x_sorted, w1p, b1p, w2b, b2)


def _combine_kernel(dest_ref, dest_next_ref, h_ref, gate_ref, g2_ref, b2_ref, y_hbm, o_ref, ybuf, sem,
                    *, alpha, nsteps):
    tm = COMBINE_ROWS
    n = TOP_K * tm
    s = pl.program_id(0)
    slot = s % 2

    @pl.when(s == 0)
    def _():
        _gather_rows(y_hbm, ybuf, sem, dest_ref, 0, n)

    @pl.when(s + 1 < nsteps)
    def _():
        _gather_rows(y_hbm, ybuf, sem, dest_next_ref, 1 - slot, n)

    _wait_rows(y_hbm, ybuf, sem, slot, n)
    gates = gate_ref[...]
    moe = gates[:, 0:1] * ybuf[slot, 0:tm, :]
    for kk in range(1, TOP_K):
        moe = moe + gates[:, kk:kk + 1] * ybuf[slot, kk * tm:(kk + 1) * tm, :]
    o_ref[...] = _layer_norm(alpha * h_ref[...] + moe, g2_ref[...], b2_ref[...])


def _combine(dest_km, h, gate_pad, g2, b2, y_sorted, alpha):
    t, d = h.shape
    tm = COMBINE_ROWS
    nsteps = t // tm
    n = TOP_K * tm
    return pl.pallas_call(
        functools.partial(_combine_kernel, alpha=alpha, nsteps=nsteps),
        grid=(nsteps,),
        in_specs=[
            pl.BlockSpec((1, 1, n), lambda s: (s, 0, 0), memory_space=pltpu.SMEM),
            pl.BlockSpec((1, 1, n), lambda s: (jnp.minimum(s + 1, nsteps - 1), 0, 0),
                         memory_space=pltpu.SMEM),
            pl.BlockSpec((tm, d), lambda s: (s, 0)),
            pl.BlockSpec((tm, ROUTER_PAD), lambda s: (s, 0)),
            pl.BlockSpec((1, d), lambda s: (0, 0)),
            pl.BlockSpec((1, d), lambda s: (0, 0)),
            pl.BlockSpec(memory_space=pl.ANY),
        ],
        out_specs=pl.BlockSpec((tm, d), lambda s: (s, 0)),
        out_shape=jax.ShapeDtypeStruct((t, d), F32),
        scratch_shapes=[pltpu.VMEM((2, n, d), F32), pltpu.SemaphoreType.DMA((2,))],
        compiler_params=pltpu.CompilerParams(dimension_semantics=("arbitrary",),
                                             vmem_limit_bytes=VMEM_LIMIT),
    )(dest_km, dest_km, h, gate_pad, g2, b2, y_sorted)


def _block_tables(counts, nblk):
    rows = EXPERT_ROWS
    counts = counts.astype(jnp.int32)
    padded = (counts + rows - 1) // rows * rows
    pad_ends = jnp.cumsum(padded)
    block_start = jnp.arange(nblk, dtype=jnp.int32) * rows
    block_expert = jnp.sum((pad_ends[None, :] <= block_start[:, None]).astype(jnp.int32), axis=1)
    block_expert = jnp.minimum(block_expert, N_EXPERTS - 1).astype(jnp.int32)
    n_active = (pad_ends[-1] // rows).astype(jnp.int32).reshape(1)
    return block_expert, n_active, pad_ends.astype(jnp.int32)


def _layer(x, w_in, mu_shift, w0, w_decay_up, a0, w_aaa_up, w_gate_up, k_k, k_a, r_k, lnx_g, lnx_b,
           w_attn_br, w_rwkv_br, w_out, ln1_g, ln1_b, w_router, b_router, w1, b1, w2, b2, ln2_g, ln2_b,
           alpha):
    b, s, d = x.shape
    t = b * s
    x2 = x.reshape(t, d)
    xb = x2.astype(BF16)
    row = lambda vec: vec.reshape(1, -1)

    off_q = RW_COLS
    off_gate = off_q + 3 * WIDTH
    pad_cols = RW_COLS_PAD - RW_COLS
    w_rw = jnp.pad(w_in[:, :RW_COLS], ((0, 0), (0, pad_cols))).astype(BF16)
    mu = jnp.pad(mu_shift, (0, pad_cols)).reshape(1, -1)
    w_qkv = w_in[:, off_q:off_gate].astype(BF16)
    w_gate = w_in[:, off_gate:].astype(BF16)

    half = HEAD_DIM // 2
    inv_freq = ROPE_THETA ** (-jnp.arange(half, dtype=F32) / half)
    ang = jnp.arange(s, dtype=F32)[:, None] * inv_freq[None, :]
    cos, sin = jnp.cos(ang), jnp.sin(ang)
    cos_t = jnp.concatenate([cos, cos, cos, cos], axis=1)
    sin_t = jnp.concatenate([-sin, sin, -sin, sin], axis=1)

    proj_rw = _project(xb, w_rw).reshape(b, s, RW_COLS_PAD)
    qt, kaug, vaug, kmean = _project_qkv(xb, w_qkv, cos_t, sin_t, b, s)
    kmean = kmean.reshape(b, s // MOBA_BLOCK, N_HEADS, HEAD_DIM).transpose(0, 2, 1, 3)

    zeros = jnp.zeros((DECAY_LORA, WIDTH), F32)
    w_lora = jnp.concatenate([jnp.concatenate([w_decay_up, zeros], axis=1),
                              jnp.concatenate([zeros, w_aaa_up], axis=1)], axis=0).astype(BF16)
    w_g = jnp.pad(w_gate_up, ((0, GATE_LORA_PAD - GATE_LORA), (0, 0))).astype(BF16)
    y_rwkv = _rwkv(proj_rw, mu, row(w0), row(a0), w_lora, w_g, row(k_k), row(k_a), row(r_k),
                   row(lnx_g), row(lnx_b))
    y_attn = _moba(qt, kaug, vaug, kmean)

    w_r = jnp.pad(w_router, ((0, 0), (0, ROUTER_PAD - N_EXPERTS)))
    b_r = jnp.pad(b_router, (0, ROUTER_PAD - N_EXPERTS), constant_values=NEG_INF).reshape(1, -1)
    h, idx_pad, gate_pad = _merge(x2, y_attn.reshape(t, WIDTH), y_rwkv.reshape(t, WIDTH), w_gate,
                                  w_attn_br.astype(BF16), w_rwkv_br.astype(BF16), w_out.astype(BF16),
                                  row(ln1_g), row(ln1_b), w_r, b_r, alpha)

    dest_pad, counts = _route(idx_pad)
    dest = dest_pad[:, :TOP_K]
    nblk = t * TOP_K // EXPERT_ROWS + N_EXPERTS
    block_expert, n_active, pad_end = _block_tables(counts[0, :N_EXPERTS], nblk)
    x_sorted = _dispatch(pad_end, dest, h, nblk * EXPERT_ROWS)
    w1p = _deinterleave(w1)
    b1p = jnp.concatenate([b1[:, 0::2], b1[:, 1::2]], axis=1)[:, None, :]
    y_sorted = _experts(block_expert, n_active, x_sorted, w1p, b1p, w2.astype(BF16), b2[:, None, :])

    tm = COMBINE_ROWS
    dest_km = dest.reshape(t // tm, tm, TOP_K).transpose(0, 2, 1).reshape(t // tm, 1, TOP_K * tm)
    out = _combine(dest_km, h, gate_pad, row(ln2_g), row(ln2_b), y_sorted, alpha)
    return out.reshape(b, s, d)


def kernel(x, w_in, mu_shift, w0, w_decay_up, a0, w_aaa_up, w_gate_up, k_k, k_a, r_k, lnx_g, lnx_b,
           w_attn_br, w_rwkv_br, w_out, ln1_g, ln1_b, w_router, b_router, w1, b1, w2, b2, ln2_g, ln2_b):
    depth = w_in.shape[0]
    alpha = (2 * depth) ** 0.25
    for l in range(depth):
        x = _layer(x, w_in[l], mu_shift[l], w0[l], w_decay_up[l], a0[l], w_aaa_up[l], w_gate_up[l],
                   k_k[l], k_a[l], r_k[l].reshape(-1), lnx_g[l], lnx_b[l], w_attn_br[l], w_rwkv_br[l],
                   w_out[l], ln1_g[l], ln1_b[l], w_router[l], b_router[l], w1[l], b1[l], w2[l], b2[l],
                   ln2_g[l], ln2_b[l], alpha)
    return x
```

```python
import functools

import jax
import jax.numpy as jnp
from jax import lax
from jax.experimental import pallas as pl
from jax.experimental.pallas import tpu as pltpu

F32 = jnp.float32
BF16 = jnp.bfloat16
HI = lax.Precision.HIGHEST

HEAD_DIM = 64
N_HEADS = 8
WIDTH = N_HEADS * HEAD_DIM
PAIR = 2 * HEAD_DIM
N_PAIRS = N_HEADS // 2
MOBA_BLOCK = 256
MOBA_TOPK = 3
KEY_GROUP = 8
VALUE_ROWS = 2 * HEAD_DIM
ROPE_THETA = 10000.0
DECAY_LORA = 64
AAA_LORA = 64
GATE_LORA = 160
GATE_LORA_PAD = 256
RW_COLS = 3 * WIDTH + DECAY_LORA + AAA_LORA + GATE_LORA
RW_COLS_PAD = 3 * WIDTH + DECAY_LORA + AAA_LORA + GATE_LORA_PAD
GN_EPS = 64e-5
LN_EPS = 1e-5
N_EXPERTS = 32
TOP_K = 4
ROUTER_PAD = 128
SWIGLU_ALPHA = 1.702
SWIGLU_LIMIT = 7.0
RWKV_CHUNK = 64
RWKV_CHUNKS_PER_STEP = 2
EXPERT_ROWS = 512
COMBINE_ROWS = 256
VMEM_LIMIT = 48 * 1024 * 1024

NEG_INF = float("-inf")
LOG2_E = 1.4426950408889634
MASK_BIAS = -1e30


def _nt(a, b, precision=None):
    return lax.dot_general(a, b, (((1,), (1,)), ((), ())), precision=precision,
                           preferred_element_type=F32)


def _dot(a, b, precision=None):
    return jnp.dot(a, b, precision=precision, preferred_element_type=F32)


def _matmul_kernel(x_ref, w_ref, o_ref):
    o_ref[...] = _dot(x_ref[...], w_ref[...])


def _project(xb, w, tm=512):
    t, d = xb.shape
    n = w.shape[1]
    return pl.pallas_call(
        _matmul_kernel,
        grid=(t // tm,),
        in_specs=[pl.BlockSpec((tm, d), lambda i: (i, 0)),
                  pl.BlockSpec((d, n), lambda i: (0, 0))],
        out_specs=pl.BlockSpec((tm, n), lambda i: (i, 0)),
        out_shape=jax.ShapeDtypeStruct((t, n), F32),
        compiler_params=pltpu.CompilerParams(dimension_semantics=("parallel",),
                                             vmem_limit_bytes=VMEM_LIMIT),
    )(xb, w)


def _qkv_kernel(x_ref, w_ref, cos_ref, sin_ref, qt_ref, kaug_ref, vaug_ref, km_ref, *, tm, steps_per_seq):
    acc = _dot(x_ref[...], w_ref[...])
    cos = jnp.concatenate([cos_ref[...]] * (WIDTH // PAIR), axis=1)
    sin = jnp.concatenate([sin_ref[...]] * (WIDTH // PAIR), axis=1)
    lane = lax.broadcasted_iota(jnp.int32, (tm, WIDTH), 1)
    first_half = (lane & (HEAD_DIM // 2)) == 0

    def rope(t):
        partner = jnp.where(first_half, pltpu.roll(t, WIDTH - HEAD_DIM // 2, 1),
                            pltpu.roll(t, HEAD_DIM // 2, 1))
        return t * cos + partner * sin

    q = rope(acc[:, :WIDTH]) * (HEAD_DIM ** -0.5 * LOG2_E)
    k = rope(acc[:, WIDTH:2 * WIDTH])
    v = acc[:, 2 * WIDTH:]
    for j in range(tm // MOBA_BLOCK):
        km_ref[0, j:j + 1, :] = jnp.mean(k[j * MOBA_BLOCK:(j + 1) * MOBA_BLOCK], axis=0, keepdims=True)

    lane_p = lax.broadcasted_iota(jnp.int32, (tm, PAIR), 1)
    row_p = lax.broadcasted_iota(jnp.int32, (tm, PAIR), 0)
    first_block = (pl.program_id(0) % steps_per_seq) * (tm // MOBA_BLOCK)
    row_block = lax.shift_right_logical(row_p, MOBA_BLOCK.bit_length() - 1)
    block_tag = jnp.where(lane_p - HEAD_DIM == first_block + row_block, 1.0, 0.0)
    ones = jnp.ones((VALUE_ROWS - HEAD_DIM, tm), F32)
    for pp in range(N_PAIRS):
        sl = slice(pp * PAIR, (pp + 1) * PAIR)
        q_t = jnp.transpose(q[:, sl])
        v_t = jnp.transpose(v[:, sl])
        k_p = k[:, sl]
        k_sw = pltpu.roll(k_p, HEAD_DIM, 1)
        for h, k_h in ((0, k_p), (1, k_sw)):
            rows = slice(h * HEAD_DIM, (h + 1) * HEAD_DIM)
            qt_ref[0, 2 * pp + h] = q_t[rows].astype(BF16)
            kaug_ref[0, 2 * pp + h] = jnp.where(lane_p < HEAD_DIM, k_h, block_tag).astype(BF16)
            vaug_ref[0, 2 * pp + h] = jnp.concatenate([v_t[rows], ones], axis=0).astype(BF16)


def _project_qkv(xb, w, cos_t, sin_t, batch, seq, tm=512):
    t, d = xb.shape
    steps_per_seq = seq // tm
    assert seq // MOBA_BLOCK <= HEAD_DIM, "block one-hot tags must fit the spare lanes of a head"
    tab_spec = pl.BlockSpec((tm, PAIR), lambda i: (i % steps_per_seq, 0))
    return pl.pallas_call(
        functools.partial(_qkv_kernel, tm=tm, steps_per_seq=steps_per_seq),
        grid=(t // tm,),
        in_specs=[pl.BlockSpec((tm, d), lambda i: (i, 0)),
                  pl.BlockSpec((d, 3 * WIDTH), lambda i: (0, 0)),
                  tab_spec, tab_spec],
        out_specs=[
            pl.BlockSpec((1, N_HEADS, HEAD_DIM, tm), lambda i: (i // steps_per_seq, 0, 0, i % steps_per_seq)),
            pl.BlockSpec((1, N_HEADS, tm, PAIR), lambda i: (i // steps_per_seq, 0, i % steps_per_seq, 0)),
            pl.BlockSpec((1, N_HEADS, VALUE_ROWS, tm),
                         lambda i: (i // steps_per_seq, 0, 0, i % steps_per_seq)),
            pl.BlockSpec((1, tm // MOBA_BLOCK, WIDTH), lambda i: (i, 0, 0))],
        out_shape=[jax.ShapeDtypeStruct((batch, N_HEADS, HEAD_DIM, seq), BF16),
                   jax.ShapeDtypeStruct((batch, N_HEADS, seq, PAIR), BF16),
                   jax.ShapeDtypeStruct((batch, N_HEADS, VALUE_ROWS, seq), BF16),
                   jax.ShapeDtypeStruct((t // tm, tm // MOBA_BLOCK, WIDTH), F32)],
        compiler_params=pltpu.CompilerParams(dimension_semantics=("parallel",),
                                             vmem_limit_bytes=VMEM_LIMIT),
    )(xb, w, cos_t, sin_t)


SPLIT_PARTS = 1


def _parts(x, n=SPLIT_PARTS):
    out = []
    for _ in range(n):
        piece = x.astype(BF16)
        out.append(piece)
        x = x - piece.astype(F32)
    return out


def _mm(a_parts, b_parts, f=None):
    f = f or _dot
    order = max(len(a_parts), len(b_parts))
    acc = None
    for i, a in enumerate(a_parts):
        for j, b in enumerate(b_parts):
            if i + j < order:
                term = f(a, b)
                acc = term if acc is None else acc + term
    return acc


def _softplus(z):
    return jnp.maximum(z, 0.0) + jnp.log(1.0 + jnp.exp(-jnp.abs(z)))


def _sigmoid(z):
    return 1.0 / (1.0 + jnp.exp(-z))


def _rwkv_kernel(p_ref, mu_ref, w0_ref, a0_ref, wlora_ref, wg_ref, kk_ref, ka_ref, rk_ref,
                 lng_ref, lnb_ref, y_ref, carry_ref, state_ref):
    c = RWKV_CHUNK
    nbatch = p_ref.shape[0]

    @pl.when(pl.program_id(0) == 0)
    def _():
        carry_ref[...] = jnp.zeros_like(carry_ref)
        state_ref[...] = jnp.zeros_like(state_ref)

    ri = lax.broadcasted_iota(jnp.int32, (PAIR, PAIR), 0)
    ci = lax.broadcasted_iota(jnp.int32, (PAIR, PAIR), 1)
    head_sum = jnp.where((ri // HEAD_DIM) == (ci // HEAD_DIM), 1.0, 0.0).astype(F32)
    eye = jnp.where(ri == ci, 1.0, 0.0).astype(F32)
    strict_lower = ri > ci
    lower = ri >= ci
    rc = lax.broadcasted_iota(jnp.int32, (c, c), 0)
    cc = lax.broadcasted_iota(jnp.int32, (c, c), 1)
    cumsum_mat = jnp.where(rc >= cc, 1.0, 0.0).astype(F32)
    lane_p = lax.broadcasted_iota(jnp.int32, (c, PAIR), 1)
    head0 = lane_p < HEAD_DIM

    def stack(t):
        return jnp.concatenate([jnp.where(head0, t, 0.0), jnp.where(head0, 0.0, t)], axis=0)

    head_sum_b = [head_sum.astype(BF16)]
    cumsum_b = [cumsum_mat.astype(BF16)]

    chains = []
    for bi in range(nbatch):
        p = p_ref[bi]
        n_rows = p.shape[0]
        row = lax.broadcasted_iota(jnp.int32, p.shape, 0)
        prev = jnp.where(row == 0, carry_ref[bi, 0:1, :], pltpu.roll(p, 1, 0))
        carry_ref[bi] = jnp.broadcast_to(p[n_rows - 1:n_rows, :], carry_ref.shape[1:])
        sh = p + (prev - p) * mu_ref[...]
        r = sh[:, 0:WIDTH]
        k = sh[:, WIDTH:2 * WIDTH]
        v = sh[:, 2 * WIDTH:3 * WIDTH]
        lora = sh[:, 3 * WIDTH:3 * WIDTH + PAIR]
        hg = sh[:, 3 * WIDTH + PAIR:]
        lane_l = lax.broadcasted_iota(jnp.int32, lora.shape, 1)
        lora_act = jnp.where(lane_l < DECAY_LORA, jnp.tanh(lora), lora)
        wa = _dot(lora_act.astype(BF16), wlora_ref[...])
        w_log = -_softplus(-(w0_ref[...] + wa[:, :WIDTH])) - 0.5
        logw = -jnp.exp(w_log)
        a = _sigmoid(a0_ref[...] + wa[:, WIDTH:])
        g = _dot(_sigmoid(hg).astype(BF16), wg_ref[...])
        kkn = k * kk_ref[...]
        k2 = k * (1.0 + (a - 1.0) * ka_ref[...])
        for ck in range(n_rows // c):
            rs = slice(ck * c, (ck + 1) * c)
            cum_all = _mm(cumsum_b, _parts(logw[rs], 3))
            for pp in range(N_PAIRS):
                sl = slice(pp * PAIR, (pp + 1) * PAIR)
                chains.append(dict(bi=bi, pp=pp, ck=ck, rs=rs, sl=sl, r=r[rs, sl], k=k2[rs, sl], v=v[rs, sl],
                                   a=a[rs, sl], kk=kkn[rs, sl], lw=logw[rs, sl], cum=cum_all[:, sl],
                                   g=g[rs, sl]))

    for ch in chains:
        ch['ss'] = _mm(_parts(ch['kk'] * ch['kk'], 2), head_sum_b)
    for ch in chains:
        kap = ch['kk'] / jnp.maximum(jnp.sqrt(ch['ss']), 1e-12)
        cum = ch['cum']
        ch['pc'] = jnp.exp(cum[c - 1:c, :])
        inv = jnp.exp(-cum)
        rm = stack(ch['r'] * jnp.exp(cum))
        bm = stack(kap * jnp.exp(cum - ch['lw']))
        am = stack(-(kap * ch['a']) * inv)
        km = stack(ch['k'] * inv)
        ch.update(rm=rm, bm=bm, am=am, km=km, vm=stack(ch['v']))
    for ch in chains:
        ch['sb'] = _mm(_parts(jnp.concatenate([ch['bm'], ch['rm']], axis=0)),
                       _parts(jnp.concatenate([ch['am'], ch['km']], axis=0)), _nt)
    for ch in chains:
        sb = ch['sb']
        ch['la'] = jnp.where(strict_lower, sb[:2 * c, :2 * c], 0.0)
        ch['lk'] = jnp.where(strict_lower, sb[:2 * c, 2 * c:], 0.0)
        ch['ma'] = jnp.where(lower, sb[2 * c:, :2 * c], 0.0)
        ch['mk'] = jnp.where(lower, sb[2 * c:, 2 * c:], 0.0)
        ch['tinv'] = eye + ch['la']
        ch['lpow'] = ch['la']
    n = 2
    while n < c:
        for ch in chains:
            lp = _parts(ch['lpow'])
            ch['lpow'] = _mm(lp, lp)
        for ch in chains:
            ch['tinv'] = ch['tinv'] + _mm(_parts(ch['tinv']), _parts(ch['lpow']))
        n *= 2

    state = {(bi, pp): state_ref[bi, pp] for bi in range(nbatch) for pp in range(N_PAIRS)}
    for ck in sorted({ch['ck'] for ch in chains}):
        sub = [ch for ch in chains if ch['ck'] == ck]
        for ch in sub:
            ch['h0'] = state[(ch['bi'], ch['pp'])]
            ch['rhs'] = _mm(_parts(jnp.concatenate([ch['bm'], ch['lk']], axis=1)),
                            _parts(jnp.concatenate([ch['h0'], ch['vm']], axis=0)))
        for ch in sub:
            ch['u'] = _mm(_parts(ch['tinv']), _parts(ch['rhs']))
        for ch in sub:
            yst = _mm(_parts(jnp.concatenate([ch['rm'], ch['ma'], ch['mk']], axis=1)),
                      _parts(jnp.concatenate([ch['h0'], ch['u'], ch['vm']], axis=0)))
            ch['y'] = yst[:c] + yst[c:]
        for ch in sub:
            pc = ch['pc']
            pc_col = jnp.transpose(jnp.broadcast_to(pc, (PAIR, PAIR)))
            upd = _mm(_parts(jnp.concatenate([jnp.transpose(ch['am'] * pc), jnp.transpose(ch['km'] * pc)],
                                             axis=1)),
                      _parts(jnp.concatenate([ch['u'], ch['vm']], axis=0)))
            state[(ch['bi'], ch['pp'])] = ch['h0'] * pc_col + upd
    for (bi, pp), value in state.items():
        state_ref[bi, pp] = value
    for ch in chains:
        ch['mean'] = _mm(_parts(ch['y'], 2), head_sum_b) * (1.0 / HEAD_DIM)
        ch['bonus'] = _mm(_parts(ch['r'] * ch['k'] * rk_ref[:, ch['sl']], 2), head_sum_b) * ch['v']
    for ch in chains:
        yc = ch['y'] - ch['mean']
        ch['yc'] = yc
        ch['var'] = _mm(_parts(yc * yc, 2), head_sum_b) * (1.0 / HEAD_DIM)
    for ch in chains:
        sl = ch['sl']
        yn = ch['yc'] * lax.rsqrt(ch['var'] + GN_EPS) * lng_ref[:, sl] + lnb_ref[:, sl]
        y_ref[ch['bi'], ch['rs'], sl] = ((yn + ch['bonus']) * ch['g']).astype(y_ref.dtype)


def _rwkv(proj_rw, mu, w0, a0, wlora, wg, k_k, k_a, r_k, lnx_g, lnx_b):
    b, s, n = proj_rw.shape
    c = RWKV_CHUNK * RWKV_CHUNKS_PER_STEP
    vec = lambda width: pl.BlockSpec((1, width), lambda ci: (0, 0))
    return pl.pallas_call(
        _rwkv_kernel,
        grid=(s // c,),
        in_specs=[pl.BlockSpec((b, c, n), lambda ci: (0, ci, 0)),
                  vec(n), vec(WIDTH), vec(WIDTH),
                  pl.BlockSpec(wlora.shape, lambda ci: (0, 0)),
                  pl.BlockSpec(wg.shape, lambda ci: (0, 0)),
                  vec(WIDTH), vec(WIDTH), vec(WIDTH), vec(WIDTH), vec(WIDTH)],
        out_specs=pl.BlockSpec((b, c, WIDTH), lambda ci: (0, ci, 0)),
        out_shape=jax.ShapeDtypeStruct((b, s, WIDTH), BF16),
        scratch_shapes=[pltpu.VMEM((b, 8, n), F32),
                        pltpu.VMEM((b, N_PAIRS, PAIR, PAIR), F32)],
        compiler_params=pltpu.CompilerParams(dimension_semantics=("arbitrary",),
                                             vmem_limit_bytes=VMEM_LIMIT),
    )(proj_rw, mu, w0, a0, wlora, wg, k_k, k_a, r_k, lnx_g, lnx_b)


MOBA_HEADS_PER_STEP = 4
MOBA_VMEM_LIMIT = 60 * 1024 * 1024


def _key_group(nb):
    return min(KEY_GROUP, nb)


def _moba_kernel(qt_ref, k_ref, vt_ref, km_ref, o_ref, sa_ref, sb_ref, mxa_ref, mxb_ref, *, nb):
    bs = MOBA_BLOCK
    heads = range(qt_ref.shape[1])
    i = pl.program_id(2)
    blk = lax.broadcasted_iota(jnp.int32, (nb, bs), 0).astype(F32)

    q_sel, q_own = [], []
    for h in heads:
        qt = qt_ref[0, h]
        km = km_ref[0, h].astype(BF16)
        gate = jnp.where(blk < i.astype(F32), _dot(km, qt), NEG_INF)
        sel = jnp.zeros((nb, bs), F32)
        for _ in range(MOBA_TOPK):
            mx = jnp.max(gate, axis=0, keepdims=True)
            hit = (gate == mx) & (mx > NEG_INF)
            idx = jnp.min(jnp.where(hit, blk, float(nb)), axis=0, keepdims=True)
            pick = blk == idx
            sel = jnp.where(pick, 1.0, sel)
            gate = jnp.where(pick, NEG_INF, gate)
        bias = jnp.where(sel > 0.0, 0.0, MASK_BIAS)
        if nb < HEAD_DIM:
            bias = jnp.concatenate([bias, jnp.zeros((HEAD_DIM - nb, bs), F32)], axis=0)
        q_sel.append(jnp.concatenate([qt, bias.astype(BF16)], axis=0))
        q_own.append(jnp.concatenate([qt, jnp.zeros((HEAD_DIM, bs), BF16)], axis=0))

    span = _key_group(nb) * bs
    n_groups = lax.shift_right_logical(i * bs + span - 1, span.bit_length() - 1)

    def col_max(sc):
        return jnp.max(jnp.max(sc.reshape(span // bs, bs, bs), axis=0), axis=0, keepdims=True)

    even, odd = (sa_ref, mxa_ref), (sb_ref, mxb_ref)

    def put(bufs, g):
        off = pl.multiple_of(g * span, span)
        for h in heads:
            s_new = _dot(k_ref[0, h, pl.ds(off, span), :], q_sel[h])
            bufs[0][h] = s_new
            bufs[1][h] = col_max(s_new)

    def absorb(bufs, g, carry):
        off = pl.multiple_of(g * span, span)
        out = []
        for h in heads:
            m, acc = carry[2 * h], carry[2 * h + 1]
            m_new = jnp.maximum(m, bufs[1][h])
            p = jnp.exp2(bufs[0][h] - m_new)
            acc = jnp.exp2(m - m_new) * acc + _dot(vt_ref[0, h, :, pl.ds(off, span)], p.astype(BF16))
            out += [m_new, acc]
        return tuple(out)

    start = pl.multiple_of(i * bs, bs)
    s_own = [_dot(k_ref[0, h, pl.ds(start, bs), :], q_own[h]) for h in heads]
    put(even, 0)

    key_i = lax.broadcasted_iota(jnp.int32, (bs, bs), 0)
    qry_i = lax.broadcasted_iota(jnp.int32, (bs, bs), 1)
    carry = []
    for h in heads:
        s = jnp.where(key_i <= qry_i, s_own[h], NEG_INF)
        m = jnp.max(s, axis=0, keepdims=True)
        p = jnp.exp2(s - m)
        carry += [m, _dot(vt_ref[0, h, :, pl.ds(start, bs)], p.astype(BF16))]
    carry = tuple(carry)

    def pair(t, carry):
        put(odd, 2 * t + 1)
        carry = absorb(even, 2 * t, carry)
        put(even, 2 * t + 2)
        return absorb(odd, 2 * t + 1, carry)

    n_pairs = lax.shift_right_logical(jnp.maximum(n_groups - 1, 0), 1)
    carry = lax.fori_loop(0, n_pairs, pair, carry)
    last_even = 2 * n_pairs

    def tail_two(*carry):
        put(odd, last_even + 1)
        return absorb(odd, last_even + 1, absorb(even, last_even, carry))

    def tail_one(*carry):
        return absorb(even, last_even, carry)

    def tail(*carry):
        return lax.cond(n_groups - last_even == 2, tail_two, tail_one, *carry)

    carry = lax.cond(n_groups > 0, tail, lambda *carry: carry, *carry)
    out_t = [carry[2 * h + 1][:HEAD_DIM] / carry[2 * h + 1][HEAD_DIM:HEAD_DIM + 1] for h in heads]
    o_ref[0] = jnp.transpose(jnp.concatenate(out_t, axis=0)).astype(o_ref.dtype)


def _moba(qt, kaug, vaug, kmean):
    b, nh, _, s = qt.shape
    nb = s // MOBA_BLOCK
    group = _key_group(nb)
    hp = MOBA_HEADS_PER_STEP
    assert nb % group == 0 and nh % hp == 0 and (hp * HEAD_DIM) % PAIR == 0
    return pl.pallas_call(
        functools.partial(_moba_kernel, nb=nb),
        grid=(b, nh // hp, nb),
        in_specs=[pl.BlockSpec((1, hp, HEAD_DIM, MOBA_BLOCK), lambda bi, hi, qi: (bi, hi, 0, qi)),
                  pl.BlockSpec((1, hp, s, PAIR), lambda bi, hi, qi: (bi, hi, 0, 0),
                               pipeline_mode=pl.Buffered(1)),
                  pl.BlockSpec((1, hp, VALUE_ROWS, s), lambda bi, hi, qi: (bi, hi, 0, 0),
                               pipeline_mode=pl.Buffered(1)),
                  pl.BlockSpec((1, hp, nb, HEAD_DIM), lambda bi, hi, qi: (bi, hi, 0, 0))],
        out_specs=pl.BlockSpec((1, MOBA_BLOCK, hp * HEAD_DIM), lambda bi, hi, qi: (bi, qi, hi)),
        out_shape=jax.ShapeDtypeStruct((b, s, nh * HEAD_DIM), BF16),
        scratch_shapes=[pltpu.VMEM((hp, group * MOBA_BLOCK, MOBA_BLOCK), F32),
                        pltpu.VMEM((hp, group * MOBA_BLOCK, MOBA_BLOCK), F32),
                        pltpu.VMEM((hp, 1, MOBA_BLOCK), F32),
                        pltpu.VMEM((hp, 1, MOBA_BLOCK), F32)],
        compiler_params=pltpu.CompilerParams(
            dimension_semantics=("parallel", "parallel", "arbitrary"),
            vmem_limit_bytes=MOBA_VMEM_LIMIT),
    )(qt, kaug, vaug, kmean)


def _layer_norm(z, g, b):
    mu = jnp.mean(z, axis=1, keepdims=True)
    zc = z - mu
    var = jnp.mean(zc * zc, axis=1, keepdims=True)
    return zc * lax.rsqrt(var + LN_EPS) * g + b


def _merge_kernel(x_ref, ya_ref, yr_ref, wgate_ref, wab_ref, wrb_ref, wout_ref, g1_ref, b1_ref,
                  wr_hi_ref, wr_lo_ref, br_ref, h_ref, idx_ref, gate_ref, *, alpha):
    x = x_ref[...]
    d = x.shape[1]
    gates = _sigmoid(_dot(x.astype(BF16), wgate_ref[...]))
    mixed = (gates[:, :d] * _dot(ya_ref[...], wab_ref[...])
             + gates[:, d:] * _dot(yr_ref[...], wrb_ref[...]))
    h = _layer_norm(alpha * x + _dot(mixed.astype(BF16), wout_ref[...]), g1_ref[...], b1_ref[...])
    h_ref[...] = h

    logits = _mm(_parts(h, 2), [wr_hi_ref[...], wr_lo_ref[...]]) + br_ref[...]
    col = lax.broadcasted_iota(jnp.int32, logits.shape, 1).astype(F32)
    idx_out = jnp.zeros(logits.shape, F32)
    val_out = jnp.zeros(logits.shape, F32)
    top = None
    denom = None
    for t in range(TOP_K):
        mx = jnp.max(logits, axis=1, keepdims=True)
        idx = jnp.min(jnp.where(logits == mx, col, float(ROUTER_PAD)), axis=1, keepdims=True)
        if t == 0:
            top = mx
        e = jnp.exp(mx - top)
        denom = e if t == 0 else denom + e
        idx_out = jnp.where(col == float(t), idx, idx_out)
        val_out = jnp.where(col == float(t), e, val_out)
        logits = jnp.where(col == idx, NEG_INF, logits)
    idx_ref[...] = idx_out.astype(jnp.int32)
    gate_ref[...] = val_out / denom


def _merge(x2, ya, yr, wgate, wab, wrb, wout, g1, b1, wr, br, alpha, tm=512):
    wr_hi, wr_lo = _parts(wr, 2)
    t, d = x2.shape
    row = lambda width: pl.BlockSpec((tm, width), lambda i: (i, 0))
    full = lambda arr: pl.BlockSpec(arr.shape, lambda i: (0, 0))
    return pl.pallas_call(
        functools.partial(_merge_kernel, alpha=alpha),
        grid=(t // tm,),
        in_specs=[row(d), row(WIDTH), row(WIDTH), full(wgate), full(wab), full(wrb), full(wout),
                  full(g1), full(b1), full(wr_hi), full(wr_lo), full(br)],
        out_specs=[row(d), row(ROUTER_PAD), row(ROUTER_PAD)],
        out_shape=[jax.ShapeDtypeStruct((t, d), F32),
                   jax.ShapeDtypeStruct((t, ROUTER_PAD), jnp.int32),
                   jax.ShapeDtypeStruct((t, ROUTER_PAD), F32)],
        compiler_params=pltpu.CompilerParams(dimension_semantics=("parallel",),
                                             vmem_limit_bytes=VMEM_LIMIT),
    )(x2, ya, yr, wgate, wab, wrb, wout, g1, b1, wr_hi, wr_lo, br)


DEINTERLEAVE_GROUP = 256


def _deinterleave_kernel(w_ref, perm_ref, o_ref):
    g = DEINTERLEAVE_GROUP
    n = w_ref.shape[2]
    half = n // 2
    for c in range(n // g):
        res = _dot(w_ref[0, :, c * g:(c + 1) * g].astype(BF16), perm_ref[...])
        o_ref[0, :, c * g // 2:(c + 1) * g // 2] = res[:, :g // 2].astype(BF16)
        o_ref[0, :, half + c * g // 2:half + (c + 1) * g // 2] = res[:, g // 2:].astype(BF16)


def _deinterleave(w1, tr=256):
    e, d, n = w1.shape
    g = DEINTERLEAVE_GROUP
    src = jnp.arange(g)
    dst = jnp.where(src % 2 == 0, src // 2, g // 2 + src // 2)
    perm = (dst[:, None] == jnp.arange(g)[None, :]).astype(BF16)
    return pl.pallas_call(
        _deinterleave_kernel,
        grid=(e, d // tr),
        in_specs=[pl.BlockSpec((1, tr, n), lambda ei, ri: (ei, ri, 0)),
                  pl.BlockSpec((g, g), lambda ei, ri: (0, 0))],
        out_specs=pl.BlockSpec((1, tr, n), lambda ei, ri: (ei, ri, 0)),
        out_shape=jax.ShapeDtypeStruct((e, d, n), BF16),
        compiler_params=pltpu.CompilerParams(dimension_semantics=("parallel", "parallel"),
                                             vmem_limit_bytes=VMEM_LIMIT),
    )(w1, perm)

def _row_copy(src_hbm, dst_buf, sem, src_row, slot, dst_row):
    return pltpu.make_async_copy(src_hbm.at[pl.ds(src_row, 1), :],
                                 dst_buf.at[slot, pl.ds(dst_row, 1), :],
                                 sem.at[slot])


GATHER_UNROLL = 16


def _gather_rows(src_hbm, dst_buf, sem, idx_ref, slot, n_rows):
    def issue(run, _):
        base = pl.multiple_of(run * GATHER_UNROLL, GATHER_UNROLL)
        for j in range(GATHER_UNROLL):
            _row_copy(src_hbm, dst_buf, sem, idx_ref[0, 0, base + j], slot, base + j).start(priority=j % 2)
        return 0
    lax.fori_loop(0, n_rows // GATHER_UNROLL, issue, 0)


def _wait_rows(src_hbm, dst_buf, sem, slot, n_rows):
    pltpu.make_async_copy(src_hbm.at[pl.ds(0, n_rows), :], dst_buf.at[slot], sem.at[slot]).wait()


ROUTE_ROWS = 512
DISPATCH_ROWS = 256


def _route_kernel(idx_ref, dest_ref, cnt_ref, run_ref, start_ref, *, tm):
    phase = pl.program_id(0)
    i = pl.program_id(1)
    idx = idx_ref[...]
    lane = lax.broadcasted_iota(jnp.int32, idx.shape, 1)
    hot = [jnp.where(lane == idx[:, k:k + 1], 1.0, 0.0) for k in range(TOP_K)]
    cnt = hot[0] + hot[1] + hot[2] + hot[3]
    tile_total = jnp.sum(cnt, axis=0, keepdims=True)

    @pl.when((phase == 0) & (i == 0))
    def _():
        run_ref[...] = jnp.zeros_like(run_ref)

    @pl.when(phase == 0)
    def _():
        run_ref[...] += tile_total
        dest_ref[...] = jnp.zeros_like(dest_ref)

    @pl.when((phase == 1) & (i == 0))
    def _():
        counts = run_ref[...]
        padded = jnp.floor((counts + (EXPERT_ROWS - 1)) * (1.0 / EXPERT_ROWS)) * EXPERT_ROWS
        ri = lax.broadcasted_iota(jnp.int32, (ROUTER_PAD, ROUTER_PAD), 0)
        ci = lax.broadcasted_iota(jnp.int32, (ROUTER_PAD, ROUTER_PAD), 1)
        before = jnp.where(ri < ci, 1.0, 0.0).astype(BF16)
        start = _mm(_parts(jnp.broadcast_to(padded, (8, ROUTER_PAD)), 3), [before])
        start_ref[...] = start[0:1]
        cnt_ref[...] = counts
        run_ref[...] = jnp.zeros_like(run_ref)

    @pl.when(phase == 1)
    def _():
        rt = lax.broadcasted_iota(jnp.int32, (tm, tm), 0)
        ct = lax.broadcasted_iota(jnp.int32, (tm, tm), 1)
        earlier = jnp.where(ct < rt, 1.0, 0.0).astype(BF16)
        pos = start_ref[...] + run_ref[...] + _dot(earlier, cnt.astype(BF16))
        dest = jnp.zeros(idx.shape, F32)
        for k in range(TOP_K):
            d_k = jnp.sum(hot[k] * pos, axis=1, keepdims=True)
            dest = jnp.where(lane == k, d_k, dest)
            pos = pos + hot[k]
        dest_ref[...] = dest.astype(jnp.int32)
        run_ref[...] += tile_total


def _route(idx_pad):
    t = idx_pad.shape[0]
    tm = ROUTE_ROWS
    return pl.pallas_call(
        functools.partial(_route_kernel, tm=tm),
        grid=(2, t // tm),
        in_specs=[pl.BlockSpec((tm, ROUTER_PAD), lambda ph, i: (i, 0))],
        out_specs=[pl.BlockSpec((tm, ROUTER_PAD), lambda ph, i: (ph * i, 0)),
                   pl.BlockSpec((1, ROUTER_PAD), lambda ph, i: (0, 0))],
        out_shape=[jax.ShapeDtypeStruct((t, ROUTER_PAD), jnp.int32),
                   jax.ShapeDtypeStruct((1, ROUTER_PAD), F32)],
        scratch_shapes=[pltpu.VMEM((1, ROUTER_PAD), F32), pltpu.VMEM((1, ROUTER_PAD), F32)],
        compiler_params=pltpu.CompilerParams(dimension_semantics=("arbitrary", "arbitrary"),
                                             vmem_limit_bytes=VMEM_LIMIT),
    )(idx_pad)


def _dispatch_kernel(pad_end_ref, dest_ref, h_ref, xs_hbm, zero_ref, sem, *, tm):
    @pl.when(pl.program_id(0) == 0)
    def _():
        zero_ref[...] = jnp.zeros_like(zero_ref)

        def last_block(e):
            end = pad_end_ref[e]
            start = pl.multiple_of(jnp.maximum(end - EXPERT_ROWS, 0), EXPERT_ROWS)
            return pltpu.make_async_copy(zero_ref, xs_hbm.at[pl.ds(start, EXPERT_ROWS), :], sem.at[0])

        for e in range(N_EXPERTS):
            last_block(e).start()
        for e in range(N_EXPERTS):
            last_block(e).wait()

        def unused_block(b):
            start = pl.multiple_of(b * EXPERT_ROWS, EXPERT_ROWS)
            return pltpu.make_async_copy(zero_ref, xs_hbm.at[pl.ds(start, EXPERT_ROWS), :], sem.at[0])

        first_unused = lax.shift_right_logical(pad_end_ref[N_EXPERTS - 1], EXPERT_ROWS.bit_length() - 1)
        n_blocks = xs_hbm.shape[0] // EXPERT_ROWS

        def start_one(b, _):
            unused_block(b).start()
            return 0

        def wait_one(b, _):
            unused_block(b).wait()
            return 0

        lax.fori_loop(first_unused, n_blocks, start_one, 0)
        lax.fori_loop(first_unused, n_blocks, wait_one, 0)

    run_len = 8

    def issue(run, _):
        base = pl.multiple_of(run * run_len, run_len)
        for j in range(run_len):
            for k in range(TOP_K):
                pltpu.make_async_copy(h_ref.at[pl.ds(base + j, 1), :],
                                      xs_hbm.at[pl.ds(dest_ref[0, 0, TOP_K * (base + j) + k], 1), :],
                                      sem.at[0]).start(priority=k % 2)
        return 0

    lax.fori_loop(0, tm // run_len, issue, 0)
    for _ in range(TOP_K):
        pltpu.make_async_copy(h_ref, xs_hbm.at[pl.ds(0, tm), :], sem.at[0]).wait()


def _dispatch(pad_end, dest, h, n_rows):
    t, d = h.shape
    tm = DISPATCH_ROWS
    n = TOP_K * tm
    grid_spec = pltpu.PrefetchScalarGridSpec(
        num_scalar_prefetch=1,
        grid=(t // tm,),
        in_specs=[pl.BlockSpec((1, 1, n), lambda i, pe: (i, 0, 0), memory_space=pltpu.SMEM),
                  pl.BlockSpec((tm, d), lambda i, pe: (i, 0))],
        out_specs=pl.BlockSpec(memory_space=pl.ANY),
        scratch_shapes=[pltpu.VMEM((EXPERT_ROWS, d), F32), pltpu.SemaphoreType.DMA((1,))],
    )
    return pl.pallas_call(
        functools.partial(_dispatch_kernel, tm=tm),
        grid_spec=grid_spec,
        out_shape=jax.ShapeDtypeStruct((n_rows, d), F32),
        compiler_params=pltpu.CompilerParams(dimension_semantics=("arbitrary",),
                                             vmem_limit_bytes=VMEM_LIMIT),
    )(pad_end, dest.reshape(t // tm, 1, n), h)


def _expert_kernel(be_ref, nact_ref, x_ref, w1_ref, b1_ref, w2_ref, b2_ref, y_ref):
    del be_ref

    @pl.when(pl.program_id(0) < nact_ref[0])
    def _():
        f = w2_ref.shape[1]
        hid = _dot(x_ref[...].astype(BF16), w1_ref[0]) + b1_ref[0]
        x_glu = jnp.minimum(hid[:, :f], SWIGLU_LIMIT)
        x_lin = jnp.clip(hid[:, f:], -SWIGLU_LIMIT, SWIGLU_LIMIT)
        act = x_glu * _sigmoid(SWIGLU_ALPHA * x_glu) * (x_lin + 1.0)
        y_ref[...] = _dot(act.astype(BF16), w2_ref[0]) + b2_ref[0]

    @pl.when(pl.program_id(0) >= nact_ref[0])
    def _():
        y_ref[...] = jnp.zeros_like(y_ref)


def _experts(block_expert, n_active, x_sorted, w1p, b1p, w2b, b2):
    n_rows, d = x_sorted.shape
    rows = EXPERT_ROWS
    nblk = n_rows // rows
    f = w2b.shape[1]
    grid_spec = pltpu.PrefetchScalarGridSpec(
        num_scalar_prefetch=2,
        grid=(nblk,),
        in_specs=[
            pl.BlockSpec((rows, d), lambda b, be, na: (jnp.minimum(b, jnp.maximum(na[0] - 1, 0)), 0)),
            pl.BlockSpec((1, d, 2 * f), lambda b, be, na: (be[b], 0, 0)),
            pl.BlockSpec((1, 1, 2 * f), lambda b, be, na: (be[b], 0, 0)),
            pl.BlockSpec((1, f, d), lambda b, be, na: (be[b], 0, 0)),
            pl.BlockSpec((1, 1, d), lambda b, be, na: (be[b], 0, 0)),
        ],
        out_specs=pl.BlockSpec((rows, d), lambda b, be, na: (b, 0)),
    )
    return pl.pallas_call(
        _expert_kernel,
        grid_spec=grid_spec,
        out_shape=jax.ShapeDtypeStruct((n_rows, d), F32),
        compiler_params=pltpu.CompilerParams(dimension_semantics=("arbitrary",),
                                             vmem_limit_bytes=VMEM_LIMIT),
    )(block_expert, n_active, x_sorted, w1p, b1p, w2b, b2)


def _combine_kernel(dest_ref, dest_next_ref, h_ref, gate_ref, g2_ref, b2_ref, y_hbm, o_ref, ybuf, sem,
                    *, alpha, nsteps):
    tm = COMBINE_ROWS
    n = TOP_K * tm
    s = pl.program_id(0)
    slot = s % 2

    @pl.when(s == 0)
    def _():
        _gather_rows(y_hbm, ybuf, sem, dest_ref, 0, n)

    @pl.when(s + 1 < nsteps)
    def _():
        _gather_rows(y_hbm, ybuf, sem, dest_next_ref, 1 - slot, n)

    _wait_rows(y_hbm, ybuf, sem, slot, n)
    gates = gate_ref[...]
    moe = gates[:, 0:1] * ybuf[slot, 0:tm, :]
    for kk in range(1, TOP_K):
        moe = moe + gates[:, kk:kk + 1] * ybuf[slot, kk * tm:(kk + 1) * tm, :]
    o_ref[...] = _layer_norm(alpha * h_ref[...] + moe, g2_ref[...], b2_ref[...])


def _combine(dest_km, h, gate_pad, g2, b2, y_sorted, alpha):
    t, d = h.shape
    tm = COMBINE_ROWS
    nsteps = t // tm
    n = TOP_K * tm
    return pl.pallas_call(
        functools.partial(_combine_kernel, alpha=alpha, nsteps=nsteps),
        grid=(nsteps,),
        in_specs=[
            pl.BlockSpec((1, 1, n), lambda s: (s, 0, 0), memory_space=pltpu.SMEM),
            pl.BlockSpec((1, 1, n), lambda s: (jnp.minimum(s + 1, nsteps - 1), 0, 0),
                         memory_space=pltpu.SMEM),
            pl.BlockSpec((tm, d), lambda s: (s, 0)),
            pl.BlockSpec((tm, ROUTER_PAD), lambda s: (s, 0)),
            pl.BlockSpec((1, d), lambda s: (0, 0)),
            pl.BlockSpec((1, d), lambda s: (0, 0)),
            pl.BlockSpec(memory_space=pl.ANY),
        ],
        out_specs=pl.BlockSpec((tm, d), lambda s: (s, 0)),
        out_shape=jax.ShapeDtypeStruct((t, d), F32),
        scratch_shapes=[pltpu.VMEM((2, n, d), F32), pltpu.SemaphoreType.DMA((2,))],
        compiler_params=pltpu.CompilerParams(dimension_semantics=("arbitrary",),
                                             vmem_limit_bytes=VMEM_LIMIT),
    )(dest_km, dest_km, h, gate_pad, g2, b2, y_sorted)


def _block_tables(counts, nblk):
    rows = EXPERT_ROWS
    counts = counts.astype(jnp.int32)
    padded = (counts + rows - 1) // rows * rows
    pad_ends = jnp.cumsum(padded)
    block_start = jnp.arange(nblk, dtype=jnp.int32) * rows
    block_expert = jnp.sum((pad_ends[None, :] <= block_start[:, None]).astype(jnp.int32), axis=1)
    block_expert = jnp.minimum(block_expert, N_EXPERTS - 1).astype(jnp.int32)
    n_active = (pad_ends[-1] // rows).astype(jnp.int32).reshape(1)
    return block_expert, n_active, pad_ends.astype(jnp.int32)


def _layer(x, w_in, mu_shift, w0, w_decay_up, a0, w_aaa_up, w_gate_up, k_k, k_a, r_k, lnx_g, lnx_b,
           w_attn_br, w_rwkv_br, w_out, ln1_g, ln1_b, w_router, b_router, w1, b1, w2, b2, ln2_g, ln2_b,
           alpha):
    b, s, d = x.shape
    t = b * s
    x2 = x.reshape(t, d)
    xb = x2.astype(BF16)
    row = lambda vec: vec.reshape(1, -1)

    off_q = RW_COLS
    off_gate = off_q + 3 * WIDTH
    pad_cols = RW_COLS_PAD - RW_COLS
    w_rw = jnp.pad(w_in[:, :RW_COLS], ((0, 0), (0, pad_cols))).astype(BF16)
    mu = jnp.pad(mu_shift, (0, pad_cols)).reshape(1, -1)
    w_qkv = w_in[:, off_q:off_gate].astype(BF16)
    w_gate = w_in[:, off_gate:].astype(BF16)

    half = HEAD_DIM // 2
    inv_freq = ROPE_THETA ** (-jnp.arange(half, dtype=F32) / half)
    ang = jnp.arange(s, dtype=F32)[:, None] * inv_freq[None, :]
    cos, sin = jnp.cos(ang), jnp.sin(ang)
    cos_t = jnp.concatenate([cos, cos, cos, cos], axis=1)
    sin_t = jnp.concatenate([-sin, sin, -sin, sin], axis=1)

    proj_rw = _project(xb, w_rw).reshape(b, s, RW_COLS_PAD)
    qt, kaug, vaug, kmean = _project_qkv(xb, w_qkv, cos_t, sin_t, b, s)
    kmean = kmean.reshape(b, s // MOBA_BLOCK, N_HEADS, HEAD_DIM).transpose(0, 2, 1, 3)

    zeros = jnp.zeros((DECAY_LORA, WIDTH), F32)
    w_lora = jnp.concatenate([jnp.concatenate([w_decay_up, zeros], axis=1),
                              jnp.concatenate([zeros, w_aaa_up], axis=1)], axis=0).astype(BF16)
    w_g = jnp.pad(w_gate_up, ((0, GATE_LORA_PAD - GATE_LORA), (0, 0))).astype(BF16)
    y_rwkv = _rwkv(proj_rw, mu, row(w0), row(a0), w_lora, w_g, row(k_k), row(k_a), row(r_k),
                   row(lnx_g), row(lnx_b))
    y_attn = _moba(qt, kaug, vaug, kmean)

    w_r = jnp.pad(w_router, ((0, 0), (0, ROUTER_PAD - N_EXPERTS)))
    b_r = jnp.pad(b_router, (0, ROUTER_PAD - N_EXPERTS), constant_values=NEG_INF).reshape(1, -1)
    h, idx_pad, gate_pad = _merge(x2, y_attn.reshape(t, WIDTH), y_rwkv.reshape(t, WIDTH), w_gate,
                                  w_attn_br.astype(BF16), w_rwkv_br.astype(BF16), w_out.astype(BF16),
                                  row(ln1_g), row(ln1_b), w_r, b_r, alpha)

    dest_pad, counts = _route(idx_pad)
    dest = dest_pad[:, :TOP_K]
    nblk = t * TOP_K // EXPERT_ROWS + N_EXPERTS
    block_expert, n_active, pad_end = _block_tables(counts[0, :N_EXPERTS], nblk)
    x_sorted = _dispatch(pad_end, dest, h, nblk * EXPERT_ROWS)
    w1p = _deinterleave(w1)
    b1p = jnp.concatenate([b1[:, 0::2], b1[:, 1::2]], axis=1)[:, None, :]
    y_sorted = _experts(block_expert, n_active, x_sorted, w1p, b1p, w2.astype(BF16), b2[:, None, :])

    tm = COMBINE_ROWS
    dest_km = dest.reshape(t // tm, tm, TOP_K).transpose(0, 2, 1).reshape(t // tm, 1, TOP_K * tm)
    out = _combine(dest_km, h, gate_pad, row(ln2_g), row(ln2_b), y_sorted, alpha)
    return out.reshape(b, s, d)


def kernel(x, w_in, mu_shift, w0, w_decay_up, a0, w_aaa_up, w_gate_up, k_k, k_a, r_k, lnx_g, lnx_b,
           w_attn_br, w_rwkv_br, w_out, ln1_g, ln1_b, w_router, b_router, w1, b1, w2, b2, ln2_g, ln2_b):
    depth = w_in.shape[0]
    alpha = (2 * depth) ** 0.25
    for l in range(depth):
        x = _layer(x, w_in[l], mu_shift[l], w0[l], w_decay_up[l], a0[l], w_aaa_up[l], w_gate_up[l],
                   k_k[l], k_a[l], r_k[l].reshape(-1), lnx_g[l], lnx_b[l], w_attn_br[l], w_rwkv_br[l],
                   w_out[l], ln1_g[l], ln1_b[l], w_router[l], b_router[l], w1[l], b1[l], w2[l], b2[l],
                   ln2_g[l], ln2_b[l], alpha)
    return x
```

```python
import functools

import jax
import jax.numpy as jnp
from jax import lax
from jax.experimental import pallas as pl
from jax.experimental.pallas import tpu as pltpu

F32 = jnp.float32
BF16 = jnp.bfloat16
HI = lax.Precision.HIGHEST

HEAD_DIM = 64
N_HEADS = 8
WIDTH = N_HEADS * HEAD_DIM
PAIR = 2 * HEAD_DIM
N_PAIRS = N_HEADS // 2
MOBA_BLOCK = 256
MOBA_TOPK = 3
KEY_GROUP = 8
VALUE_ROWS = 2 * HEAD_DIM
ROPE_THETA = 10000.0
DECAY_LORA = 64
AAA_LORA = 64
GATE_LORA = 160
GATE_LORA_PAD = 256
RW_COLS = 3 * WIDTH + DECAY_LORA + AAA_LORA + GATE_LORA
RW_COLS_PAD = 3 * WIDTH + DECAY_LORA + AAA_LORA + GATE_LORA_PAD
GN_EPS = 64e-5
LN_EPS = 1e-5
N_EXPERTS = 32
TOP_K = 4
ROUTER_PAD = 128
SWIGLU_ALPHA = 1.702
SWIGLU_LIMIT = 7.0
RWKV_CHUNK = 64
RWKV_CHUNKS_PER_STEP = 2
EXPERT_ROWS = 512
COMBINE_ROWS = 256
VMEM_LIMIT = 48 * 1024 * 1024

NEG_INF = float("-inf")
LOG2_E = 1.4426950408889634
MASK_BIAS = -1e30


def _nt(a, b, precision=None):
    return lax.dot_general(a, b, (((1,), (1,)), ((), ())), precision=precision,
                           preferred_element_type=F32)


def _dot(a, b, precision=None):
    return jnp.dot(a, b, precision=precision, preferred_element_type=F32)


def _matmul_kernel(x_ref, w_ref, o_ref):
    o_ref[...] = _dot(x_ref[...], w_ref[...])


def _project(xb, w, tm=512):
    t, d = xb.shape
    n = w.shape[1]
    return pl.pallas_call(
        _matmul_kernel,
        grid=(t // tm,),
        in_specs=[pl.BlockSpec((tm, d), lambda i: (i, 0)),
                  pl.BlockSpec((d, n), lambda i: (0, 0))],
        out_specs=pl.BlockSpec((tm, n), lambda i: (i, 0)),
        out_shape=jax.ShapeDtypeStruct((t, n), F32),
        compiler_params=pltpu.CompilerParams(dimension_semantics=("parallel",),
                                             vmem_limit_bytes=VMEM_LIMIT),
    )(xb, w)


def _qkv_kernel(x_ref, w_ref, cos_ref, sin_ref, qt_ref, kaug_ref, vaug_ref, km_ref, *, tm, steps_per_seq):
    acc = _dot(x_ref[...], w_ref[...])
    cos = jnp.concatenate([cos_ref[...]] * (WIDTH // PAIR), axis=1)
    sin = jnp.concatenate([sin_ref[...]] * (WIDTH // PAIR), axis=1)
    lane = lax.broadcasted_iota(jnp.int32, (tm, WIDTH), 1)
    first_half = (lane & (HEAD_DIM // 2)) == 0

    def rope(t):
        partner = jnp.where(first_half, pltpu.roll(t, WIDTH - HEAD_DIM // 2, 1),
                            pltpu.roll(t, HEAD_DIM // 2, 1))
        return t * cos + partner * sin

    q = rope(acc[:, :WIDTH]) * (HEAD_DIM ** -0.5 * LOG2_E)
    k = rope(acc[:, WIDTH:2 * WIDTH])
    v = acc[:, 2 * WIDTH:]
    for j in range(tm // MOBA_BLOCK):
        km_ref[0, j:j + 1, :] = jnp.mean(k[j * MOBA_BLOCK:(j + 1) * MOBA_BLOCK], axis=0, keepdims=True)

    lane_p = lax.broadcasted_iota(jnp.int32, (tm, PAIR), 1)
    row_p = lax.broadcasted_iota(jnp.int32, (tm, PAIR), 0)
    first_block = (pl.program_id(0) % steps_per_seq) * (tm // MOBA_BLOCK)
    row_block = lax.shift_right_logical(row_p, MOBA_BLOCK.bit_length() - 1)
    block_tag = jnp.where(lane_p - HEAD_DIM == first_block + row_block, 1.0, 0.0)
    ones = jnp.ones((VALUE_ROWS - HEAD_DIM, tm), F32)
    for pp in range(N_PAIRS):
        sl = slice(pp * PAIR, (pp + 1) * PAIR)
        q_t = jnp.transpose(q[:, sl])
        v_t = jnp.transpose(v[:, sl])
        k_p = k[:, sl]
        k_sw = pltpu.roll(k_p, HEAD_DIM, 1)
        for h, k_h in ((0, k_p), (1, k_sw)):
            rows = slice(h * HEAD_DIM, (h + 1) * HEAD_DIM)
            qt_ref[0, 2 * pp + h] = q_t[rows].astype(BF16)
            kaug_ref[0, 2 * pp + h] = jnp.where(lane_p < HEAD_DIM, k_h, block_tag).astype(BF16)
            vaug_ref[0, 2 * pp + h] = jnp.concatenate([v_t[rows], ones], axis=0).astype(BF16)


def _project_qkv(xb, w, cos_t, sin_t, batch, seq, tm=512):
    t, d = xb.shape
    steps_per_seq = seq // tm
    assert seq // MOBA_BLOCK <= HEAD_DIM, "block one-hot tags must fit the spare lanes of a head"
    tab_spec = pl.BlockSpec((tm, PAIR), lambda i: (i % steps_per_seq, 0))
    return pl.pallas_call(
        functools.partial(_qkv_kernel, tm=tm, steps_per_seq=steps_per_seq),
        grid=(t // tm,),
        in_specs=[pl.BlockSpec((tm, d), lambda i: (i, 0)),
                  pl.BlockSpec((d, 3 * WIDTH), lambda i: (0, 0)),
                  tab_spec, tab_spec],
        out_specs=[
            pl.BlockSpec((1, N_HEADS, HEAD_DIM, tm), lambda i: (i // steps_per_seq, 0, 0, i % steps_per_seq)),
            pl.BlockSpec((1, N_HEADS, tm, PAIR), lambda i: (i // steps_per_seq, 0, i % steps_per_seq, 0)),
            pl.BlockSpec((1, N_HEADS, VALUE_ROWS, tm),
                         lambda i: (i // steps_per_seq, 0, 0, i % steps_per_seq)),
            pl.BlockSpec((1, tm // MOBA_BLOCK, WIDTH), lambda i: (i, 0, 0))],
        out_shape=[jax.ShapeDtypeStruct((batch, N_HEADS, HEAD_DIM, seq), BF16),
                   jax.ShapeDtypeStruct((batch, N_HEADS, seq, PAIR), BF16),
                   jax.ShapeDtypeStruct((batch, N_HEADS, VALUE_ROWS, seq), BF16),
                   jax.ShapeDtypeStruct((t // tm, tm // MOBA_BLOCK, WIDTH), F32)],
        compiler_params=pltpu.CompilerParams(dimension_semantics=("parallel",),
                                             vmem_limit_bytes=VMEM_LIMIT),
    )(xb, w, cos_t, sin_t)


SPLIT_PARTS = 1


def _parts(x, n=SPLIT_PARTS):
    out = []
    for _ in range(n):
        piece = x.astype(BF16)
        out.append(piece)
        x = x - piece.astype(F32)
    return out


def _mm(a_parts, b_parts, f=None):
    f = f or _dot
    order = max(len(a_parts), len(b_parts))
    acc = None
    for i, a in enumerate(a_parts):
        for j, b in enumerate(b_parts):
            if i + j < order:
                term = f(a, b)
                acc = term if acc is None else acc + term
    return acc


def _softplus(z):
    return jnp.maximum(z, 0.0) + jnp.log(1.0 + jnp.exp(-jnp.abs(z)))


def _sigmoid(z):
    return 1.0 / (1.0 + jnp.exp(-z))


def _rwkv_kernel(p_ref, mu_ref, w0_ref, a0_ref, wlora_ref, wg_ref, kk_ref, ka_ref, rk_ref,
                 lng_ref, lnb_ref, y_ref, carry_ref, state_ref):
    c = RWKV_CHUNK
    nbatch = p_ref.shape[0]

    @pl.when(pl.program_id(0) == 0)
    def _():
        carry_ref[...] = jnp.zeros_like(carry_ref)
        state_ref[...] = jnp.zeros_like(state_ref)

    ri = lax.broadcasted_iota(jnp.int32, (PAIR, PAIR), 0)
    ci = lax.broadcasted_iota(jnp.int32, (PAIR, PAIR), 1)
    head_sum = jnp.where((ri // HEAD_DIM) == (ci // HEAD_DIM), 1.0, 0.0).astype(F32)
    eye = jnp.where(ri == ci, 1.0, 0.0).astype(F32)
    strict_lower = ri > ci
    lower = ri >= ci
    rc = lax.broadcasted_iota(jnp.int32, (c, c), 0)
    cc = lax.broadcasted_iota(jnp.int32, (c, c), 1)
    cumsum_mat = jnp.where(rc >= cc, 1.0, 0.0).astype(F32)
    lane_p = lax.broadcasted_iota(jnp.int32, (c, PAIR), 1)
    head0 = lane_p < HEAD_DIM

    def stack(t):
        return jnp.concatenate([jnp.where(head0, t, 0.0), jnp.where(head0, 0.0, t)], axis=0)

    head_sum_b = [head_sum.astype(BF16)]
    cumsum_b = [cumsum_mat.astype(BF16)]

    chains = []
    for bi in range(nbatch):
        p = p_ref[bi]
        n_rows = p.shape[0]
        row = lax.broadcasted_iota(jnp.int32, p.shape, 0)
        prev = jnp.where(row == 0, carry_ref[bi, 0:1, :], pltpu.roll(p, 1, 0))
        carry_ref[bi] = jnp.broadcast_to(p[n_rows - 1:n_rows, :], carry_ref.shape[1:])
        sh = p + (prev - p) * mu_ref[...]
        r = sh[:, 0:WIDTH]
        k = sh[:, WIDTH:2 * WIDTH]
        v = sh[:, 2 * WIDTH:3 * WIDTH]
        lora = sh[:, 3 * WIDTH:3 * WIDTH + PAIR]
        hg = sh[:, 3 * WIDTH + PAIR:]
        lane_l = lax.broadcasted_iota(jnp.int32, lora.shape, 1)
        lora_act = jnp.where(lane_l < DECAY_LORA, jnp.tanh(lora), lora)
        wa = _dot(lora_act.astype(BF16), wlora_ref[...])
        w_log = -_softplus(-(w0_ref[...] + wa[:, :WIDTH])) - 0.5
        logw = -jnp.exp(w_log)
        a = _sigmoid(a0_ref[...] + wa[:, WIDTH:])
        g = _dot(_sigmoid(hg).astype(BF16), wg_ref[...])
        kkn = k * kk_ref[...]
        k2 = k * (1.0 + (a - 1.0) * ka_ref[...])
        for ck in range(n_rows // c):
            rs = slice(ck * c, (ck + 1) * c)
            cum_all = _mm(cumsum_b, _parts(logw[rs], 3))
            for pp in range(N_PAIRS):
                sl = slice(pp * PAIR, (pp + 1) * PAIR)
                chains.append(dict(bi=bi, pp=pp, ck=ck, rs=rs, sl=sl, r=r[rs, sl], k=k2[rs, sl], v=v[rs, sl],
                                   a=a[rs, sl], kk=kkn[rs, sl], lw=logw[rs, sl], cum=cum_all[:, sl],
                                   g=g[rs, sl]))

    for ch in chains:
        ch['ss'] = _mm(_parts(ch['kk'] * ch['kk'], 2), head_sum_b)
    for ch in chains:
        kap = ch['kk'] / jnp.maximum(jnp.sqrt(ch['ss']), 1e-12)
        cum = ch['cum']
        ch['pc'] = jnp.exp(cum[c - 1:c, :])
        inv = jnp.exp(-cum)
        rm = stack(ch['r'] * jnp.exp(cum))
        bm = stack(kap * jnp.exp(cum - ch['lw']))
        am = stack(-(kap * ch['a']) * inv)
        km = stack(ch['k'] * inv)
        ch.update(rm=rm, bm=bm, am=am, km=km, vm=stack(ch['v']))
    for ch in chains:
        ch['sb'] = _mm(_parts(jnp.concatenate([ch['bm'], ch['rm']], axis=0)),
                       _parts(jnp.concatenate([ch['am'], ch['km']], axis=0)), _nt)
    for ch in chains:
        sb = ch['sb']
        ch['la'] = jnp.where(strict_lower, sb[:2 * c, :2 * c], 0.0)
        ch['lk'] = jnp.where(strict_lower, sb[:2 * c, 2 * c:], 0.0)
        ch['ma'] = jnp.where(lower, sb[2 * c:, :2 * c], 0.0)
        ch['mk'] = jnp.where(lower, sb[2 * c:, 2 * c:], 0.0)
        ch['tinv'] = eye + ch['la']
        ch['lpow'] = ch['la']
    n = 2
    while n < c:
        for ch in chains:
            lp = _parts(ch['lpow'])
            ch['lpow'] = _mm(lp, lp)
        for ch in chains:
            ch['tinv'] = ch['tinv'] + _mm(_parts(ch['tinv']), _parts(ch['lpow']))
        n *= 2

    state = {(bi, pp): state_ref[bi, pp] for bi in range(nbatch) for pp in range(N_PAIRS)}
    for ck in sorted({ch['ck'] for ch in chains}):
        sub = [ch for ch in chains if ch['ck'] == ck]
        for ch in sub:
            ch['h0'] = state[(ch['bi'], ch['pp'])]
            ch['rhs'] = _mm(_parts(jnp.concatenate([ch['bm'], ch['lk']], axis=1)),
                            _parts(jnp.concatenate([ch['h0'], ch['vm']], axis=0)))
        for ch in sub:
            ch['u'] = _mm(_parts(ch['tinv']), _parts(ch['rhs']))
        for ch in sub:
            yst = _mm(_parts(jnp.concatenate([ch['rm'], ch['ma'], ch['mk']], axis=1)),
                      _parts(jnp.concatenate([ch['h0'], ch['u'], ch['vm']], axis=0)))
            ch['y'] = yst[:c] + yst[c:]
        for ch in sub:
            pc = ch['pc']
            pc_col = jnp.transpose(jnp.broadcast_to(pc, (PAIR, PAIR)))
            upd = _mm(_parts(jnp.concatenate([jnp.transpose(ch['am'] * pc), jnp.transpose(ch['km'] * pc)],
                                             axis=1)),
                      _parts(jnp.concatenate([ch['u'], ch['vm']], axis=0)))
            state[(ch['bi'], ch['pp'])] = ch['h0'] * pc_col + upd
    for (bi, pp), value in state.items():
        state_ref[bi, pp] = value
    for ch in chains:
        ch['mean'] = _mm(_parts(ch['y'], 2), head_sum_b) * (1.0 / HEAD_DIM)
        ch['bonus'] = _mm(_parts(ch['r'] * ch['k'] * rk_ref[:, ch['sl']], 2), head_sum_b) * ch['v']
    for ch in chains:
        yc = ch['y'] - ch['mean']
        ch['yc'] = yc
        ch['var'] = _mm(_parts(yc * yc, 2), head_sum_b) * (1.0 / HEAD_DIM)
    for ch in chains:
        sl = ch['sl']
        yn = ch['yc'] * lax.rsqrt(ch['var'] + GN_EPS) * lng_ref[:, sl] + lnb_ref[:, sl]
        y_ref[ch['bi'], ch['rs'], sl] = ((yn + ch['bonus']) * ch['g']).astype(y_ref.dtype)


def _rwkv(proj_rw, mu, w0, a0, wlora, wg, k_k, k_a, r_k, lnx_g, lnx_b):
    b, s, n = proj_rw.shape
    c = RWKV_CHUNK * RWKV_CHUNKS_PER_STEP
    vec = lambda width: pl.BlockSpec((1, width), lambda ci: (0, 0))
    return pl.pallas_call(
        _rwkv_kernel,
        grid=(s // c,),
        in_specs=[pl.BlockSpec((b, c, n), lambda ci: (0, ci, 0)),
                  vec(n), vec(WIDTH), vec(WIDTH),
                  pl.BlockSpec(wlora.shape, lambda ci: (0, 0)),
                  pl.BlockSpec(wg.shape, lambda ci: (0, 0)),
                  vec(WIDTH), vec(WIDTH), vec(WIDTH), vec(WIDTH), vec(WIDTH)],
        out_specs=pl.BlockSpec((b, c, WIDTH), lambda ci: (0, ci, 0)),
        out_shape=jax.ShapeDtypeStruct((b, s, WIDTH), BF16),
        scratch_shapes=[pltpu.VMEM((b, 8, n), F32),
                        pltpu.VMEM((b, N_PAIRS, PAIR, PAIR), F32)],
        compiler_params=pltpu.CompilerParams(dimension_semantics=("arbitrary",),
                                             vmem_limit_bytes=VMEM_LIMIT),
    )(proj_rw, mu, w0, a0, wlora, wg, k_k, k_a, r_k, lnx_g, lnx_b)


MOBA_HEADS_PER_STEP = 4
MOBA_VMEM_LIMIT = 60 * 1024 * 1024


def _key_group(nb):
    return min(KEY_GROUP, nb)


def _moba_kernel(qt_ref, k_ref, vt_ref, km_ref, o_ref, sa_ref, sb_ref, mxa_ref, mxb_ref, *, nb):
    bs = MOBA_BLOCK
    heads = range(qt_ref.shape[1])
    i = pl.program_id(2)
    blk = lax.broadcasted_iota(jnp.int32, (nb, bs), 0).astype(F32)

    q_sel, q_own = [], []
    for h in heads:
        qt = qt_ref[0, h]
        km = km_ref[0, h].astype(BF16)
        gate = jnp.where(blk < i.astype(F32), _dot(km, qt), NEG_INF)
        sel = jnp.zeros((nb, bs), F32)
        for _ in range(MOBA_TOPK):
            mx = jnp.max(gate, axis=0, keepdims=True)
            hit = (gate == mx) & (mx > NEG_INF)
            idx = jnp.min(jnp.where(hit, blk, float(nb)), axis=0, keepdims=True)
            pick = blk == idx
            sel = jnp.where(pick, 1.0, sel)
            gate = jnp.where(pick, NEG_INF, gate)
        bias = jnp.where(sel > 0.0, 0.0, MASK_BIAS)
        if nb < HEAD_DIM:
            bias = jnp.concatenate([bias, jnp.zeros((HEAD_DIM - nb, bs), F32)], axis=0)
        q_sel.append(jnp.concatenate([qt, bias.astype(BF16)], axis=0))
        q_own.append(jnp.concatenate([qt, jnp.zeros((HEAD_DIM, bs), BF16)], axis=0))

    span = _key_group(nb) * bs
    n_groups = lax.shift_right_logical(i * bs + span - 1, span.bit_length() - 1)

    def col_max(sc):
        return jnp.max(jnp.max(sc.reshape(span // bs, bs, bs), axis=0), axis=0, keepdims=True)

    even, odd = (sa_ref, mxa_ref), (sb_ref, mxb_ref)

    def put(bufs, g):
        off = pl.multiple_of(g * span, span)
        for h in heads:
            s_new = _dot(k_ref[0, h, pl.ds(off, span), :], q_sel[h])
            bufs[0][h] = s_new
            bufs[1][h] = col_max(s_new)

    def absorb(bufs, g, carry):
        off = pl.multiple_of(g * span, span)
        out = []
        for h in heads:
            m, acc = carry[2 * h], carry[2 * h + 1]
            m_new = jnp.maximum(m, bufs[1][h])
            p = jnp.exp2(bufs[0][h] - m_new)
            acc = jnp.exp2(m - m_new) * acc + _dot(vt_ref[0, h, :, pl.ds(off, span)], p.astype(BF16))
            out += [m_new, acc]
        return tuple(out)

    start = pl.multiple_of(i * bs, bs)
    s_own = [_dot(k_ref[0, h, pl.ds(start, bs), :], q_own[h]) for h in heads]
    put(even, 0)

    key_i = lax.broadcasted_iota(jnp.int32, (bs, bs), 0)
    qry_i = lax.broadcasted_iota(jnp.int32, (bs, bs), 1)
    carry = []
    for h in heads:
        s = jnp.where(key_i <= qry_i, s_own[h], NEG_INF)
        m = jnp.max(s, axis=0, keepdims=True)
        p = jnp.exp2(s - m)
        carry += [m, _dot(vt_ref[0, h, :, pl.ds(start, bs)], p.astype(BF16))]
    carry = tuple(carry)

    def pair(t, carry):
        put(odd, 2 * t + 1)
        carry = absorb(even, 2 * t, carry)
        put(even, 2 * t + 2)
        return absorb(odd, 2 * t + 1, carry)

    n_pairs = lax.shift_right_logical(jnp.maximum(n_groups - 1, 0), 1)
    carry = lax.fori_loop(0, n_pairs, pair, carry)
    last_even = 2 * n_pairs

    def tail_two(*carry):
        put(odd, last_even + 1)
        return absorb(odd, last_even + 1, absorb(even, last_even, carry))

    def tail_one(*carry):
        return absorb(even, last_even, carry)

    def tail(*carry):
        return lax.cond(n_groups - last_even == 2, tail_two, tail_one, *carry)

    carry = lax.cond(n_groups > 0, tail, lambda *carry: carry, *carry)
    out_t = [carry[2 * h + 1][:HEAD_DIM] / carry[2 * h + 1][HEAD_DIM:HEAD_DIM + 1] for h in heads]
    o_ref[0] = jnp.transpose(jnp.concatenate(out_t, axis=0)).astype(o_ref.dtype)


def _moba(qt, kaug, vaug, kmean):
    b, nh, _, s = qt.shape
    nb = s // MOBA_BLOCK
    group = _key_group(nb)
    hp = MOBA_HEADS_PER_STEP
    assert nb % group == 0 and nh % hp == 0 and (hp * HEAD_DIM) % PAIR == 0
    return pl.pallas_call(
        functools.partial(_moba_kernel, nb=nb),
        grid=(b, nh // hp, nb),
        in_specs=[pl.BlockSpec((1, hp, HEAD_DIM, MOBA_BLOCK), lambda bi, hi, qi: (bi, hi, 0, qi)),
                  pl.BlockSpec((1, hp, s, PAIR), lambda bi, hi, qi: (bi, hi, 0, 0),
                               pipeline_mode=pl.Buffered(1)),
                  pl.BlockSpec((1, hp, VALUE_ROWS, s), lambda bi, hi, qi: (bi, hi, 0, 0),
                               pipeline_mode=pl.Buffered(1)),
                  pl.BlockSpec((1, hp, nb, HEAD_DIM), lambda bi, hi, qi: (bi, hi, 0, 0))],
        out_specs=pl.BlockSpec((1, MOBA_BLOCK, hp * HEAD_DIM), lambda bi, hi, qi: (bi, qi, hi)),
        out_shape=jax.ShapeDtypeStruct((b, s, nh * HEAD_DIM), BF16),
        scratch_shapes=[pltpu.VMEM((hp, group * MOBA_BLOCK, MOBA_BLOCK), F32),
                        pltpu.VMEM((hp, group * MOBA_BLOCK, MOBA_BLOCK), F32),
                        pltpu.VMEM((hp, 1, MOBA_BLOCK), F32),
                        pltpu.VMEM((hp, 1, MOBA_BLOCK), F32)],
        compiler_params=pltpu.CompilerParams(
            dimension_semantics=("parallel", "parallel", "arbitrary"),
            vmem_limit_bytes=MOBA_VMEM_LIMIT),
    )(qt, kaug, vaug, kmean)


def _layer_norm(z, g, b):
    mu = jnp.mean(z, axis=1, keepdims=True)
    zc = z - mu
    var = jnp.mean(zc * zc, axis=1, keepdims=True)
    return zc * lax.rsqrt(var + LN_EPS) * g + b


def _merge_kernel(x_ref, ya_ref, yr_ref, wgate_ref, wab_ref, wrb_ref, wout_ref, g1_ref, b1_ref,
                  wr_hi_ref, wr_lo_ref, br_ref, h_ref, idx_ref, gate_ref, *, alpha):
    x = x_ref[...]
    d = x.shape[1]
    gates = _sigmoid(_dot(x.astype(BF16), wgate_ref[...]))
    mixed = (gates[:, :d] * _dot(ya_ref[...], wab_ref[...])
             + gates[:, d:] * _dot(yr_ref[...], wrb_ref[...]))
    h = _layer_norm(alpha * x + _dot(mixed.astype(BF16), wout_ref[...]), g1_ref[...], b1_ref[...])
    h_ref[...] = h

    logits = _mm(_parts(h, 2), [wr_hi_ref[...], wr_lo_ref[...]]) + br_ref[...]
    col = lax.broadcasted_iota(jnp.int32, logits.shape, 1).astype(F32)
    idx_out = jnp.zeros(logits.shape, F32)
    val_out = jnp.zeros(logits.shape, F32)
    top = None
    denom = None
    for t in range(TOP_K):
        mx = jnp.max(logits, axis=1, keepdims=True)
        idx = jnp.min(jnp.where(logits == mx, col, float(ROUTER_PAD)), axis=1, keepdims=True)
        if t == 0:
            top = mx
        e = jnp.exp(mx - top)
        denom = e if t == 0 else denom + e
        idx_out = jnp.where(col == float(t), idx, idx_out)
        val_out = jnp.where(col == float(t), e, val_out)
        logits = jnp.where(col == idx, NEG_INF, logits)
    idx_ref[...] = idx_out.astype(jnp.int32)
    gate_ref[...] = val_out / denom


def _merge(x2, ya, yr, wgate, wab, wrb, wout, g1, b1, wr, br, alpha, tm=512):
    wr_hi, wr_lo = _parts(wr, 2)
    t, d = x2.shape
    row = lambda width: pl.BlockSpec((tm, width), lambda i: (i, 0))
    full = lambda arr: pl.BlockSpec(arr.shape, lambda i: (0, 0))
    return pl.pallas_call(
        functools.partial(_merge_kernel, alpha=alpha),
        grid=(t // tm,),
        in_specs=[row(d), row(WIDTH), row(WIDTH), full(wgate), full(wab), full(wrb), full(wout),
                  full(g1), full(b1), full(wr_hi), full(wr_lo), full(br)],
        out_specs=[row(d), row(ROUTER_PAD), row(ROUTER_PAD)],
        out_shape=[jax.ShapeDtypeStruct((t, d), F32),
                   jax.ShapeDtypeStruct((t, ROUTER_PAD), jnp.int32),
                   jax.ShapeDtypeStruct((t, ROUTER_PAD), F32)],
        compiler_params=pltpu.CompilerParams(dimension_semantics=("parallel",),
                                             vmem_limit_bytes=VMEM_LIMIT),
    )(x2, ya, yr, wgate, wab, wrb, wout, g1, b1, wr_hi, wr_lo, br)


DEINTERLEAVE_GROUP = 256


def _deinterleave_kernel(w_ref, perm_ref, o_ref):
    g = DEINTERLEAVE_GROUP
    n = w_ref.shape[2]
    half = n // 2
    for c in range(n // g):
        res = _dot(w_ref[0, :, c * g:(c + 1) * g].astype(BF16), perm_ref[...])
        o_ref[0, :, c * g // 2:(c + 1) * g // 2] = res[:, :g // 2].astype(BF16)
        o_ref[0, :, half + c * g // 2:half + (c + 1) * g // 2] = res[:, g // 2:].astype(BF16)


def _deinterleave(w1, tr=512):
    e, d, n = w1.shape
    g = DEINTERLEAVE_GROUP
    src = jnp.arange(g)
    dst = jnp.where(src % 2 == 0, src // 2, g // 2 + src // 2)
    perm = (dst[:, None] == jnp.arange(g)[None, :]).astype(BF16)
    return pl.pallas_call(
        _deinterleave_kernel,
        grid=(e, d // tr),
        in_specs=[pl.BlockSpec((1, tr, n), lambda ei, ri: (ei, ri, 0)),
                  pl.BlockSpec((g, g), lambda ei, ri: (0, 0))],
        out_specs=pl.BlockSpec((1, tr, n), lambda ei, ri: (ei, ri, 0)),
        out_shape=jax.ShapeDtypeStruct((e, d, n), BF16),
        compiler_params=pltpu.CompilerParams(dimension_semantics=("parallel", "parallel"),
                                             vmem_limit_bytes=VMEM_LIMIT),
    )(w1, perm)

def _row_copy(src_hbm, dst_buf, sem, src_row, slot, dst_row):
    return pltpu.make_async_copy(src_hbm.at[pl.ds(src_row, 1), :],
                                 dst_buf.at[slot, pl.ds(dst_row, 1), :],
                                 sem.at[slot])


GATHER_UNROLL = 16


def _gather_rows(src_hbm, dst_buf, sem, idx_ref, slot, n_rows):
    def issue(run, _):
        base = pl.multiple_of(run * GATHER_UNROLL, GATHER_UNROLL)
        for j in range(GATHER_UNROLL):
            _row_copy(src_hbm, dst_buf, sem, idx_ref[0, 0, base + j], slot, base + j).start(priority=j % 2)
        return 0
    lax.fori_loop(0, n_rows // GATHER_UNROLL, issue, 0)


def _wait_rows(src_hbm, dst_buf, sem, slot, n_rows):
    pltpu.make_async_copy(src_hbm.at[pl.ds(0, n_rows), :], dst_buf.at[slot], sem.at[slot]).wait()


ROUTE_ROWS = 512
DISPATCH_ROWS = 256


def _route_kernel(idx_ref, dest_ref, cnt_ref, run_ref, start_ref, *, tm):
    phase = pl.program_id(0)
    i = pl.program_id(1)
    idx = idx_ref[...]
    lane = lax.broadcasted_iota(jnp.int32, idx.shape, 1)
    hot = [jnp.where(lane == idx[:, k:k + 1], 1.0, 0.0) for k in range(TOP_K)]
    cnt = hot[0] + hot[1] + hot[2] + hot[3]
    tile_total = jnp.sum(cnt, axis=0, keepdims=True)

    @pl.when((phase == 0) & (i == 0))
    def _():
        run_ref[...] = jnp.zeros_like(run_ref)

    @pl.when(phase == 0)
    def _():
        run_ref[...] += tile_total
        dest_ref[...] = jnp.zeros_like(dest_ref)

    @pl.when((phase == 1) & (i == 0))
    def _():
        counts = run_ref[...]
        padded = jnp.floor((counts + (EXPERT_ROWS - 1)) * (1.0 / EXPERT_ROWS)) * EXPERT_ROWS
        ri = lax.broadcasted_iota(jnp.int32, (ROUTER_PAD, ROUTER_PAD), 0)
        ci = lax.broadcasted_iota(jnp.int32, (ROUTER_PAD, ROUTER_PAD), 1)
        before = jnp.where(ri < ci, 1.0, 0.0).astype(BF16)
        start = _mm(_parts(jnp.broadcast_to(padded, (8, ROUTER_PAD)), 3), [before])
        start_ref[...] = start[0:1]
        cnt_ref[...] = counts
        run_ref[...] = jnp.zeros_like(run_ref)

    @pl.when(phase == 1)
    def _():
        rt = lax.broadcasted_iota(jnp.int32, (tm, tm), 0)
        ct = lax.broadcasted_iota(jnp.int32, (tm, tm), 1)
        earlier = jnp.where(ct < rt, 1.0, 0.0).astype(BF16)
        pos = start_ref[...] + run_ref[...] + _dot(earlier, cnt.astype(BF16))
        dest = jnp.zeros(idx.shape, F32)
        for k in range(TOP_K):
            d_k = jnp.sum(hot[k] * pos, axis=1, keepdims=True)
            dest = jnp.where(lane == k, d_k, dest)
            pos = pos + hot[k]
        dest_ref[...] = dest.astype(jnp.int32)
        run_ref[...] += tile_total


def _route(idx_pad):
    t = idx_pad.shape[0]
    tm = ROUTE_ROWS
    return pl.pallas_call(
        functools.partial(_route_kernel, tm=tm),
        grid=(2, t // tm),
        in_specs=[pl.BlockSpec((tm, ROUTER_PAD), lambda ph, i: (i, 0))],
        out_specs=[pl.BlockSpec((tm, ROUTER_PAD), lambda ph, i: (ph * i, 0)),
                   pl.BlockSpec((1, ROUTER_PAD), lambda ph, i: (0, 0))],
        out_shape=[jax.ShapeDtypeStruct((t, ROUTER_PAD), jnp.int32),
                   jax.ShapeDtypeStruct((1, ROUTER_PAD), F32)],
        scratch_shapes=[pltpu.VMEM((1, ROUTER_PAD), F32), pltpu.VMEM((1, ROUTER_PAD), F32)],
        compiler_params=pltpu.CompilerParams(dimension_semantics=("arbitrary", "arbitrary"),
                                             vmem_limit_bytes=VMEM_LIMIT),
    )(idx_pad)


def _dispatch_kernel(pad_end_ref, dest_ref, h_ref, xs_hbm, zero_ref, sem, *, tm):
    @pl.when(pl.program_id(0) == 0)
    def _():
        zero_ref[...] = jnp.zeros_like(zero_ref)

        def last_block(e):
            end = pad_end_ref[e]
            start = pl.multiple_of(jnp.maximum(end - EXPERT_ROWS, 0), EXPERT_ROWS)
            return pltpu.make_async_copy(zero_ref, xs_hbm.at[pl.ds(start, EXPERT_ROWS), :], sem.at[0])

        for e in range(N_EXPERTS):
            last_block(e).start()
        for e in range(N_EXPERTS):
            last_block(e).wait()

        def unused_block(b):
            start = pl.multiple_of(b * EXPERT_ROWS, EXPERT_ROWS)
            return pltpu.make_async_copy(zero_ref, xs_hbm.at[pl.ds(start, EXPERT_ROWS), :], sem.at[0])

        first_unused = lax.shift_right_logical(pad_end_ref[N_EXPERTS - 1], EXPERT_ROWS.bit_length() - 1)
        n_blocks = xs_hbm.shape[0] // EXPERT_ROWS

        def start_one(b, _):
            unused_block(b).start()
            return 0

        def wait_one(b, _):
            unused_block(b).wait()
            return 0

        lax.fori_loop(first_unused, n_blocks, start_one, 0)
        lax.fori_loop(first_unused, n_blocks, wait_one, 0)

    run_len = 8

    def issue(run, _):
        base = pl.multiple_of(run * run_len, run_len)
        for j in range(run_len):
            for k in range(TOP_K):
                pltpu.make_async_copy(h_ref.at[pl.ds(base + j, 1), :],
                                      xs_hbm.at[pl.ds(dest_ref[0, 0, TOP_K * (base + j) + k], 1), :],
                                      sem.at[0]).start(priority=k % 2)
        return 0

    lax.fori_loop(0, tm // run_len, issue, 0)
    for _ in range(TOP_K):
        pltpu.make_async_copy(h_ref, xs_hbm.at[pl.ds(0, tm), :], sem.at[0]).wait()


def _dispatch(pad_end, dest, h, n_rows):
    t, d = h.shape
    tm = DISPATCH_ROWS
    n = TOP_K * tm
    grid_spec = pltpu.PrefetchScalarGridSpec(
        num_scalar_prefetch=1,
        grid=(t // tm,),
        in_specs=[pl.BlockSpec((1, 1, n), lambda i, pe: (i, 0, 0), memory_space=pltpu.SMEM),
                  pl.BlockSpec((tm, d), lambda i, pe: (i, 0))],
        out_specs=pl.BlockSpec(memory_space=pl.ANY),
        scratch_shapes=[pltpu.VMEM((EXPERT_ROWS, d), F32), pltpu.SemaphoreType.DMA((1,))],
    )
    return pl.pallas_call(
        functools.partial(_dispatch_kernel, tm=tm),
        grid_spec=grid_spec,
        out_shape=jax.ShapeDtypeStruct((n_rows, d), F32),
        compiler_params=pltpu.CompilerParams(dimension_semantics=("arbitrary",),
                                             vmem_limit_bytes=VMEM_LIMIT),
    )(pad_end, dest.reshape(t // tm, 1, n), h)


def _expert_kernel(be_ref, nact_ref, x_ref, w1_ref, b1_ref, w2_ref, b2_ref, y_ref):
    del be_ref

    @pl.when(pl.program_id(0) < nact_ref[0])
    def _():
        f = w2_ref.shape[1]
        hid = _dot(x_ref[...].astype(BF16), w1_ref[0]) + b1_ref[0]
        x_glu = jnp.minimum(hid[:, :f], SWIGLU_LIMIT)
        x_lin = jnp.clip(hid[:, f:], -SWIGLU_LIMIT, SWIGLU_LIMIT)
        act = x_glu * _sigmoid(SWIGLU_ALPHA * x_glu) * (x_lin + 1.0)
        y_ref[...] = _dot(act.astype(BF16), w2_ref[0]) + b2_ref[0]

    @pl.when(pl.program_id(0) >= nact_ref[0])
    def _():
        y_ref[...] = jnp.zeros_like(y_ref)


def _experts(block_expert, n_active, x_sorted, w1p, b1p, w2b, b2):
    n_rows, d = x_sorted.shape
    rows = EXPERT_ROWS
    nblk = n_rows // rows
    f = w2b.shape[1]
    grid_spec = pltpu.PrefetchScalarGridSpec(
        num_scalar_prefetch=2,
        grid=(nblk,),
        in_specs=[
            pl.BlockSpec((rows, d), lambda b, be, na: (jnp.minimum(b, jnp.maximum(na[0] - 1, 0)), 0)),
            pl.BlockSpec((1, d, 2 * f), lambda b, be, na: (be[b], 0, 0)),
            pl.BlockSpec((1, 1, 2 * f), lambda b, be, na: (be[b], 0, 0)),
            pl.BlockSpec((1, f, d), lambda b, be, na: (be[b], 0, 0)),
            pl.BlockSpec((1, 1, d), lambda b, be, na: (be[b], 0, 0)),
        ],
        out_specs=pl.BlockSpec((rows, d), lambda b, be, na: (b, 0)),
    )
    return pl.pallas_call(
        _expert_kernel,
        grid_spec=grid_spec,
        out_shape=jax.ShapeDtypeStruct((n_rows, d), F32),
        compiler_params=pltpu.CompilerParams(dimension_semantics=("arbitrary",),
                                             vmem_limit_bytes=VMEM_LIMIT),
    )(block_expert, n_active, x_sorted, w1p, b1p, w2b, b2)


def _combine_kernel(dest_ref, dest_next_ref, h_ref, gate_ref, g2_ref, b2_ref, y_hbm, o_ref, ybuf, sem,
                    *, alpha, nsteps):
    tm = COMBINE_ROWS
    n = TOP_K * tm
    s = pl.program_id(0)
    slot = s % 2

    @pl.when(s == 0)
    def _():
        _gather_rows(y_hbm, ybuf, sem, dest_ref, 0, n)

    @pl.when(s + 1 < nsteps)
    def _():
        _gather_rows(y_hbm, ybuf, sem, dest_next_ref, 1 - slot, n)

    _wait_rows(y_hbm, ybuf, sem, slot, n)
    gates = gate_ref[...]
    moe = gates[:, 0:1] * ybuf[slot, 0:tm, :]
    for kk in range(1, TOP_K):
        moe = moe + gates[:, kk:kk + 1] * ybuf[slot, kk * tm:(kk + 1) * tm, :]
    o_ref[...] = _layer_norm(alpha * h_ref[...] + moe, g2_ref[...], b2_ref[...])


def _combine(dest_km, h, gate_pad, g2, b2, y_sorted, alpha):
    t, d = h.shape
    tm = COMBINE_ROWS
    nsteps = t // tm
    n = TOP_K * tm
    return pl.pallas_call(
        functools.partial(_combine_kernel, alpha=alpha, nsteps=nsteps),
        grid=(nsteps,),
        in_specs=[
            pl.BlockSpec((1, 1, n), lambda s: (s, 0, 0), memory_space=pltpu.SMEM),
            pl.BlockSpec((1, 1, n), lambda s: (jnp.minimum(s + 1, nsteps - 1), 0, 0),
                         memory_space=pltpu.SMEM),
            pl.BlockSpec((tm, d), lambda s: (s, 0)),
            pl.BlockSpec((tm, ROUTER_PAD), lambda s: (s, 0)),
            pl.BlockSpec((1, d), lambda s: (0, 0)),
            pl.BlockSpec((1, d), lambda s: (0, 0)),
            pl.BlockSpec(memory_space=pl.ANY),
        ],
        out_specs=pl.BlockSpec((tm, d), lambda s: (s, 0)),
        out_shape=jax.ShapeDtypeStruct((t, d), F32),
        scratch_shapes=[pltpu.VMEM((2, n, d), F32), pltpu.SemaphoreType.DMA((2,))],
        compiler_params=pltpu.CompilerParams(dimension_semantics=("arbitrary",),
                                             vmem_limit_bytes=VMEM_LIMIT),
    )(dest_km, dest_km, h, gate_pad, g2, b2, y_sorted)


def _block_tables(counts, nblk):
    rows = EXPERT_ROWS
    counts = counts.astype(jnp.int32)
    padded = (counts + rows - 1) // rows * rows
    pad_ends = jnp.cumsum(padded)
    block_start = jnp.arange(nblk, dtype=jnp.int32) * rows
    block_expert = jnp.sum((pad_ends[None, :] <= block_start[:, None]).astype(jnp.int32), axis=1)
    block_expert = jnp.minimum(block_expert, N_EXPERTS - 1).astype(jnp.int32)
    n_active = (pad_ends[-1] // rows).astype(jnp.int32).reshape(1)
    return block_expert, n_active, pad_ends.astype(jnp.int32)


def _layer(x, w_in, mu_shift, w0, w_decay_up, a0, w_aaa_up, w_gate_up, k_k, k_a, r_k, lnx_g, lnx_b,
           w_attn_br, w_rwkv_br, w_out, ln1_g, ln1_b, w_router, b_router, w1, b1, w2, b2, ln2_g, ln2_b,
           alpha):
    b, s, d = x.shape
    t = b * s
    x2 = x.reshape(t, d)
    xb = x2.astype(BF16)
    row = lambda vec: vec.reshape(1, -1)

    off_q = RW_COLS
    off_gate = off_q + 3 * WIDTH
    pad_cols = RW_COLS_PAD - RW_COLS
    w_rw = jnp.pad(w_in[:, :RW_COLS], ((0, 0), (0, pad_cols))).astype(BF16)
    mu = jnp.pad(mu_shift, (0, pad_cols)).reshape(1, -1)
    w_qkv = w_in[:, off_q:off_gate].astype(BF16)
    w_gate = w_in[:, off_gate:].astype(BF16)

    half = HEAD_DIM // 2
    inv_freq = ROPE_THETA ** (-jnp.arange(half, dtype=F32) / half)
    ang = jnp.arange(s, dtype=F32)[:, None] * inv_freq[None, :]
    cos, sin = jnp.cos(ang), jnp.sin(ang)
    cos_t = jnp.concatenate([cos, cos, cos, cos], axis=1)
    sin_t = jnp.concatenate([-sin, sin, -sin, sin], axis=1)

    proj_rw = _project(xb, w_rw).reshape(b, s, RW_COLS_PAD)
    qt, kaug, vaug, kmean = _project_qkv(xb, w_qkv, cos_t, sin_t, b, s)
    kmean = kmean.reshape(b, s // MOBA_BLOCK, N_HEADS, HEAD_DIM).transpose(0, 2, 1, 3)

    zeros = jnp.zeros((DECAY_LORA, WIDTH), F32)
    w_lora = jnp.concatenate([jnp.concatenate([w_decay_up, zeros], axis=1),
                              jnp.concatenate([zeros, w_aaa_up], axis=1)], axis=0).astype(BF16)
    w_g = jnp.pad(w_gate_up, ((0, GATE_LORA_PAD - GATE_LORA), (0, 0))).astype(BF16)
    y_rwkv = _rwkv(proj_rw, mu, row(w0), row(a0), w_lora, w_g, row(k_k), row(k_a), row(r_k),
                   row(lnx_g), row(lnx_b))
    y_attn = _moba(qt, kaug, vaug, kmean)

    w_r = jnp.pad(w_router, ((0, 0), (0, ROUTER_PAD - N_EXPERTS)))
    b_r = jnp.pad(b_router, (0, ROUTER_PAD - N_EXPERTS), constant_values=NEG_INF).reshape(1, -1)
    h, idx_pad, gate_pad = _merge(x2, y_attn.reshape(t, WIDTH), y_rwkv.reshape(t, WIDTH), w_gate,
                                  w_attn_br.astype(BF16), w_rwkv_br.astype(BF16), w_out.astype(BF16),
                                  row(ln1_g), row(ln1_b), w_r, b_r, alpha)

    dest_pad, counts = _route(idx_pad)
    dest = dest_pad[:, :TOP_K]
    nblk = t * TOP_K // EXPERT_ROWS + N_EXPERTS
    block_expert, n_active, pad_end = _block_tables(counts[0, :N_EXPERTS], nblk)
    x_sorted = _dispatch(pad_end, dest, h, nblk * EXPERT_ROWS)
    w1p = _deinterleave(w1)
    b1p = jnp.concatenate([b1[:, 0::2], b1[:, 1::2]], axis=1)[:, None, :]
    y_sorted = _experts(block_expert, n_active, x_sorted, w1p, b1p, w2.astype(BF16), b2[:, None, :])

    tm = COMBINE_ROWS
    dest_km = dest.reshape(t // tm, tm, TOP_K).transpose(0, 2, 1).reshape(t // tm, 1, TOP_K * tm)
    out = _combine(dest_km, h, gate_pad, row(ln2_g), row(ln2_b), y_sorted, alpha)
    return out.reshape(b, s, d)


def kernel(x, w_in, mu_shift, w0, w_decay_up, a0, w_aaa_up, w_gate_up, k_k, k_a, r_k, lnx_g, lnx_b,
           w_attn_br, w_rwkv_br, w_out, ln1_g, ln1_b, w_router, b_router, w1, b1, w2, b2, ln2_g, ln2_b):
    depth = w_in.shape[0]
    alpha = (2 * depth) ** 0.25
    for l in range(depth):
        x = _layer(x, w_in[l], mu_shift[l], w0[l], w_decay_up[l], a0[l], w_aaa_up[l], w_gate_up[l],
                   k_k[l], k_a[l], r_k[l].reshape(-1), lnx_g[l], lnx_b[l], w_attn_br[l], w_rwkv_br[l],
                   w_out[l], ln1_g[l], ln1_b[l], w_router[l], b_router[l], w1[l], b1[l], w2[l], b2[l],
                   ln2_g[l], ln2_b[l], alpha)
    return x
```
